```python
import jax
import jax.numpy as jnp
from jax import lax
import numpy as np

D_MODEL = 1024
BATCH = 16
SEQ = 2048
DEPTH = 2
DEC_BATCH = 128
DEC_SEQ = 8
PAST_LEN = 16384
PAGE_SIZE = 128

N_GROUPS = 4
GROUP_W = D_MODEL // N_GROUPS
HEAD_DIM = 64
GROUP_HEADS = GROUP_W // HEAD_DIM

MLA_HEADS = GROUP_HEADS
Q_LORA = 3 * GROUP_W // 4
KV_LORA = GROUP_W // 2
QK_NOPE = HEAD_DIM
QK_ROPE = HEAD_DIM // 2
V_DIM = HEAD_DIM
ROPE_BASE = 10000.0
Q_BLOCK = 128
MLA_SCALE = (QK_NOPE + QK_ROPE) ** -0.5

CONV_W = 31
CONV_DIM = GROUP_W
CONV_GROUPS = GROUP_HEADS

CHUNK = 128
SGU_DIM = GROUP_W
SGU_HEADS = GROUP_HEADS

RW_DIM = GROUP_W
RW_HEADS = GROUP_HEADS
RW_N = HEAD_DIM
DECAY_LORA = 64
AAA_LORA = 64
GATE_LORA = 128

N_MEM = 256
X_HEADS = 4
X_HEAD_DIM = 128

D_FF = ((8 * D_MODEL + 3 * 256 - 1) // (3 * 256)) * 256

MLA_COLS = Q_LORA + KV_LORA + QK_ROPE
CONV_COLS = 2 * CONV_DIM
SGU_COLS = 2 * SGU_DIM
RW_COLS = 3 * RW_DIM + DECAY_LORA + AAA_LORA + GATE_LORA
IN_COLS = MLA_COLS + CONV_COLS + SGU_COLS + RW_COLS

EPS = 1e-6
LN_EPS = 1e-5
RW_LN_EPS = 64e-5
NEG = -1e30

kernel_name = 'hymba_mla_conformer_gmlp_rwkv7_step'


def rmsnorm(x, g, eps=EPS):
    xf = x.astype(jnp.float32)
    y = xf * lax.rsqrt(jnp.mean(xf * xf, axis=-1, keepdims=True) + eps)
    return (y * g.astype(jnp.float32)).astype(x.dtype)


def layernorm(x, g, b, eps=LN_EPS):
    xf = x.astype(jnp.float32)
    xc = xf - jnp.mean(xf, axis=-1, keepdims=True)
    y = xc * lax.rsqrt(jnp.mean(xc * xc, axis=-1, keepdims=True) + eps)
    return (y * g.astype(jnp.float32) + b.astype(jnp.float32)).astype(x.dtype)


def rope(x, pos):
    half = x.shape[-1] // 2
    inv = jnp.power(ROPE_BASE, -jnp.arange(half, dtype=jnp.float32) / half)
    ang = pos.astype(jnp.float32)[:, None] * inv[None, :]
    cos = jnp.cos(ang)[None, :, None, :]
    sin = jnp.sin(ang)[None, :, None, :]
    xf = x.astype(jnp.float32)
    x1, x2 = xf[..., :half], xf[..., half:]
    return jnp.concatenate([x1 * cos - x2 * sin, x1 * sin + x2 * cos], axis=-1).astype(x.dtype)


def mla_queries_latent(pa, pos, p):
    B, T, _ = pa.shape
    cq = rmsnorm(pa[..., :Q_LORA], p['mla_q_norm'])
    ckv = rmsnorm(pa[..., Q_LORA:Q_LORA + KV_LORA], p['mla_kv_norm'])
    kpe = pa[..., Q_LORA + KV_LORA:].reshape(B, T, 1, QK_ROPE)
    q = (cq @ p['mla_w_uq']).reshape(B, T, MLA_HEADS, QK_NOPE + QK_ROPE)
    q_nope = rmsnorm(q[..., :QK_NOPE], p['mla_gq_nope'])
    q_pe = rope(rmsnorm(q[..., QK_NOPE:], p['mla_gq_rope']), pos)
    kpe = rope(rmsnorm(kpe, p['mla_gk_rope']), pos)[:, :, 0]
    return q_nope, q_pe, ckv, kpe


def mla_expand(ckv, p):
    lead = ckv.shape[:-1]
    k_nope = rmsnorm((ckv @ p['mla_w_uk']).reshape(*lead, MLA_HEADS, QK_NOPE), p['mla_gk_nope'])
    v = (ckv @ p['mla_w_uv']).reshape(*lead, MLA_HEADS, V_DIM)
    return k_nope, v


def mla_scores(q_nope, q_pe, k_nope, kpe):
    s = jnp.einsum('bqhd,bkhd->bhqk', q_nope, k_nope) + jnp.einsum('bqhr,bkr->bhqk', q_pe, kpe)
    return s.astype(jnp.float32) * MLA_SCALE


def mla_attend_prompt(q_nope, q_pe, k_nope, v, kpe):
    B, T = q_nope.shape[:2]
    nb = T // Q_BLOCK
    qn = q_nope.reshape(B, nb, Q_BLOCK, MLA_HEADS, QK_NOPE).swapaxes(0, 1)
    qr = q_pe.reshape(B, nb, Q_BLOCK, MLA_HEADS, QK_ROPE).swapaxes(0, 1)
    kpos = jnp.arange(T)

    def block(args):
        i, qn_b, qr_b = args
        s = mla_scores(qn_b, qr_b, k_nope, kpe)
        qpos = i * Q_BLOCK + jnp.arange(Q_BLOCK)
        s = jnp.where(kpos[None, :] <= qpos[:, None], s, NEG)
        pr = jax.nn.softmax(s, axis=-1)
        return jnp.einsum('bhqk,bkhd->bqhd', pr.astype(v.dtype), v)

    o = lax.map(block, (jnp.arange(nb), qn, qr))
    return o.swapaxes(0, 1).reshape(B, T, MLA_HEADS * V_DIM)


def mla_attend_sample(q_nope, q_pe, k_nope, v, kpe, cache_ckv, cache_kpe, page_table, layer, p):
    B, Q = q_nope.shape[:2]
    s = mla_scores(q_nope, q_pe, k_nope, kpe)
    s = jnp.where(jnp.tril(jnp.ones((Q, Q), dtype=bool)), s, NEG)
    m = jnp.max(s, axis=-1, keepdims=True)
    e = jnp.exp(s - m)
    l = jnp.sum(e, axis=-1, keepdims=True)
    acc = jnp.einsum('bhqk,bkhd->bhqd', e, v.astype(jnp.float32))

    def page_step(carry, phys):
        m, l, acc = carry
        ckv_p = cache_ckv[phys, layer]
        kpe_p = cache_kpe[phys, layer]
        kn_p, v_p = mla_expand(ckv_p, p)
        sp = mla_scores(q_nope, q_pe, kn_p, kpe_p)
        m_new = jnp.maximum(m, jnp.max(sp, axis=-1, keepdims=True))
        corr = jnp.exp(m - m_new)
        ep = jnp.exp(sp - m_new)
        l = l * corr + jnp.sum(ep, axis=-1, keepdims=True)
        acc = acc * corr + jnp.einsum('bhqk,bkhd->bhqd', ep, v_p.astype(jnp.float32))
        return (m_new, l, acc), None

    (m, l, acc), _ = lax.scan(page_step, (m, l, acc), page_table.T)
    o = (acc / l).astype(q_nope.dtype)
    return o.transpose(0, 2, 1, 3).reshape(B, Q, MLA_HEADS * V_DIM)


def conv_module(pb, conv_state, p):
    a, gate = pb[..., :CONV_DIM], pb[..., CONV_DIM:]
    glu = a * jax.nn.sigmoid(gate)
    xin = jnp.concatenate([conv_state.astype(glu.dtype), glu], axis=1)
    y = lax.conv_general_dilated(xin, p['conv_w'].astype(xin.dtype)[:, None, :],
                                 window_strides=(1,), padding='VALID',
                                 dimension_numbers=('NWC', 'WIO', 'NWC'),
                                 feature_group_count=CONV_DIM)
    y = y + p['conv_b']
    B, T, _ = y.shape
    gw = CONV_DIM // CONV_GROUPS
    y = layernorm(y.reshape(B, T, CONV_GROUPS, gw), p['conv_norm_g'].reshape(CONV_GROUPS, gw),
                  p['conv_norm_b'].reshape(CONV_GROUPS, gw)).reshape(B, T, CONV_DIM)
    y = jax.nn.silu(y) @ p['conv_pw']
    return y, xin[:, -(CONV_W - 1):]


def spatial_gating(pc, p):
    z = jax.nn.gelu(pc)
    u, v = z[..., :SGU_DIM], z[..., SGU_DIM:]
    v = layernorm(v, p['sgu_norm_g'], p['sgu_norm_b'])
    B, T, _ = v.shape
    L = min(T, CHUNK)
    nc = T // L
    w = p['sgu_w'][:, :L, :L] * jnp.tril(jnp.ones((L, L), p['sgu_w'].dtype))
    vc = v.reshape(B, nc, L, SGU_HEADS, SGU_DIM // SGU_HEADS)
    sv = jnp.einsum('hij,bcjhd->bcihd', w, vc) + p['sgu_b'][:, :L].T[None, None, :, :, None]
    return u * sv.reshape(B, T, SGU_DIM), v


def rwkv_mix(pd, shift_state, wkv_state, p):
    B, T, _ = pd.shape
    prev = jnp.concatenate([shift_state[:, None].astype(pd.dtype), pd[:, :-1]], axis=1)
    xs = pd + (prev - pd) * p['rw_mu']
    o1, o2, o3 = RW_DIM, 2 * RW_DIM, 3 * RW_DIM
    o4 = o3 + DECAY_LORA
    o5 = o4 + AAA_LORA
    r, k, v = xs[..., :o1], xs[..., o1:o2], xs[..., o2:o3]
    xw, xa, xg = xs[..., o3:o4], xs[..., o4:o5], xs[..., o5:]
    w = -jax.nn.softplus(-(p['rw_w0'] + jnp.tanh(xw) @ p['rw_w2'])) - 0.5
    decay = jnp.exp(-jnp.exp(w.astype(jnp.float32)))
    a = jax.nn.sigmoid(p['rw_a0'] + xa @ p['rw_a2'])
    g = jax.nn.sigmoid(xg) @ p['rw_g2']

    def hs(t):
        return t.reshape(B, T, RW_HEADS, RW_N)

    kk = hs(k * p['rw_kk']).astype(jnp.float32)
    kk = kk * lax.rsqrt(jnp.sum(kk * kk, axis=-1, keepdims=True) + 1e-12)
    k = k * (1.0 + (a - 1.0) * p['rw_ka'])
    rh, kh, vh, ah, dh = hs(r), hs(k), hs(v), hs(a), hs(decay)
    seqs = tuple(t.astype(jnp.float32).transpose(1, 0, 2, 3) for t in (rh, dh, kh, vh, kk, ah))

    def step(S, inp):
        r_t, w_t, k_t, v_t, kk_t, a_t = inp
        sa = jnp.einsum('bhij,bhj->bhi', S, -kk_t)
        S = (S * w_t[:, :, None, :] + sa[..., None] * (kk_t * a_t)[:, :, None, :]
             + v_t[..., None] * k_t[:, :, None, :])
        return S, jnp.einsum('bhij,bhj->bhi', S, r_t)

    S_fin, y = lax.scan(step, wkv_state.astype(jnp.float32), seqs)
    y = y.transpose(1, 0, 2, 3)
    y = layernorm(y, p['rw_ln_g'].reshape(RW_HEADS, RW_N), p['rw_ln_b'].reshape(RW_HEADS, RW_N),
                  eps=RW_LN_EPS)
    bonus = jnp.sum((rh * kh * p['rw_rk']).astype(jnp.float32), axis=-1, keepdims=True)
    y = y + bonus * vh.astype(jnp.float32)
    y = (y.reshape(B, T, RW_DIM) * g).astype(pd.dtype)
    return y, pd[:, -1], S_fin


def memory_kv(mem, p):
    B, M, _ = mem.shape
    mn = rmsnorm(mem, p['mem_norm'])
    k = rmsnorm((mn @ p['wk_x']).reshape(B, M, X_HEADS, X_HEAD_DIM), p['xk_norm'])
    v = (mn @ p['wv_x']).reshape(B, M, X_HEADS, X_HEAD_DIM)
    return k, v


def trunk_layer(x, pos, attend, mem_k, mem_v, conv_state, shift_state, wkv_state, p):
    B, T, _ = x.shape
    proj = rmsnorm(x, p['norm_mix']) @ p['w_in']
    c1 = MLA_COLS
    c2 = c1 + CONV_COLS
    c3 = c2 + SGU_COLS
    pa, pb, pc, pd = proj[..., :c1], proj[..., c1:c2], proj[..., c2:c3], proj[..., c3:]
    q_nope, q_pe, ckv, kpe = mla_queries_latent(pa, pos, p)
    k_nope, v = mla_expand(ckv, p)
    oa = attend(q_nope, q_pe, k_nope, v, kpe)
    ob, conv_new = conv_module(pb, conv_state, p)
    oc, v_sgu = spatial_gating(pc, p)
    od, shift_new, wkv_new = rwkv_mix(pd, shift_state, wkv_state, p)
    o = jnp.concatenate([oa, ob, oc, od], axis=-1).reshape(B, T, N_GROUPS, GROUP_W)
    o = rmsnorm(o, p['out_norm'].reshape(N_GROUPS, GROUP_W)).reshape(B, T, D_MODEL)
    x = x + o @ p['w_out']
    q = rmsnorm((rmsnorm(x, p['norm_x']) @ p['wq_x']).reshape(B, T, X_HEADS, X_HEAD_DIM), p['xq_norm'])
    s = jnp.einsum('bthd,bmhd->bhtm', q, mem_k).astype(jnp.float32) * (X_HEAD_DIM ** -0.5)
    pr = jax.nn.softmax(s, axis=-1)
    xo = jnp.einsum('bhtm,bmhd->bthd', pr.astype(mem_v.dtype), mem_v).reshape(B, T, X_HEADS * X_HEAD_DIM)
    x = x + xo @ p['wo_x']
    hf = rmsnorm(x, p['norm_ffn']) @ p['w_ffn_in']
    x = x + (jax.nn.silu(hf[..., :D_FF]) * hf[..., D_FF:]) @ p['w_ffn_out']
    return x, ckv, kpe, conv_new, shift_new, wkv_new, v_sgu


def _normal(k, shape, scale):
    return jax.random.normal(k, shape, jnp.float32) * scale


def setup_inputs(seed: int = 0) -> dict:
    key = jax.random.key(seed)
    ks = iter(jax.random.split(key, 80))
    n_pages = PAST_LEN // PAGE_SIZE
    n_phys = (DEC_BATCH * n_pages * 5) // 4
    L = DEPTH

    def gain(shape):
        return 1.0 + _normal(next(ks), shape, 0.1)

    d = {}
    d['x_prompt'] = _normal(next(ks), (BATCH, SEQ, D_MODEL), 1.0)
    d['x_sample'] = _normal(next(ks), (DEC_BATCH, DEC_SEQ, D_MODEL), 1.0)
    d['mem_prompt'] = _normal(next(ks), (BATCH, N_MEM, D_MODEL), 1.0)
    d['cache_ckv'] = _normal(next(ks), (n_phys, DEPTH, PAGE_SIZE, KV_LORA), 1.0)
    d['cache_kpe'] = _normal(next(ks), (n_phys, DEPTH, PAGE_SIZE, QK_ROPE), 1.0)
    d['cache_mem_k'] = _normal(next(ks), (DEPTH, DEC_BATCH, N_MEM, X_HEADS, X_HEAD_DIM), 1.0)
    d['cache_mem_v'] = _normal(next(ks), (DEPTH, DEC_BATCH, N_MEM, X_HEADS, X_HEAD_DIM), 1.0)
    d['state_conv'] = _normal(next(ks), (DEPTH, DEC_BATCH, CONV_W - 1, CONV_DIM), 0.5)
    d['state_shift'] = _normal(next(ks), (DEPTH, DEC_BATCH, RW_COLS), 1.0)
    d['state_wkv'] = _normal(next(ks), (DEPTH, DEC_BATCH, RW_HEADS, RW_N, RW_N), 0.3)
    perm = jax.random.permutation(next(ks), n_phys)[:DEC_BATCH * n_pages]
    d['page_table'] = perm.reshape(DEC_BATCH, n_pages).astype(jnp.int32)
    d['norm_mix'] = gain((L, D_MODEL))
    d['w_in'] = _normal(next(ks), (L, D_MODEL, IN_COLS), D_MODEL ** -0.5)
    d['mla_q_norm'] = gain((L, Q_LORA))
    d['mla_kv_norm'] = gain((L, KV_LORA))
    d['mla_w_uq'] = _normal(next(ks), (L, Q_LORA, MLA_HEADS * (QK_NOPE + QK_ROPE)), Q_LORA ** -0.5)
    d['mla_w_uk'] = _normal(next(ks), (L, KV_LORA, MLA_HEADS * QK_NOPE), KV_LORA ** -0.5)
    d['mla_w_uv'] = _normal(next(ks), (L, KV_LORA, MLA_HEADS * V_DIM), KV_LORA ** -0.5)
    d['mla_gq_nope'] = gain((L, QK_NOPE))
    d['mla_gq_rope'] = gain((L, QK_ROPE))
    d['mla_gk_nope'] = gain((L, QK_NOPE))
    d['mla_gk_rope'] = gain((L, QK_ROPE))
    d['conv_w'] = _normal(next(ks), (L, CONV_W, CONV_DIM), CONV_W ** -0.5)
    d['conv_b'] = _normal(next(ks), (L, CONV_DIM), 0.02)
    d['conv_norm_g'] = gain((L, CONV_DIM))
    d['conv_norm_b'] = _normal(next(ks), (L, CONV_DIM), 0.02)
    d['conv_pw'] = _normal(next(ks), (L, CONV_DIM, CONV_DIM), CONV_DIM ** -0.5)
    d['sgu_norm_g'] = gain((L, SGU_DIM))
    d['sgu_norm_b'] = _normal(next(ks), (L, SGU_DIM), 0.02)
    d['sgu_w'] = _normal(next(ks), (L, SGU_HEADS, CHUNK, CHUNK), CHUNK ** -0.5)
    d['sgu_b'] = gain((L, SGU_HEADS, CHUNK))
    d['rw_mu'] = jax.random.uniform(next(ks), (L, RW_COLS), jnp.float32)
    d['rw_w0'] = _normal(next(ks), (L, RW_DIM), 0.5)
    d['rw_w2'] = _normal(next(ks), (L, DECAY_LORA, RW_DIM), 0.5 * DECAY_LORA ** -0.5)
    d['rw_a0'] = _normal(next(ks), (L, RW_DIM), 0.5)
    d['rw_a2'] = _normal(next(ks), (L, AAA_LORA, RW_DIM), 0.5 * AAA_LORA ** -0.5)
    d['rw_g2'] = _normal(next(ks), (L, GATE_LORA, RW_DIM), GATE_LORA ** -0.5)
    d['rw_kk'] = 0.85 + _normal(next(ks), (L, RW_DIM), 0.1)
    d['rw_ka'] = gain((L, RW_DIM))
    d['rw_rk'] = _normal(next(ks), (L, RW_HEADS, RW_N), 0.1)
    d['rw_ln_g'] = gain((L, RW_DIM))
    d['rw_ln_b'] = _normal(next(ks), (L, RW_DIM), 0.02)
    d['out_norm'] = gain((L, D_MODEL))
    d['w_out'] = _normal(next(ks), (L, D_MODEL, D_MODEL), D_MODEL ** -0.5)
    d['norm_x'] = gain((L, D_MODEL))
    d['mem_norm'] = gain((L, D_MODEL))
    d['wq_x'] = _normal(next(ks), (L, D_MODEL, X_HEADS * X_HEAD_DIM), D_MODEL ** -0.5)
    d['wk_x'] = _normal(next(ks), (L, D_MODEL, X_HEADS * X_HEAD_DIM), D_MODEL ** -0.5)
    d['wv_x'] = _normal(next(ks), (L, D_MODEL, X_HEADS * X_HEAD_DIM), D_MODEL ** -0.5)
    d['xq_norm'] = gain((L, X_HEAD_DIM))
    d['xk_norm'] = gain((L, X_HEAD_DIM))
    d['wo_x'] = _normal(next(ks), (L, X_HEADS * X_HEAD_DIM, D_MODEL), (X_HEADS * X_HEAD_DIM) ** -0.5)
    d['norm_ffn'] = gain((L, D_MODEL))
    d['w_ffn_in'] = _normal(next(ks), (L, D_MODEL, 2 * D_FF), D_MODEL ** -0.5)
    d['w_ffn_out'] = _normal(next(ks), (L, D_FF, D_MODEL), D_FF ** -0.5)
    return d


def reference(x_prompt, x_sample, mem_prompt, cache_ckv, cache_kpe, cache_mem_k, cache_mem_v,
              state_conv, state_shift, state_wkv, page_table,
              norm_mix, w_in, mla_q_norm, mla_kv_norm, mla_w_uq, mla_w_uk, mla_w_uv,
              mla_gq_nope, mla_gq_rope, mla_gk_nope, mla_gk_rope,
              conv_w, conv_b, conv_norm_g, conv_norm_b, conv_pw,
              sgu_norm_g, sgu_norm_b, sgu_w, sgu_b,
              rw_mu, rw_w0, rw_w2, rw_a0, rw_a2, rw_g2, rw_kk, rw_ka, rw_rk, rw_ln_g, rw_ln_b,
              out_norm, w_out,
              norm_x, mem_norm, wq_x, wk_x, wv_x, xq_norm, xk_norm, wo_x,
              norm_ffn, w_ffn_in, w_ffn_out):
    Bp, Tp, _ = x_prompt.shape
    Bs, Ts, _ = x_sample.shape
    pos_p = jnp.arange(Tp, dtype=jnp.int32)
    pos_s = PAST_LEN + jnp.arange(Ts, dtype=jnp.int32)
    y_p, y_s = x_prompt, x_sample
    ckv_p_l, kpe_p_l, memk_l, memv_l, conv_p_l, shift_p_l, wkv_p_l = [], [], [], [], [], [], []
    ckv_s_l, kpe_s_l, conv_s_l, shift_s_l, wkv_s_l, sgu_s_l = [], [], [], [], [], []
    for l in range(DEPTH):
        p = {
            'norm_mix': norm_mix[l], 'w_in': w_in[l],
            'mla_q_norm': mla_q_norm[l], 'mla_kv_norm': mla_kv_norm[l],
            'mla_w_uq': mla_w_uq[l], 'mla_w_uk': mla_w_uk[l], 'mla_w_uv': mla_w_uv[l],
            'mla_gq_nope': mla_gq_nope[l], 'mla_gq_rope': mla_gq_rope[l],
            'mla_gk_nope': mla_gk_nope[l], 'mla_gk_rope': mla_gk_rope[l],
            'conv_w': conv_w[l], 'conv_b': conv_b[l], 'conv_norm_g': conv_norm_g[l],
            'conv_norm_b': conv_norm_b[l], 'conv_pw': conv_pw[l],
            'sgu_norm_g': sgu_norm_g[l], 'sgu_norm_b': sgu_norm_b[l], 'sgu_w': sgu_w[l], 'sgu_b': sgu_b[l],
            'rw_mu': rw_mu[l], 'rw_w0': rw_w0[l], 'rw_w2': rw_w2[l], 'rw_a0': rw_a0[l], 'rw_a2': rw_a2[l],
            'rw_g2': rw_g2[l], 'rw_kk': rw_kk[l], 'rw_ka': rw_ka[l], 'rw_rk': rw_rk[l],
            'rw_ln_g': rw_ln_g[l], 'rw_ln_b': rw_ln_b[l],
            'out_norm': out_norm[l], 'w_out': w_out[l],
            'norm_x': norm_x[l], 'mem_norm': mem_norm[l], 'wq_x': wq_x[l], 'wk_x': wk_x[l],
            'wv_x': wv_x[l], 'xq_norm': xq_norm[l], 'xk_norm': xk_norm[l], 'wo_x': wo_x[l],
            'norm_ffn': norm_ffn[l], 'w_ffn_in': w_ffn_in[l], 'w_ffn_out': w_ffn_out[l],
        }
        mk, mv = memory_kv(mem_prompt, p)
        conv0 = jnp.zeros((Bp, CONV_W - 1, CONV_DIM), x_prompt.dtype)
        shift0 = jnp.zeros((Bp, RW_COLS), x_prompt.dtype)
        wkv0 = jnp.zeros((Bp, RW_HEADS, RW_N, RW_N), jnp.float32)
        y_p, ckv_p, kpe_p, conv_p, shift_p, wkv_p, _ = trunk_layer(
            y_p, pos_p, mla_attend_prompt, mk, mv, conv0, shift0, wkv0, p)
        ckv_p_l.append(ckv_p)
        kpe_p_l.append(kpe_p)
        memk_l.append(mk)
        memv_l.append(mv)
        conv_p_l.append(conv_p)
        shift_p_l.append(shift_p)
        wkv_p_l.append(wkv_p)

        def attend_s(qn, qr, kn, vv, kp, layer=l, prm=p):
            return mla_attend_sample(qn, qr, kn, vv, kp, cache_ckv, cache_kpe, page_table, layer, prm)

        y_s, ckv_s, kpe_s, conv_s, shift_s, wkv_s, sgu_s = trunk_layer(
            y_s, pos_s, attend_s, cache_mem_k[l], cache_mem_v[l],
            state_conv[l], state_shift[l], state_wkv[l], p)
        ckv_s_l.append(ckv_s)
        kpe_s_l.append(kpe_s)
        conv_s_l.append(conv_s)
        shift_s_l.append(shift_s)
        wkv_s_l.append(wkv_s)
        sgu_s_l.append(sgu_s)

    n_pp = Tp // PAGE_SIZE
    ckv_prompt = jnp.stack(ckv_p_l, axis=1).reshape(Bp, DEPTH, n_pp, PAGE_SIZE, KV_LORA).transpose(0, 2, 1, 3, 4)
    kpe_prompt = jnp.stack(kpe_p_l, axis=1).reshape(Bp, DEPTH, n_pp, PAGE_SIZE, QK_ROPE).transpose(0, 2, 1, 3, 4)
    ckv_sample = jnp.stack(ckv_s_l, axis=1)
    kpe_sample = jnp.stack(kpe_s_l, axis=1)
    mem_k_prompt = jnp.stack(memk_l, axis=0)
    mem_v_prompt = jnp.stack(memv_l, axis=0)
    conv_prompt = jnp.stack(conv_p_l, axis=0)
    conv_sample = jnp.stack(conv_s_l, axis=0)
    shift_prompt = jnp.stack(shift_p_l, axis=0)
    shift_sample = jnp.stack(shift_s_l, axis=0)
    wkv_prompt = jnp.stack(wkv_p_l, axis=0)
    wkv_sample = jnp.stack(wkv_s_l, axis=0)
    sgu_v_sample = jnp.stack(sgu_s_l, axis=0)
    return (y_p, y_s, ckv_prompt, kpe_prompt, ckv_sample, kpe_sample, mem_k_prompt, mem_v_prompt,
            conv_prompt, conv_sample, shift_prompt, shift_sample, wkv_prompt, wkv_sample, sgu_v_sample)
```

```python
import functools

import numpy as np
import jax
import jax.numpy as jnp
from jax import lax
from jax.experimental import pallas as pl
from jax.experimental.pallas import tpu as pltpu

F32 = jnp.float32
BF16 = jnp.bfloat16

EPS = 1e-6
LN_EPS = 1e-5
RW_LN_EPS = 64e-5
NEG = -1e30
ROPE_BASE = 10000.0

LANES = 128
SUBLANES = 8
VMEM_LIMIT_BYTES = 56 * 1024 * 1024

GROUP_W = 256
HEAD_DIM = 64
N_HEADS = 4
Q_LORA = 192
KV_LORA = 128
QK_ROPE = 32
QK_NOPE = 64
CONV_W = 31
CHUNK = 128
PAGE = 128
X_HEADS = 4
X_HEAD_DIM = 128
MLA_SCALE = (QK_NOPE + QK_ROPE) ** -0.5
PA_W = 512
PD_W = 1024
PAGES_PER_STEP = 16


def _cparams(*sem):
    return pltpu.CompilerParams(dimension_semantics=sem, vmem_limit_bytes=VMEM_LIMIT_BYTES)


def _dot(a, b):
    return jnp.dot(a, b, preferred_element_type=F32)


def _dot_nt(a, b):
    return lax.dot_general(a, b, (((1,), (1,)), ((), ())), preferred_element_type=F32)


def _seg_sum(x, e):
    hi = x.astype(BF16)
    lo = (x - hi.astype(F32)).astype(BF16)
    return _dot(hi, e) + _dot(lo, e)


def _rms(x, width=None):
    w = x.shape[-1] if width is None else width
    return x * lax.rsqrt(jnp.sum(x * x, axis=-1, keepdims=True) * (1.0 / w) + EPS)


def _sigmoid(x):
    return 1.0 / (1.0 + jnp.exp(-x))


def _tile(n, pref):
    t = min(n, pref)
    while n % t:
        t //= 2
    return t


def _full(shape):
    nd = len(shape)
    return pl.BlockSpec(shape, lambda *a: (0,) * nd)


def _proj_in_kernel(x_ref, g_ref, w_ref, pa_ref, pb_ref, pc_ref, pd_ref):
    hb = (_rms(x_ref[...]) * g_ref[...]).astype(BF16)
    pa_ref[...] = _dot(hb, w_ref[:, 0:512])
    pb_ref[...] = _dot(hb, w_ref[:, 512:1024])
    pc_ref[...] = _dot(hb, w_ref[:, 1024:1536])
    pd_ref[...] = _dot(hb, w_ref[:, 1536:2560])


def proj_in(x, g, w):
    n, d = x.shape
    tm = _tile(n, 512)
    row = lambda wd: pl.BlockSpec((tm, wd), lambda i: (i, 0))
    return pl.pallas_call(
        _proj_in_kernel,
        grid=(n // tm,),
        in_specs=[row(d), _full(g.shape), _full(w.shape)],
        out_specs=[row(512), row(512), row(512), row(1024)],
        out_shape=[jax.ShapeDtypeStruct((n, wd), F32) for wd in (512, 512, 512, 1024)],
        compiler_params=_cparams("parallel"),
        name="proj_in",
    )(x, g, w)


def _rope128(x, c, sa, sb):
    w = x.shape[-1]
    return x * c + pltpu.roll(x, w - 16, 1) * sa + pltpu.roll(x, 16, 1) * sb


def _mla_prep_kernel(pa_ref, c_ref, sa_ref, sb_ref, gqn_ref, gkv_ref, wuq_ref, eq_ref, gq_ref,
                     gkr_ref, wuk_ref, ek_ref, gk_ref, wuv_ref,
                     q_ref, k_ref, v_ref, ckv_ref, kpe_ref):
    pa = pa_ref[...]
    c, sa, sb = c_ref[...], sa_ref[...], sb_ref[...]
    c4 = jnp.concatenate([c] * 4, axis=1)
    sa4 = jnp.concatenate([sa] * 4, axis=1)
    sb4 = jnp.concatenate([sb] * 4, axis=1)
    cq = _rms(pa[:, 0:256], Q_LORA) * gqn_ref[...]
    q = _dot(cq.astype(BF16), wuq_ref[...])
    qn = q * lax.rsqrt(_seg_sum(q * q, eq_ref[...]) + EPS) * gq_ref[...]
    q_ref[...] = (_rope128(qn, c4, sa4, sb4) * MLA_SCALE).astype(BF16)
    ckv = _rms(pa[:, 256:384]) * gkv_ref[...]
    ckv_ref[...] = ckv
    kp = _rms(pa[:, 384:512], QK_ROPE) * gkr_ref[...]
    kr = _rope128(kp, c, sa, sb)
    kpe_ref[...] = kr[:, 0:QK_ROPE]
    ckv_b = ckv.astype(BF16)
    kn = _dot(ckv_b, wuk_ref[...])
    kn = kn * lax.rsqrt(_seg_sum(kn * kn, ek_ref[...]) + EPS) * gk_ref[...]
    k_ref[...] = (kn + jnp.concatenate([kr] * 4, axis=1)).astype(BF16)
    v_ref[...] = _dot(ckv_b, wuv_ref[...]).astype(BF16)


def mla_prep(pa, tabs, wts, n_tab_blocks):
    n = pa.shape[0]
    c, sa, sb = tabs
    tm = c.shape[0] // n_tab_blocks
    assert n % tm == 0
    row = lambda wd: pl.BlockSpec((tm, wd), lambda i: (i, 0))
    tab = pl.BlockSpec((tm, LANES), lambda i: (i % n_tab_blocks, 0))
    return pl.pallas_call(
        _mla_prep_kernel,
        grid=(n // tm,),
        in_specs=[row(PA_W), tab, tab, tab] + [_full(w.shape) for w in wts],
        out_specs=[row(512), row(512), row(256), row(KV_LORA), row(QK_ROPE)],
        out_shape=[jax.ShapeDtypeStruct((n, 512), BF16), jax.ShapeDtypeStruct((n, 512), BF16),
                   jax.ShapeDtypeStruct((n, 256), BF16), jax.ShapeDtypeStruct((n, KV_LORA), F32),
                   jax.ShapeDtypeStruct((n, QK_ROPE), F32)],
        compiler_params=_cparams("parallel"),
        name="mla_prep",
    )(pa, c, sa, sb, *wts)


def _mla_attn_prompt_kernel(q_ref, k_ref, v_ref, o_ref, *, tq):
    i = pl.program_id(1)
    outs = []
    for h in range(N_HEADS):
        q = q_ref[:, h * LANES:(h + 1) * LANES]

        def blk(j, carry, masked, q=q, h=h):
            m, l, acc = carry
            off = pl.multiple_of(j * tq, tq)
            kb = k_ref[pl.ds(off, tq), h * LANES:(h + 1) * LANES]
            vb = v_ref[pl.ds(off, tq), h * HEAD_DIM:(h + 1) * HEAD_DIM]
            s = _dot_nt(q, kb)
            if masked:
                r = lax.broadcasted_iota(jnp.int32, (tq, tq), 0)
                cc = lax.broadcasted_iota(jnp.int32, (tq, tq), 1)
                s = jnp.where(cc <= r, s, NEG)
            m_new = jnp.maximum(m, jnp.max(s, axis=-1, keepdims=True))
            corr = jnp.exp(m - m_new)
            p = jnp.exp(s - m_new)
            l = l * corr + jnp.sum(p, axis=-1, keepdims=True)
            acc = acc * corr + _dot(p.astype(BF16), vb)
            return m_new, l, acc

        init = (jnp.full((tq, 1), NEG, F32), jnp.zeros((tq, 1), F32), jnp.zeros((tq, HEAD_DIM), F32))
        carry = lax.fori_loop(0, i, functools.partial(blk, masked=False), init)
        m, l, acc = blk(i, carry, True)
        outs.append(acc / l)
    o_ref[...] = jnp.concatenate(outs, axis=1)


def mla_attn_prompt(q, k, v, b, t):
    tq = _tile(t, 256)
    nq = t // tq
    return pl.pallas_call(
        functools.partial(_mla_attn_prompt_kernel, tq=tq),
        grid=(b, nq),
        in_specs=[pl.BlockSpec((tq, 512), lambda bi, i: (bi * nq + i, 0)),
                  pl.BlockSpec((t, 512), lambda bi, i: (bi, 0)),
                  pl.BlockSpec((t, 256), lambda bi, i: (bi, 0))],
        out_specs=pl.BlockSpec((tq, 256), lambda bi, i: (bi * nq + i, 0)),
        out_shape=jax.ShapeDtypeStruct((b * t, 256), F32),
        compiler_params=_cparams("parallel", "arbitrary"),
        name="mla_attn_prompt",
    )(q, k, v)


def _mla_attn_sample_kernel(pt_ref, qbd_ref, qpe_ref, ckvn_ref, kpen_ref, wuk_ref, eh_ref, wuv_ref, *rest,
                            n_pg, tq):
    ckv_refs = rest[:n_pg]
    kpe_refs = rest[n_pg:2 * n_pg]
    o_ref = rest[2 * n_pg]
    m_scr, l_scr, acc_scr = rest[2 * n_pg + 1:]
    p_id = pl.program_id(1)
    qbd = qbd_ref[0]
    qpe = qpe_ref[0]

    def scores(ckv, kpe):
        ckv_b = ckv.astype(BF16)
        kn = _dot(ckv_b, wuk_ref[...])
        ss = _dot_nt(eh_ref[...], (kn * kn).astype(BF16))
        s = _dot_nt(qbd, kn.astype(BF16)) * lax.rsqrt(ss + EPS) + _dot_nt(qpe, kpe.astype(BF16))
        return s, ckv_b

    @pl.when(p_id == 0)
    def _():
        s, ckv_b = scores(ckvn_ref[0], kpen_ref[0])
        r = lax.broadcasted_iota(jnp.int32, s.shape, 0) % tq
        cc = lax.broadcasted_iota(jnp.int32, s.shape, 1)
        s = jnp.where(cc <= r, s, NEG)
        m = jnp.max(s, axis=-1, keepdims=True)
        p = jnp.exp(s - m)
        m_scr[...] = m
        l_scr[...] = jnp.sum(p, axis=-1, keepdims=True)
        acc_scr[...] = _dot(p.astype(BF16), ckv_b)

    ckv = jnp.concatenate([r[0, 0] for r in ckv_refs], axis=0)
    kpe = jnp.concatenate([r[0, 0] for r in kpe_refs], axis=0)
    s, ckv_b = scores(ckv, kpe)
    m = m_scr[...]
    m_new = jnp.maximum(m, jnp.max(s, axis=-1, keepdims=True))
    corr = jnp.exp(m - m_new)
    p = jnp.exp(s - m_new)
    m_scr[...] = m_new
    l_scr[...] = l_scr[...] * corr + jnp.sum(p, axis=-1, keepdims=True)
    acc_scr[...] = acc_scr[...] * corr + _dot(p.astype(BF16), ckv_b)

    @pl.when(p_id == pl.num_programs(1) - 1)
    def _():
        lat = (acc_scr[...] / l_scr[...]).astype(BF16)
        full = _dot(lat, wuv_ref[...])
        lane_head = lax.broadcasted_iota(jnp.int32, (tq, 256), 1) // HEAD_DIM
        out = jnp.zeros((tq, 256), F32)
        for h in range(N_HEADS):
            out = jnp.where(lane_head == h, full[h * tq:(h + 1) * tq, :], out)
        o_ref[0] = out


def mla_attn_sample(page_table, qbd, qpe, ckv_new, kpe_new, wuk, eh, wuv, cache_ckv, cache_kpe, layer):
    bs, n_pages = page_table.shape
    tq = qbd.shape[1] // N_HEADS
    n_pg = min(PAGES_PER_STEP, n_pages)
    assert n_pages % n_pg == 0
    nr = N_HEADS * tq

    def pg_spec(k, width):
        return pl.BlockSpec((1, 1, PAGE, width), lambda b, p, pt: (pt[b, p * n_pg + k], layer, 0, 0))

    per_b = lambda shp: pl.BlockSpec((1,) + shp, lambda b, p, pt: (b, 0, 0))
    cst = lambda shp: pl.BlockSpec(shp, lambda b, p, pt: (0,) * len(shp))
    grid_spec = pltpu.PrefetchScalarGridSpec(
        num_scalar_prefetch=1,
        grid=(bs, n_pages // n_pg),
        in_specs=[per_b((nr, 256)), per_b((nr, QK_ROPE)), per_b((PAGE, KV_LORA)), per_b((PAGE, QK_ROPE)),
                  cst(wuk.shape), cst(eh.shape), cst(wuv.shape)]
                 + [pg_spec(k, KV_LORA) for k in range(n_pg)]
                 + [pg_spec(k, QK_ROPE) for k in range(n_pg)],
        out_specs=pl.BlockSpec((1, tq, 256), lambda b, p, pt: (b, 0, 0)),
        scratch_shapes=[pltpu.VMEM((nr, 1), F32), pltpu.VMEM((nr, 1), F32), pltpu.VMEM((nr, KV_LORA), F32)],
    )
    return pl.pallas_call(
        functools.partial(_mla_attn_sample_kernel, n_pg=n_pg, tq=tq),
        grid_spec=grid_spec,
        out_shape=jax.ShapeDtypeStruct((bs, tq, 256), F32),
        compiler_params=_cparams("parallel", "arbitrary"),
        name="mla_attn_sample",
    )(page_table, qbd, qpe, ckv_new, kpe_new, wuk, eh, wuv, *([cache_ckv] * n_pg), *([cache_kpe] * n_pg))


CONV_HALO = 32


def _conv_kernel(pb_ref, st_ref, cw_ref, cb_ref, e_ref, g_ref, b_ref, pw_ref, o_ref, st_out_ref, xbuf, *, tt):
    j = pl.program_id(1)
    lo = CONV_HALO - (CONV_W - 1)

    @pl.when(j == 0)
    def _():
        xbuf[pl.ds(lo, CONV_W - 1), :] = st_ref[0]

    pb = pb_ref[...]
    xbuf[pl.ds(CONV_HALO, tt), :] = pb[:, 0:256] * _sigmoid(pb[:, 256:512])
    y = jnp.zeros((tt, 256), F32) + cb_ref[...]
    for k in range(CONV_W):
        y = y + xbuf[pl.ds(lo + k, tt), :] * cw_ref[pl.ds(k, 1), :]
    new_state = xbuf[pl.ds(lo + tt, CONV_W - 1), :]
    xbuf[pl.ds(lo, CONV_W - 1), :] = new_state
    st_out_ref[0] = new_state
    e = e_ref[...]
    yc = y - _seg_sum(y, e)
    yn = yc * lax.rsqrt(_seg_sum(yc * yc, e) + LN_EPS) * g_ref[...] + b_ref[...]
    act = yn * _sigmoid(yn)
    o_ref[...] = _dot(act.astype(BF16), pw_ref[...])


def conv_module(pb, state, wts, b, t):
    tt = _tile(t, 256)
    nt = t // tt
    return pl.pallas_call(
        functools.partial(_conv_kernel, tt=tt),
        grid=(b, nt),
        in_specs=[pl.BlockSpec((tt, 512), lambda bi, j: (bi * nt + j, 0)),
                  pl.BlockSpec((1, CONV_W - 1, 256), lambda bi, j: (bi, 0, 0))]
                 + [_full(w.shape) for w in wts],
        out_specs=[pl.BlockSpec((tt, 256), lambda bi, j: (bi * nt + j, 0)),
                   pl.BlockSpec((1, CONV_W - 1, 256), lambda bi, j: (bi, 0, 0))],
        out_shape=[jax.ShapeDtypeStruct((b * t, 256), F32), jax.ShapeDtypeStruct((b, CONV_W - 1, 256), F32)],
        scratch_shapes=[pltpu.VMEM((CONV_HALO + tt, 256), F32)],
        compiler_params=_cparams("parallel", "arbitrary"),
        name="conv_module",
    )(pb, state, *wts)


def _sgu_kernel(pc_ref, g_ref, b_ref, w_ref, bias_ref, o_ref, v_ref, *, n_chunks):
    x = pc_ref[...]
    z = 0.5 * x * (1.0 + jnp.tanh(0.7978845608028654 * (x + 0.044715 * (x * x * x))))
    u = z[:, 0:256]
    v = z[:, 256:512]
    vc = v - jnp.mean(v, axis=-1, keepdims=True)
    v = vc * lax.rsqrt(jnp.mean(vc * vc, axis=-1, keepdims=True) + LN_EPS) * g_ref[...] + b_ref[...]
    v_ref[...] = v
    lane_head = lax.broadcasted_iota(jnp.int32, (CHUNK, 256), 1) // HEAD_DIM
    for c in range(n_chunks):
        vcb = v[c * CHUNK:(c + 1) * CHUNK, :]
        sv = bias_ref[...]
        for h in range(N_HEADS):
            sv = sv + _dot(w_ref[h], jnp.where(lane_head == h, vcb, 0.0).astype(BF16))
        o_ref[pl.ds(c * CHUNK, CHUNK), :] = u[c * CHUNK:(c + 1) * CHUNK, :] * sv


def sgu(pc, wts):
    n = pc.shape[0]
    tm = _tile(n, 512)
    row = lambda wd: pl.BlockSpec((tm, wd), lambda i: (i, 0))
    return pl.pallas_call(
        functools.partial(_sgu_kernel, n_chunks=tm // CHUNK),
        grid=(n // tm,),
        in_specs=[row(512)] + [_full(w.shape) for w in wts],
        out_specs=[row(256), row(256)],
        out_shape=[jax.ShapeDtypeStruct((n, 256), F32), jax.ShapeDtypeStruct((n, 256), F32)],
        compiler_params=_cparams("parallel"),
        name="sgu",
    )(pc, *wts)


RW_HALO = 8


def _rwkv_prep_kernel(pd_ref, sh_ref, mu_ref, w0_ref, w2_ref, a0_ref, a2_ref, g2_ref, kkp_ref, ka_ref, rk_ref,
                      e_ref, r_ref, w_ref, k_ref, v_ref, kk_ref, kka_ref, g_ref, bv_ref, xbuf, *, tt):
    j = pl.program_id(1)

    @pl.when(j == 0)
    def _():
        xbuf[pl.ds(RW_HALO - 1, 1), :] = sh_ref[0]

    pd = pd_ref[...]
    xbuf[pl.ds(RW_HALO, tt), :] = pd
    prev = xbuf[pl.ds(RW_HALO - 1, tt), :]
    xbuf[pl.ds(RW_HALO - 1, 1), :] = pd[tt - 1:tt, :]
    xs = pd + (prev - pd) * mu_ref[...]
    r = xs[:, 0:256]
    k = xs[:, 256:512]
    v = xs[:, 512:768]
    xwa = xs[:, 768:896]
    xg = xs[:, 896:1024]
    z = -(w0_ref[...] + _dot(jnp.tanh(xwa).astype(BF16), w2_ref[...]))
    softplus = jnp.maximum(z, 0.0) + jnp.log(1.0 + jnp.exp(-jnp.abs(z)))
    w_ref[...] = jnp.exp(-jnp.exp(-softplus - 0.5))
    a = _sigmoid(a0_ref[...] + _dot(xwa.astype(BF16), a2_ref[...]))
    g_ref[...] = _dot(_sigmoid(xg).astype(BF16), g2_ref[...])
    e = e_ref[...]
    kk = k * kkp_ref[...]
    kk = kk * lax.rsqrt(_seg_sum(kk * kk, e) + 1e-12)
    k2 = k * (1.0 + (a - 1.0) * ka_ref[...])
    r_ref[...] = r
    k_ref[...] = k2
    v_ref[...] = v
    kk_ref[...] = kk
    kka_ref[...] = kk * a
    bv_ref[...] = _seg_sum(r * k2 * rk_ref[...], e) * v


def rwkv_prep(pd, shift, wts, b, t):
    tt = _tile(t, 256)
    nt = t // tt
    row = lambda wd: pl.BlockSpec((tt, wd), lambda bi, j: (bi * nt + j, 0))
    return pl.pallas_call(
        functools.partial(_rwkv_prep_kernel, tt=tt),
        grid=(b, nt),
        in_specs=[row(PD_W), pl.BlockSpec((1, 1, PD_W), lambda bi, j: (bi, 0, 0))]
                 + [_full(w.shape) for w in wts],
        out_specs=[row(256)] * 8,
        out_shape=[jax.ShapeDtypeStruct((b * t, 256), F32)] * 8,
        scratch_shapes=[pltpu.VMEM((RW_HALO + tt, PD_W), F32)],
        compiler_params=_cparams("parallel", "arbitrary"),
        name="rwkv_prep",
    )(pd, shift, *wts)


def _rwkv_scan_kernel(wt_ref, kkt_ref, kkat_ref, kt_ref, rt_ref, v_ref, s0_ref, y_ref, sf_ref, s_scr, *, sb):
    c = pl.program_id(1)

    @pl.when(c == 0)
    def _():
        s_scr[...] = s0_ref[0]

    lane = lax.broadcasted_iota(jnp.int32, (HEAD_DIM, LANES), 1)
    base = jnp.where(lane < HEAD_DIM, 0, HEAD_DIM)

    def group(gi, carry):
        row0 = pl.multiple_of(gi * SUBLANES, SUBLANES)
        v8 = v_ref[0, pl.ds(row0, SUBLANES), :]
        states = list(carry)
        ys = [[], []]
        for i in range(SUBLANES):
            idx = base + (row0 + i)
            for p in range(2):
                st = states[p]
                col = lambda ref: jnp.take_along_axis(ref[0, 0, p], idx, axis=1)
                vrow = v8[i:i + 1, p * LANES:(p + 1) * LANES]
                sa = -jnp.sum(st * col(kkt_ref), axis=0, keepdims=True)
                st = st * col(wt_ref) + col(kkat_ref) * sa + col(kt_ref) * vrow
                ys[p].append(jnp.sum(st * col(rt_ref), axis=0, keepdims=True))
                states[p] = st
        y8 = jnp.concatenate([jnp.concatenate(ys[0], axis=0), jnp.concatenate(ys[1], axis=0)], axis=1)
        y_ref[0, pl.ds(row0, SUBLANES), :] = y8
        return tuple(states)

    s0 = (s_scr[:, 0:LANES], s_scr[:, LANES:2 * LANES])
    sl, sr = lax.fori_loop(0, sb // SUBLANES, group, s0)
    s_scr[:, 0:LANES] = sl
    s_scr[:, LANES:2 * LANES] = sr

    @pl.when(c == pl.num_programs(1) - 1)
    def _():
        sf_ref[0, :, 0:LANES] = sl
        sf_ref[0, :, LANES:2 * LANES] = sr


def rwkv_scan(cols, v, s0, b, t):
    sb = min(t, HEAD_DIM)
    nblk = t // sb
    col_spec = pl.BlockSpec((1, 1, 2, HEAD_DIM, LANES), lambda bi, c: (bi, c, 0, 0, 0))
    st_spec = pl.BlockSpec((1, HEAD_DIM, 256), lambda bi, c: (bi, 0, 0))
    seq_spec = pl.BlockSpec((1, sb, 256), lambda bi, c: (bi, c, 0))
    return pl.pallas_call(
        functools.partial(_rwkv_scan_kernel, sb=sb),
        grid=(b, nblk),
        in_specs=[col_spec] * 5 + [seq_spec, st_spec],
        out_specs=[seq_spec, st_spec],
        out_shape=[jax.ShapeDtypeStruct((b, t, 256), F32), jax.ShapeDtypeStruct((b, HEAD_DIM, 256), F32)],
        scratch_shapes=[pltpu.VMEM((HEAD_DIM, 256), F32)],
        compiler_params=_cparams("parallel", "arbitrary"),
        name="rwkv_scan",
    )(*cols, v, s0)


def _mix_out_kernel(x_ref, oa_ref, ob_ref, oc_ref, y_ref, bv_ref, g_ref, e_ref, lg_ref, lb_ref, on_ref, w_ref,
                    o_ref):
    e = e_ref[...]
    y = y_ref[...]
    yc = y - _seg_sum(y, e)
    yn = yc * lax.rsqrt(_seg_sum(yc * yc, e) + RW_LN_EPS) * lg_ref[...] + lb_ref[...]
    od = (yn + bv_ref[...]) * g_ref[...]
    acc = x_ref[...]
    for gi, o in enumerate((oa_ref[...], ob_ref[...], oc_ref[...], od)):
        on = (_rms(o) * on_ref[:, gi * 256:(gi + 1) * 256]).astype(BF16)
        acc = acc + _dot(on, w_ref[pl.ds(gi * 256, 256), :])
    o_ref[...] = acc


def mix_out(x, oa, ob, oc, y, bv, g, wts):
    n, d = x.shape
    tm = _tile(n, 512)
    row = lambda wd: pl.BlockSpec((tm, wd), lambda i: (i, 0))
    return pl.pallas_call(
        _mix_out_kernel,
        grid=(n // tm,),
        in_specs=[row(d)] + [row(256)] * 6 + [_full(w.shape) for w in wts],
        out_specs=row(d),
        out_shape=jax.ShapeDtypeStruct((n, d), F32),
        compiler_params=_cparams("parallel"),
        name="mix_out",
    )(x, oa, ob, oc, y, bv, g, *wts)


def _norm_proj_kernel(x_ref, g_ref, w_ref, hg_ref, *o_refs, n_norm_heads, scale):
    hb = (_rms(x_ref[...]) * g_ref[...]).astype(BF16)
    y = _dot(hb, w_ref[...])
    for i, o_ref in enumerate(o_refs):
        parts = []
        for h in range(X_HEADS):
            blk = y[:, (i * X_HEADS + h) * X_HEAD_DIM:(i * X_HEADS + h + 1) * X_HEAD_DIM]
            if i * X_HEADS + h < n_norm_heads:
                blk = _rms(blk) * hg_ref[...] * scale
            parts.append(blk)
        o_ref[...] = jnp.concatenate(parts, axis=1).astype(o_ref.dtype)


def norm_proj(x, g, w, hg, n_norm_heads, scale, out_dtype):
    n, d = x.shape
    n_out = w.shape[1] // 512
    tm = _tile(n, 512)
    row = lambda wd: pl.BlockSpec((tm, wd), lambda i: (i, 0))
    return pl.pallas_call(
        functools.partial(_norm_proj_kernel, n_norm_heads=n_norm_heads, scale=scale),
        grid=(n // tm,),
        in_specs=[row(d), _full(g.shape), _full(w.shape), _full(hg.shape)],
        out_specs=[row(512)] * n_out,
        out_shape=[jax.ShapeDtypeStruct((n, 512), out_dtype)] * n_out,
        compiler_params=_cparams("parallel"),
        name="norm_proj",
    )(x, g, w, hg)


def _xattn_kernel(x_ref, q_ref, k_ref, v_ref, wo_ref, o_ref):
    outs = []
    for h in range(X_HEADS):
        sl = slice(h * X_HEAD_DIM, (h + 1) * X_HEAD_DIM)
        s = _dot_nt(q_ref[:, sl].astype(BF16), k_ref[0, :, sl].astype(BF16))
        p = jnp.exp(s - jnp.max(s, axis=-1, keepdims=True))
        p = p / jnp.sum(p, axis=-1, keepdims=True)
        outs.append(_dot(p.astype(BF16), v_ref[0, :, sl].astype(BF16)))
    xo = jnp.concatenate(outs, axis=1).astype(BF16)
    o_ref[...] = x_ref[...] + _dot(xo, wo_ref[...])


def xattn(x, q, mem_k, mem_v, wo, b, t):
    n, d = x.shape
    tq = _tile(t, 512)
    nq = t // tq
    n_mem = mem_k.shape[1]
    row = lambda wd: pl.BlockSpec((tq, wd), lambda bi, i: (bi * nq + i, 0))
    kv = pl.BlockSpec((1, n_mem, 512), lambda bi, i: (bi, 0, 0))
    return pl.pallas_call(
        _xattn_kernel,
        grid=(b, nq),
        in_specs=[row(d), row(512), kv, kv, _full(wo.shape)],
        out_specs=row(d),
        out_shape=jax.ShapeDtypeStruct((n, d), F32),
        compiler_params=_cparams("parallel", "arbitrary"),
        name="xattn",
    )(x, q, mem_k, mem_v, wo)


FFN_CHUNK = 256


def _ffn_kernel(x_ref, g_ref, wg_ref, wu_ref, wo_ref, o_ref, *, n_chunks):
    x = x_ref[...]
    hb = (_rms(x) * g_ref[...]).astype(BF16)
    acc = x
    for c in range(n_chunks):
        a = _dot(hb, wg_ref[c])
        u = _dot(hb, wu_ref[c])
        acc = acc + _dot((a * _sigmoid(a) * u).astype(BF16), wo_ref[c])
    o_ref[...] = acc


def ffn(x, g, wg, wu, wo):
    n, d = x.shape
    tm = _tile(n, 512)
    row = pl.BlockSpec((tm, d), lambda i: (i, 0))
    return pl.pallas_call(
        functools.partial(_ffn_kernel, n_chunks=wg.shape[0]),
        grid=(n // tm,),
        in_specs=[row, _full(g.shape), _full(wg.shape), _full(wu.shape), _full(wo.shape)],
        out_specs=row,
        out_shape=jax.ShapeDtypeStruct((n, d), F32),
        compiler_params=_cparams("parallel"),
        name="ffn",
    )(x, g, wg, wu, wo)


def _seg_matrix(seg_ids, seg_len):
    s = np.asarray(seg_ids)
    m = (s[:, None] == s[None, :]) & (s[:, None] >= 0)
    return jnp.asarray(m.astype(np.float32) / np.asarray(seg_len, np.float32)[None, :], BF16)


def _mla_segments():
    lane = np.arange(512)
    blk, off = lane // 128, lane % 128
    q_ids = np.where(off < 32, 2 * blk, np.where(off < 64, -1, 2 * blk + 1))
    q_len = np.where(off < 32, 32.0, 64.0)
    k_ids = np.where(off < 64, -1, blk)
    k_len = np.full(512, 64.0)
    return _seg_matrix(q_ids, q_len), _seg_matrix(k_ids, k_len)


def _head_matrix(scale_len):
    lane = np.arange(256)
    return _seg_matrix(lane // HEAD_DIM, np.full(256, scale_len))


def _rope_tables(pos):
    half = QK_ROPE // 2
    inv = jnp.power(ROPE_BASE, -jnp.arange(half, dtype=F32) / half)
    ang = pos.astype(F32)[:, None] * inv[None, :]
    cos, sin = jnp.cos(ang), jnp.sin(ang)
    n = pos.shape[0]
    z = lambda w: jnp.zeros((n, w), F32)
    c = jnp.concatenate([cos, cos, jnp.ones((n, LANES - QK_ROPE), F32)], axis=1)
    sa = jnp.concatenate([-sin, z(LANES - half)], axis=1)
    sb = jnp.concatenate([z(half), sin, z(LANES - QK_ROPE)], axis=1)
    return c, sa, sb


def _layer_weights(p):
    row = lambda v: v.reshape(1, -1).astype(F32)
    zc = lambda a, w: jnp.zeros((a.shape[0], w), a.dtype)
    w_in = p['w_in']
    c1, c2, c3 = 352, 352 + 512, 352 + 1024
    pa = w_in[:, :c1]
    w_in_p = jnp.concatenate(
        [pa[:, :Q_LORA], zc(pa, 64), pa[:, Q_LORA:Q_LORA + KV_LORA], pa[:, Q_LORA + KV_LORA:], zc(pa, 96),
         w_in[:, c1:c2], w_in[:, c2:c3], w_in[:, c3:]], axis=1).astype(BF16)

    wuq = p['mla_w_uq'].reshape(Q_LORA, N_HEADS, QK_NOPE + QK_ROPE)
    wuq = jnp.concatenate([wuq[:, :, QK_NOPE:], jnp.zeros((Q_LORA, N_HEADS, 32), F32), wuq[:, :, :QK_NOPE]], axis=2)
    wuq = jnp.concatenate([wuq.reshape(Q_LORA, 512), jnp.zeros((256 - Q_LORA, 512), F32)], axis=0).astype(BF16)
    wuk = p['mla_w_uk'].reshape(KV_LORA, N_HEADS, QK_NOPE)
    wuk_p = jnp.concatenate([jnp.zeros((KV_LORA, N_HEADS, 64), F32), wuk], axis=2).reshape(KV_LORA, 512).astype(BF16)
    blk = lambda a, b_, c_: jnp.tile(jnp.concatenate([a, b_, c_]), N_HEADS).reshape(1, 512)
    z32, z64 = jnp.zeros((32,), F32), jnp.zeros((64,), F32)
    gq = blk(p['mla_gq_rope'], z32, p['mla_gq_nope'])
    gk = blk(z32, z32, p['mla_gk_nope'])
    gqn = jnp.concatenate([p['mla_q_norm'], z64]).reshape(1, 256)
    gkr = jnp.concatenate([p['mla_gk_rope'], jnp.zeros((96,), F32)]).reshape(1, LANES)
    eq, ek = _mla_segments()
    mla = (gqn, row(p['mla_kv_norm']), wuq, eq, gq, gkr, wuk_p, ek, gk, p['mla_w_uv'].astype(BF16))

    e64 = _head_matrix(64.0)
    conv = (p['conv_w'], row(p['conv_b']), e64, row(p['conv_norm_g']), row(p['conv_norm_b']),
            p['conv_pw'].astype(BF16))

    z64r = jnp.zeros((64, GROUP_W), F32)
    rw = (row(p['rw_mu']),
          row(p['rw_w0']), jnp.concatenate([p['rw_w2'], z64r], axis=0).astype(BF16),
          row(p['rw_a0']), jnp.concatenate([z64r, p['rw_a2']], axis=0).astype(BF16),
          p['rw_g2'].astype(BF16), row(p['rw_kk']), row(p['rw_ka']), row(p['rw_rk']), _head_matrix(1.0))

    mix = (e64, row(p['rw_ln_g']), row(p['rw_ln_b']), row(p['out_norm']), p['w_out'].astype(BF16))

    d_ff = p['w_ffn_out'].shape[0]
    nck = d_ff // FFN_CHUNK
    d = w_in.shape[0]
    wg = p['w_ffn_in'][:, :d_ff].reshape(d, nck, FFN_CHUNK).transpose(1, 0, 2).astype(BF16)
    wu = p['w_ffn_in'][:, d_ff:].reshape(d, nck, FFN_CHUNK).transpose(1, 0, 2).astype(BF16)
    wo = p['w_ffn_out'].reshape(nck, FFN_CHUNK, d).astype(BF16)

    return dict(
        norm_mix=row(p['norm_mix']), w_in=w_in_p, mla=mla, conv=conv, rw=rw, mix=mix,
        wuk=p['mla_w_uk'].astype(BF16), wuv=p['mla_w_uv'].astype(BF16), gk_nope=p['mla_gk_nope'],
        sgu_ln=(row(p['sgu_norm_g']), row(p['sgu_norm_b'])), sgu_w=p['sgu_w'], sgu_b=p['sgu_b'],
        norm_x=row(p['norm_x']), wq=p['wq_x'].astype(BF16), xq_norm=row(p['xq_norm']),
        mem_norm=row(p['mem_norm']), wkv=jnp.concatenate([p['wk_x'], p['wv_x']], axis=1).astype(BF16),
        xk_norm=row(p['xk_norm']), wo_x=p['wo_x'].astype(BF16),
        norm_ffn=row(p['norm_ffn']), wg=wg, wu=wu, wo=wo)


def _sgu_weights(lw, t):
    l = min(t, CHUNK)
    w = lw['sgu_w'][:, :l, :l] * jnp.tril(jnp.ones((l, l), F32))
    reps = CHUNK // l
    if reps > 1:
        w = jnp.einsum('ab,hij->haibj', jnp.eye(reps, dtype=F32), w).reshape(N_HEADS, CHUNK, CHUNK)
    bias = jnp.tile(lw['sgu_b'][:, :l].T, (reps, 1))
    bias = jnp.repeat(bias, HEAD_DIM, axis=1)
    return lw['sgu_ln'] + (w.astype(BF16), bias)


def _to_cols(a, b, t):
    sb = min(t, HEAD_DIM)
    nblk = t // sb
    a = a.reshape(b, nblk, sb, 2, 2, HEAD_DIM).transpose(0, 1, 3, 5, 4, 2)
    if sb < HEAD_DIM:
        a = jnp.pad(a, ((0, 0),) * 5 + ((0, HEAD_DIM - sb),))
    return a.reshape(b, nblk, 2, HEAD_DIM, LANES)


def _trunk_layer(x, b, t, lw, tabs, n_tab_blocks, attend, mem_k, mem_v, conv_state, shift_state, wkv_state):
    pa, pb, pc, pd = proj_in(x, lw['norm_mix'], lw['w_in'])
    q, k, v, ckv, kpe = mla_prep(pa, tabs, lw['mla'], n_tab_blocks)
    oa = attend(q, k, v, ckv, kpe)
    ob, conv_new = conv_module(pb, conv_state, lw['conv'], b, t)
    oc, v_sgu = sgu(pc, _sgu_weights(lw, t))
    r, w, k2, vv, kk, kka, g, bv = rwkv_prep(pd, shift_state.reshape(b, 1, PD_W), lw['rw'], b, t)
    cols = [_to_cols(a, b, t) for a in (w, kk, kka, k2, r)]
    s0 = wkv_state.transpose(0, 3, 1, 2).reshape(b, HEAD_DIM, 256)
    y, s_fin = rwkv_scan(cols, vv.reshape(b, t, 256), s0, b, t)
    wkv_new = s_fin.reshape(b, HEAD_DIM, N_HEADS, HEAD_DIM).transpose(0, 2, 3, 1)
    shift_new = pd.reshape(b, t, PD_W)[:, -1]
    x = mix_out(x, oa, ob, oc, y.reshape(b * t, 256), bv, g, lw['mix'])
    q_dtype = BF16 if t % 16 == 0 else F32
    (qx,) = norm_proj(x, lw['norm_x'], lw['wq'], lw['xq_norm'], X_HEADS, X_HEAD_DIM ** -0.5, q_dtype)
    x = xattn(x, qx, mem_k, mem_v, lw['wo_x'], b, t)
    x = ffn(x, lw['norm_ffn'], lw['wg'], lw['wu'], lw['wo'])
    return x, ckv, kpe, conv_new, shift_new, wkv_new, v_sgu


def kernel(x_prompt, x_sample, mem_prompt, cache_ckv, cache_kpe, cache_mem_k, cache_mem_v, state_conv, state_shift, state_wkv, page_table, norm_mix, w_in, mla_q_norm, mla_kv_norm, mla_w_uq, mla_w_uk, mla_w_uv, mla_gq_nope, mla_gq_rope, mla_gk_nope, mla_gk_rope, conv_w, conv_b, conv_norm_g, conv_norm_b, conv_pw, sgu_norm_g, sgu_norm_b, sgu_w, sgu_b, rw_mu, rw_w0, rw_w2, rw_a0, rw_a2, rw_g2, rw_kk, rw_ka, rw_rk, rw_ln_g, rw_ln_b, out_norm, w_out, norm_x, mem_norm, wq_x, wk_x, wv_x, xq_norm, xk_norm, wo_x, norm_ffn, w_ffn_in, w_ffn_out):
    params = dict(
        norm_mix=norm_mix, w_in=w_in, mla_q_norm=mla_q_norm, mla_kv_norm=mla_kv_norm, mla_w_uq=mla_w_uq,
        mla_w_uk=mla_w_uk, mla_w_uv=mla_w_uv, mla_gq_nope=mla_gq_nope, mla_gq_rope=mla_gq_rope,
        mla_gk_nope=mla_gk_nope, mla_gk_rope=mla_gk_rope, conv_w=conv_w, conv_b=conv_b, conv_norm_g=conv_norm_g,
        conv_norm_b=conv_norm_b, conv_pw=conv_pw, sgu_norm_g=sgu_norm_g, sgu_norm_b=sgu_norm_b, sgu_w=sgu_w,
        sgu_b=sgu_b, rw_mu=rw_mu, rw_w0=rw_w0, rw_w2=rw_w2, rw_a0=rw_a0, rw_a2=rw_a2, rw_g2=rw_g2, rw_kk=rw_kk,
        rw_ka=rw_ka, rw_rk=rw_rk, rw_ln_g=rw_ln_g, rw_ln_b=rw_ln_b, out_norm=out_norm, w_out=w_out, norm_x=norm_x,
        mem_norm=mem_norm, wq_x=wq_x, wk_x=wk_x, wv_x=wv_x, xq_norm=xq_norm, xk_norm=xk_norm, wo_x=wo_x,
        norm_ffn=norm_ffn, w_ffn_in=w_ffn_in, w_ffn_out=w_ffn_out)
    depth = w_in.shape[0]
    bp, tp, d = x_prompt.shape
    bs, ts, _ = x_sample.shape
    n_mem = mem_prompt.shape[1]
    n_pages = page_table.shape[1]
    past_len = n_pages * PAGE

    tm_p = _tile(tp, 512)
    tabs_p = _rope_tables(jnp.arange(tp, dtype=jnp.int32))
    tm_s = _tile(bs * ts, 512)
    tabs_s = tuple(jnp.tile(a, (tm_s // ts, 1)) for a in _rope_tables(past_len + jnp.arange(ts, dtype=jnp.int32)))
    eh = np.zeros((N_HEADS * ts, 256), np.float32)
    for h in range(N_HEADS):
        eh[h * ts:(h + 1) * ts, h * HEAD_DIM:(h + 1) * HEAD_DIM] = 1.0 / HEAD_DIM
    eh = jnp.asarray(eh, BF16)

    y_p = x_prompt.reshape(bp * tp, d)
    y_s = x_sample.reshape(bs * ts, d)
    mem_flat = mem_prompt.reshape(bp * n_mem, d)
    zeros_conv = jnp.zeros((bp, CONV_W - 1, GROUP_W), F32)
    zeros_shift = jnp.zeros((bp, PD_W), F32)
    zeros_wkv = jnp.zeros((bp, N_HEADS, HEAD_DIM, HEAD_DIM), F32)
    outs_p, outs_s, memk_l, memv_l = [], [], [], []
    for l in range(depth):
        lw = _layer_weights({k_: v_[l] for k_, v_ in params.items()})

        mk, mv = norm_proj(mem_flat, lw['mem_norm'], lw['wkv'], lw['xk_norm'], X_HEADS, 1.0, F32)
        memk_l.append(mk.reshape(bp, n_mem, X_HEADS, X_HEAD_DIM))
        memv_l.append(mv.reshape(bp, n_mem, X_HEADS, X_HEAD_DIM))

        def attend_p(q, k, v, ckv, kpe):
            return mla_attn_prompt(q, k, v, bp, tp)

        res = _trunk_layer(y_p, bp, tp, lw, tabs_p, tp // tm_p, attend_p, mk.reshape(bp, n_mem, 512),
                           mv.reshape(bp, n_mem, 512), zeros_conv, zeros_shift, zeros_wkv)
        y_p = res[0]
        outs_p.append(res[1:])

        def attend_s(q, k, v, ckv, kpe, l=l, lw=lw):
            qf = q.astype(F32).reshape(bs, ts, N_HEADS, LANES)
            qn = (qf[..., 64:] * lw['gk_nope']).transpose(0, 2, 1, 3)
            qbd = jnp.einsum('bhtj,hg->bhtgj', qn, jnp.eye(N_HEADS, dtype=F32)).reshape(bs, N_HEADS * ts, 256)
            qpe = qf[..., :QK_ROPE].transpose(0, 2, 1, 3).reshape(bs, N_HEADS * ts, QK_ROPE)
            pad = lambda a: jnp.pad(a.reshape(bs, ts, -1), ((0, 0), (0, PAGE - ts), (0, 0)))
            o = mla_attn_sample(page_table, qbd.astype(BF16), qpe.astype(BF16), pad(ckv), pad(kpe), lw['wuk'], eh,
                                lw['wuv'], cache_ckv, cache_kpe, l)
            return o.reshape(bs * ts, 256)

        res = _trunk_layer(y_s, bs, ts, lw, tabs_s, 1, attend_s, cache_mem_k[l].reshape(bs, n_mem, 512),
                           cache_mem_v[l].reshape(bs, n_mem, 512), state_conv[l], state_shift[l], state_wkv[l])
        y_s = res[0]
        outs_s.append(res[1:])

    n_pp = tp // PAGE
    stack = lambda outs, i, ax: jnp.stack([o[i] for o in outs], axis=ax)
    ckv_prompt = stack(outs_p, 0, 0).reshape(depth, bp, n_pp, PAGE, KV_LORA).transpose(1, 2, 0, 3, 4)
    kpe_prompt = stack(outs_p, 1, 0).reshape(depth, bp, n_pp, PAGE, QK_ROPE).transpose(1, 2, 0, 3, 4)
    ckv_sample = stack(outs_s, 0, 0).reshape(depth, bs, ts, KV_LORA).transpose(1, 0, 2, 3)
    kpe_sample = stack(outs_s, 1, 0).reshape(depth, bs, ts, QK_ROPE).transpose(1, 0, 2, 3)
    return (y_p.reshape(bp, tp, d), y_s.reshape(bs, ts, d), ckv_prompt, kpe_prompt, ckv_sample, kpe_sample,
            jnp.stack(memk_l, 0), jnp.stack(memv_l, 0),
            stack(outs_p, 2, 0), stack(outs_s, 2, 0), stack(outs_p, 3, 0), stack(outs_s, 3, 0),
            stack(outs_p, 4, 0), stack(outs_s, 4, 0),
            stack(outs_s, 5, 0).reshape(depth, bs, ts, GROUP_W))
```

```python
import functools

import numpy as np
import jax
import jax.numpy as jnp
from jax import lax
from jax.experimental import pallas as pl
from jax.experimental.pallas import tpu as pltpu

F32 = jnp.float32
BF16 = jnp.bfloat16

EPS = 1e-6
LN_EPS = 1e-5
RW_LN_EPS = 64e-5
NEG = -1e30
ROPE_BASE = 10000.0

LANES = 128
SUBLANES = 8
VMEM_LIMIT_BYTES = 56 * 1024 * 1024

GROUP_W = 256
HEAD_DIM = 64
N_HEADS = 4
Q_LORA = 192
KV_LORA = 128
QK_ROPE = 32
QK_NOPE = 64
CONV_W = 31
CHUNK = 128
PAGE = 128
X_HEADS = 4
X_HEAD_DIM = 128
MLA_SCALE = (QK_NOPE + QK_ROPE) ** -0.5
PA_W = 512
PD_W = 1024
PAGES_PER_STEP = 16
SCAN_NB_PROMPT = 4
SCAN_NB_SAMPLE = 8


def _cparams(*sem):
    return pltpu.CompilerParams(dimension_semantics=sem, vmem_limit_bytes=VMEM_LIMIT_BYTES)


def _dot(a, b):
    return jnp.dot(a, b, preferred_element_type=F32)


def _dot_nt(a, b):
    return lax.dot_general(a, b, (((1,), (1,)), ((), ())), preferred_element_type=F32)


def _seg_sum(x, e):
    hi = x.astype(BF16)
    lo = (x - hi.astype(F32)).astype(BF16)
    return _dot(hi, e) + _dot(lo, e)


def _rms(x, width=None):
    w = x.shape[-1] if width is None else width
    return x * lax.rsqrt(jnp.sum(x * x, axis=-1, keepdims=True) * (1.0 / w) + EPS)


def _sigmoid(x):
    return 1.0 / (1.0 + jnp.exp(-x))


def _tile(n, pref):
    t = min(n, pref)
    while n % t:
        t //= 2
    return t


def _full(shape):
    nd = len(shape)
    return pl.BlockSpec(shape, lambda *a: (0,) * nd)


def _proj_in_kernel(x_ref, g_ref, w_ref, pa_ref, pb_ref, pc_ref, pd_ref):
    hb = (_rms(x_ref[...]) * g_ref[...]).astype(BF16)
    pa_ref[...] = _dot(hb, w_ref[:, 0:512])
    pb_ref[...] = _dot(hb, w_ref[:, 512:1024])
    pc_ref[...] = _dot(hb, w_ref[:, 1024:1536])
    pd_ref[...] = _dot(hb, w_ref[:, 1536:2560])


def proj_in(x, g, w):
    n, d = x.shape
    tm = _tile(n, 512)
    row = lambda wd: pl.BlockSpec((tm, wd), lambda i: (i, 0))
    return pl.pallas_call(
        _proj_in_kernel,
        grid=(n // tm,),
        in_specs=[row(d), _full(g.shape), _full(w.shape)],
        out_specs=[row(512), row(512), row(512), row(1024)],
        out_shape=[jax.ShapeDtypeStruct((n, wd), F32) for wd in (512, 512, 512, 1024)],
        compiler_params=_cparams("parallel"),
        name="proj_in",
    )(x, g, w)


def _rope128(x, c, sa, sb):
    w = x.shape[-1]
    return x * c + pltpu.roll(x, w - 16, 1) * sa + pltpu.roll(x, 16, 1) * sb


def _mla_prep_kernel(pa_ref, c_ref, sa_ref, sb_ref, gqn_ref, gkv_ref, wuq_ref, eq_ref, gq_ref,
                     gkr_ref, wuk_ref, ek_ref, gk_ref, wuv_ref,
                     q_ref, k_ref, v_ref, ckv_ref, kpe_ref):
    pa = pa_ref[...]
    c, sa, sb = c_ref[...], sa_ref[...], sb_ref[...]
    c4 = jnp.concatenate([c] * 4, axis=1)
    sa4 = jnp.concatenate([sa] * 4, axis=1)
    sb4 = jnp.concatenate([sb] * 4, axis=1)
    cq = _rms(pa[:, 0:256], Q_LORA) * gqn_ref[...]
    q = _dot(cq.astype(BF16), wuq_ref[...])
    qn = q * lax.rsqrt(_seg_sum(q * q, eq_ref[...]) + EPS) * gq_ref[...]
    q_ref[...] = (_rope128(qn, c4, sa4, sb4) * MLA_SCALE).astype(BF16)
    ckv = _rms(pa[:, 256:384]) * gkv_ref[...]
    ckv_ref[...] = ckv
    kp = _rms(pa[:, 384:512], QK_ROPE) * gkr_ref[...]
    kr = _rope128(kp, c, sa, sb)
    kpe_ref[...] = kr[:, 0:QK_ROPE]
    ckv_b = ckv.astype(BF16)
    kn = _dot(ckv_b, wuk_ref[...])
    kn = kn * lax.rsqrt(_seg_sum(kn * kn, ek_ref[...]) + EPS) * gk_ref[...]
    k_ref[...] = (kn + jnp.concatenate([kr] * 4, axis=1)).astype(BF16)
    v_ref[...] = _dot(ckv_b, wuv_ref[...]).astype(BF16)


def mla_prep(pa, tabs, wts, n_tab_blocks):
    n = pa.shape[0]
    c, sa, sb = tabs
    tm = c.shape[0] // n_tab_blocks
    assert n % tm == 0
    row = lambda wd: pl.BlockSpec((tm, wd), lambda i: (i, 0))
    tab = pl.BlockSpec((tm, LANES), lambda i: (i % n_tab_blocks, 0))
    return pl.pallas_call(
        _mla_prep_kernel,
        grid=(n // tm,),
        in_specs=[row(PA_W), tab, tab, tab] + [_full(w.shape) for w in wts],
        out_specs=[row(512), row(512), row(256), row(KV_LORA), row(QK_ROPE)],
        out_shape=[jax.ShapeDtypeStruct((n, 512), BF16), jax.ShapeDtypeStruct((n, 512), BF16),
                   jax.ShapeDtypeStruct((n, 256), BF16), jax.ShapeDtypeStruct((n, KV_LORA), F32),
                   jax.ShapeDtypeStruct((n, QK_ROPE), F32)],
        compiler_params=_cparams("parallel"),
        name="mla_prep",
    )(pa, c, sa, sb, *wts)


def _mla_attn_prompt_kernel(q_ref, k_ref, v_ref, o_ref, *, tq):
    i = pl.program_id(1)
    outs = []
    for h in range(N_HEADS):
        q = q_ref[:, h * LANES:(h + 1) * LANES]

        def blk(j, carry, masked, q=q, h=h):
            m, l, acc = carry
            off = pl.multiple_of(j * tq, tq)
            kb = k_ref[pl.ds(off, tq), h * LANES:(h + 1) * LANES]
            vb = v_ref[pl.ds(off, tq), h * HEAD_DIM:(h + 1) * HEAD_DIM]
            s = _dot_nt(q, kb)
            if masked:
                r = lax.broadcasted_iota(jnp.int32, (tq, tq), 0)
                cc = lax.broadcasted_iota(jnp.int32, (tq, tq), 1)
                s = jnp.where(cc <= r, s, NEG)
            m_new = jnp.maximum(m, jnp.max(s, axis=-1, keepdims=True))
            corr = jnp.exp(m - m_new)
            p = jnp.exp(s - m_new)
            l = l * corr + jnp.sum(p, axis=-1, keepdims=True)
            acc = acc * corr + _dot(p.astype(BF16), vb)
            return m_new, l, acc

        init = (jnp.full((tq, 1), NEG, F32), jnp.zeros((tq, 1), F32), jnp.zeros((tq, HEAD_DIM), F32))
        carry = lax.fori_loop(0, i, functools.partial(blk, masked=False), init)
        m, l, acc = blk(i, carry, True)
        outs.append(acc / l)
    o_ref[...] = jnp.concatenate(outs, axis=1)


def mla_attn_prompt(q, k, v, b, t):
    tq = _tile(t, 256)
    nq = t // tq
    return pl.pallas_call(
        functools.partial(_mla_attn_prompt_kernel, tq=tq),
        grid=(b, nq),
        in_specs=[pl.BlockSpec((tq, 512), lambda bi, i: (bi * nq + i, 0)),
                  pl.BlockSpec((t, 512), lambda bi, i: (bi, 0)),
                  pl.BlockSpec((t, 256), lambda bi, i: (bi, 0))],
        out_specs=pl.BlockSpec((tq, 256), lambda bi, i: (bi * nq + i, 0)),
        out_shape=jax.ShapeDtypeStruct((b * t, 256), F32),
        compiler_params=_cparams("parallel", "arbitrary"),
        name="mla_attn_prompt",
    )(q, k, v)


N_SUB = 4


def _mla_attn_sample_kernel(pt_ref, qbd_ref, qpe_ref, ckvn_ref, kpen_ref, wuk_ref, wukt_ref, wuv_ref, ckv_hbm,
                            kpe_hbm, o_ref, ckv_buf, kpe_buf, sem, m_scr, l_scr, acc_scr, lhs_scr, *, n_pg, tq,
                            layer):
    b_id = pl.program_id(0)
    p_id = pl.program_id(1)
    n_steps = pl.num_programs(1)
    step = b_id * n_steps + p_id
    slot = step % 2

    def page_copies(b, p, sl):
        copies = []
        for k in range(n_pg):
            phys = pt_ref[b, p * n_pg + k]
            copies.append(pltpu.make_async_copy(ckv_hbm.at[phys, layer], ckv_buf.at[sl, k], sem.at[sl, 0]))
            copies.append(pltpu.make_async_copy(kpe_hbm.at[phys, layer], kpe_buf.at[sl, k], sem.at[sl, 1]))
        return copies

    @pl.when(step == 0)
    def _():
        for c in page_copies(b_id, p_id, slot):
            c.start()

    @pl.when(step + 1 < pl.num_programs(0) * n_steps)
    def _():
        wrap = p_id + 1 == n_steps
        for c in page_copies(jnp.where(wrap, b_id + 1, b_id), jnp.where(wrap, 0, p_id + 1), 1 - slot):
            c.start()

    qpe = qpe_ref[0]
    nr = N_HEADS * tq

    @pl.when(p_id == 0)
    def _():
        lhs_scr[0:GROUP_W, :] = wukt_ref[...]
        lhs_scr[GROUP_W:GROUP_W + nr, :] = _dot_nt(qbd_ref[0], wuk_ref[...]).astype(BF16)

    def scores(ckv, kpe_t):
        ckv_b = ckv.astype(BF16)
        out = _dot_nt(lhs_scr[...], ckv_b)
        rinv = []
        for h in range(N_HEADS):
            kn = out[h * HEAD_DIM:(h + 1) * HEAD_DIM]
            ss = jnp.sum(kn * kn, axis=0, keepdims=True) * (1.0 / HEAD_DIM)
            rinv.append(jnp.broadcast_to(lax.rsqrt(ss + EPS), (tq, ss.shape[1])))
        s = out[GROUP_W:GROUP_W + nr] * jnp.concatenate(rinv, axis=0) + _dot(qpe, kpe_t.astype(BF16))
        return s, ckv_b

    @pl.when(p_id == 0)
    def _():
        s, ckv_b = scores(ckvn_ref[0], kpen_ref[0])
        r = lax.broadcasted_iota(jnp.int32, s.shape, 0) % tq
        cc = lax.broadcasted_iota(jnp.int32, s.shape, 1)
        s = jnp.where(cc <= r, s, NEG)
        m = jnp.max(s, axis=-1, keepdims=True)
        p = jnp.exp(s - m)
        m_scr[...] = m
        l_scr[...] = jnp.sum(p, axis=-1, keepdims=True)
        acc_scr[...] = _dot(p.astype(BF16), ckv_b)

    for c in page_copies(b_id, p_id, slot):
        c.wait()

    n_sub = min(N_SUB, n_pg)
    per = n_pg // n_sub
    s_parts, ckv_parts = [], []
    for g in range(n_sub):
        ckv = jnp.concatenate([ckv_buf[slot, g * per + k] for k in range(per)], axis=0)
        kpe_t = jnp.concatenate([kpe_buf[slot, g * per + k] for k in range(per)], axis=1)
        s, ckv_b = scores(ckv, kpe_t)
        s_parts.append(s)
        ckv_parts.append(ckv_b)
    m = m_scr[...]
    m_new = m
    for s in s_parts:
        m_new = jnp.maximum(m_new, jnp.max(s, axis=-1, keepdims=True))
    corr = jnp.exp(m - m_new)
    l = l_scr[...] * corr
    acc = acc_scr[...] * corr
    for s, ckv_b in zip(s_parts, ckv_parts):
        p = jnp.exp(s - m_new)
        l = l + jnp.sum(p, axis=-1, keepdims=True)
        acc = acc + _dot(p.astype(BF16), ckv_b)
    m_scr[...] = m_new
    l_scr[...] = l
    acc_scr[...] = acc

    @pl.when(p_id == n_steps - 1)
    def _():
        lat = (acc / l).astype(BF16)
        full = _dot(lat, wuv_ref[...])
        lane_head = lax.broadcasted_iota(jnp.int32, (tq, 256), 1) // HEAD_DIM
        out = jnp.zeros((tq, 256), F32)
        for h in range(N_HEADS):
            out = jnp.where(lane_head == h, full[h * tq:(h + 1) * tq, :], out)
        o_ref[0] = out


def mla_attn_sample(page_table, qbd, qpe, ckv_new, kpe_new_t, wuk, wukt, wuv, cache_ckv, cache_kpe_t, layer):
    bs, n_pages = page_table.shape
    tq = qbd.shape[1] // N_HEADS
    n_pg = min(PAGES_PER_STEP, n_pages)
    assert n_pages % n_pg == 0
    nr = N_HEADS * tq
    per_b = lambda shp: pl.BlockSpec((1,) + shp, lambda b, p, pt: (b, 0, 0))
    cst = lambda shp: pl.BlockSpec(shp, lambda b, p, pt: (0,) * len(shp))
    hbm = pl.BlockSpec(memory_space=pl.ANY)
    grid_spec = pltpu.PrefetchScalarGridSpec(
        num_scalar_prefetch=1,
        grid=(bs, n_pages // n_pg),
        in_specs=[per_b((nr, 256)), per_b((nr, QK_ROPE)), per_b((PAGE, KV_LORA)), per_b((QK_ROPE, PAGE)),
                  cst(wuk.shape), cst(wukt.shape), cst(wuv.shape), hbm, hbm],
        out_specs=pl.BlockSpec((1, tq, 256), lambda b, p, pt: (b, 0, 0)),
        scratch_shapes=[pltpu.VMEM((2, n_pg, PAGE, KV_LORA), F32), pltpu.VMEM((2, n_pg, QK_ROPE, PAGE), F32),
                        pltpu.SemaphoreType.DMA((2, 2)),
                        pltpu.VMEM((nr, 1), F32), pltpu.VMEM((nr, 1), F32), pltpu.VMEM((nr, KV_LORA), F32),
                        pltpu.VMEM((GROUP_W + nr, KV_LORA), BF16)],
    )
    return pl.pallas_call(
        functools.partial(_mla_attn_sample_kernel, n_pg=n_pg, tq=tq, layer=layer),
        grid_spec=grid_spec,
        out_shape=jax.ShapeDtypeStruct((bs, tq, 256), F32),
        compiler_params=_cparams("arbitrary", "arbitrary"),
        name="mla_attn_sample",
    )(page_table, qbd, qpe, ckv_new, kpe_new_t, wuk, wukt, wuv, cache_ckv, cache_kpe_t)


CONV_HALO = 32


def _conv_kernel(pb_ref, st_ref, cw_ref, cb_ref, e_ref, g_ref, b_ref, pw_ref, o_ref, st_out_ref, xbuf, *, tt):
    j = pl.program_id(1)
    lo = CONV_HALO - (CONV_W - 1)

    @pl.when(j == 0)
    def _():
        xbuf[pl.ds(lo, CONV_W - 1), :] = st_ref[0]

    pb = pb_ref[...]
    xbuf[pl.ds(CONV_HALO, tt), :] = pb[:, 0:256] * _sigmoid(pb[:, 256:512])
    y = jnp.zeros((tt, 256), F32) + cb_ref[...]
    for k in range(CONV_W):
        y = y + xbuf[pl.ds(lo + k, tt), :] * cw_ref[pl.ds(k, 1), :]
    new_state = xbuf[pl.ds(lo + tt, CONV_W - 1), :]
    xbuf[pl.ds(lo, CONV_W - 1), :] = new_state
    st_out_ref[0] = new_state
    e = e_ref[...]
    yc = y - _seg_sum(y, e)
    yn = yc * lax.rsqrt(_seg_sum(yc * yc, e) + LN_EPS) * g_ref[...] + b_ref[...]
    act = yn * _sigmoid(yn)
    o_ref[...] = _dot(act.astype(BF16), pw_ref[...])


def conv_module(pb, state, wts, b, t):
    tt = _tile(t, 256)
    nt = t // tt
    return pl.pallas_call(
        functools.partial(_conv_kernel, tt=tt),
        grid=(b, nt),
        in_specs=[pl.BlockSpec((tt, 512), lambda bi, j: (bi * nt + j, 0)),
                  pl.BlockSpec((1, CONV_W - 1, 256), lambda bi, j: (bi, 0, 0))]
                 + [_full(w.shape) for w in wts],
        out_specs=[pl.BlockSpec((tt, 256), lambda bi, j: (bi * nt + j, 0)),
                   pl.BlockSpec((1, CONV_W - 1, 256), lambda bi, j: (bi, 0, 0))],
        out_shape=[jax.ShapeDtypeStruct((b * t, 256), F32), jax.ShapeDtypeStruct((b, CONV_W - 1, 256), F32)],
        scratch_shapes=[pltpu.VMEM((CONV_HALO + tt, 256), F32)],
        compiler_params=_cparams("parallel", "arbitrary"),
        name="conv_module",
    )(pb, state, *wts)


def _sgu_kernel(pc_ref, g_ref, b_ref, w_ref, bias_ref, o_ref, v_ref, *, n_chunks):
    x = pc_ref[...]
    z = 0.5 * x * (1.0 + jnp.tanh(0.7978845608028654 * (x + 0.044715 * (x * x * x))))
    u = z[:, 0:256]
    v = z[:, 256:512]
    vc = v - jnp.mean(v, axis=-1, keepdims=True)
    v = vc * lax.rsqrt(jnp.mean(vc * vc, axis=-1, keepdims=True) + LN_EPS) * g_ref[...] + b_ref[...]
    v_ref[...] = v
    lane_head = lax.broadcasted_iota(jnp.int32, (CHUNK, 256), 1) // HEAD_DIM
    for c in range(n_chunks):
        vcb = v[c * CHUNK:(c + 1) * CHUNK, :]
        sv = bias_ref[...]
        for h in range(N_HEADS):
            sv = sv + _dot(w_ref[h], jnp.where(lane_head == h, vcb, 0.0).astype(BF16))
        o_ref[pl.ds(c * CHUNK, CHUNK), :] = u[c * CHUNK:(c + 1) * CHUNK, :] * sv


def sgu(pc, wts):
    n = pc.shape[0]
    tm = _tile(n, 512)
    row = lambda wd: pl.BlockSpec((tm, wd), lambda i: (i, 0))
    return pl.pallas_call(
        functools.partial(_sgu_kernel, n_chunks=tm // CHUNK),
        grid=(n // tm,),
        in_specs=[row(512)] + [_full(w.shape) for w in wts],
        out_specs=[row(256), row(256)],
        out_shape=[jax.ShapeDtypeStruct((n, 256), F32), jax.ShapeDtypeStruct((n, 256), F32)],
        compiler_params=_cparams("parallel"),
        name="sgu",
    )(pc, *wts)


RW_HALO = 8


def _rwkv_prep_kernel(pd_ref, sh_ref, mu_ref, w0_ref, w2_ref, a0_ref, a2_ref, g2_ref, kkp_ref, ka_ref, rk_ref,
                      e_ref, r_ref, w_ref, k_ref, v_ref, kk_ref, kka_ref, g_ref, bv_ref, xbuf, *, tt):
    j = pl.program_id(1)

    @pl.when(j == 0)
    def _():
        xbuf[pl.ds(RW_HALO - 1, 1), :] = sh_ref[0]

    pd = pd_ref[...]
    xbuf[pl.ds(RW_HALO, tt), :] = pd
    prev = xbuf[pl.ds(RW_HALO - 1, tt), :]
    xbuf[pl.ds(RW_HALO - 1, 1), :] = pd[tt - 1:tt, :]
    xs = pd + (prev - pd) * mu_ref[...]
    r = xs[:, 0:256]
    k = xs[:, 256:512]
    v = xs[:, 512:768]
    xwa = xs[:, 768:896]
    xg = xs[:, 896:1024]
    z = -(w0_ref[...] + _dot(jnp.tanh(xwa).astype(BF16), w2_ref[...]))
    softplus = jnp.maximum(z, 0.0) + jnp.log(1.0 + jnp.exp(-jnp.abs(z)))
    w_ref[...] = jnp.exp(-jnp.exp(-softplus - 0.5))
    a = _sigmoid(a0_ref[...] + _dot(xwa.astype(BF16), a2_ref[...]))
    g_ref[...] = _dot(_sigmoid(xg).astype(BF16), g2_ref[...])
    e = e_ref[...]
    kk = k * kkp_ref[...]
    kk = kk * lax.rsqrt(_seg_sum(kk * kk, e) + 1e-12)
    k2 = k * (1.0 + (a - 1.0) * ka_ref[...])
    r_ref[...] = r
    k_ref[...] = k2
    v_ref[...] = v
    kk_ref[...] = kk
    kka_ref[...] = kk * a
    bv_ref[...] = _seg_sum(r * k2 * rk_ref[...], e) * v


def rwkv_prep(pd, shift, wts, b, t):
    tt = _tile(t, 256)
    nt = t // tt
    row = lambda wd: pl.BlockSpec((tt, wd), lambda bi, j: (bi * nt + j, 0))
    return pl.pallas_call(
        functools.partial(_rwkv_prep_kernel, tt=tt),
        grid=(b, nt),
        in_specs=[row(PD_W), pl.BlockSpec((1, 1, PD_W), lambda bi, j: (bi, 0, 0))]
                 + [_full(w.shape) for w in wts],
        out_specs=[row(256)] * 8,
        out_shape=[jax.ShapeDtypeStruct((b * t, 256), F32)] * 8,
        scratch_shapes=[pltpu.VMEM((RW_HALO + tt, PD_W), F32)],
        compiler_params=_cparams("parallel", "arbitrary"),
        name="rwkv_prep",
    )(pd, shift, *wts)


RW_BLOCK = 128


def _pack_bf16_pair(a, b):
    ua = lax.bitcast_convert_type(a.astype(BF16).astype(F32), jnp.uint32)
    ub = lax.bitcast_convert_type(b.astype(BF16).astype(F32), jnp.uint32)
    return lax.bitcast_convert_type(ua | (ub >> 16), jnp.int32)


def _unpack_bf16_pair(word):
    u = lax.bitcast_convert_type(word, jnp.uint32)
    return (lax.bitcast_convert_type(u & jnp.uint32(0xFFFF0000), F32),
            lax.bitcast_convert_type(u << 16, F32))


def _rwkv_scan_kernel(w_ref, kk_ref, kka_ref, k_ref, r_ref, v_ref, s0_ref, y_ref, sf_ref, s_scr, col_scr, *, nb, sblk):
    c = pl.program_id(1)

    @pl.when(c == 0)
    def _():
        s_scr[...] = s0_ref[...]

    lane = lax.broadcasted_iota(jnp.int32, (HEAD_DIM, LANES), 1)
    low = lane < HEAD_DIM
    n_half = 2 if sblk > HEAD_DIM else 1
    for b in range(nb):
        for p in range(2):
            blk = lambda ref: ref[b, :, p * LANES:(p + 1) * LANES]
            w = blk(w_ref)
            w_hi = w.astype(BF16).astype(F32)
            words = (_pack_bf16_pair(w_hi, w - w_hi), _pack_bf16_pair(blk(kk_ref), blk(kka_ref)),
                     _pack_bf16_pair(blk(k_ref), blk(r_ref)))
            for wi, x in enumerate(words):
                if sblk < LANES:
                    x = jnp.concatenate([x, jnp.zeros((LANES - sblk, LANES), jnp.int32)], axis=0)
                xt = x.T
                h0, h1 = xt[0:HEAD_DIM], xt[HEAD_DIM:LANES]
                col_scr[wi, b, p, 0] = jnp.where(low, h0, pltpu.roll(h1, HEAD_DIM, 1))
                if n_half == 2:
                    col_scr[wi, b, p, 1] = jnp.where(low, pltpu.roll(h0, HEAD_DIM, 1), h1)

    base = jnp.where(low, 0, HEAD_DIM)
    for half in range(n_half):
        n_groups = min(sblk - half * HEAD_DIM, HEAD_DIM) // SUBLANES

        def group(gi, carry, half=half):
            row0 = pl.multiple_of(half * HEAD_DIM + gi * SUBLANES, SUBLANES)
            for b in range(nb):
                v8 = v_ref[b, pl.ds(row0, SUBLANES), :]
                ys = []
                for p in range(2):
                    st = s_scr[b, :, p * LANES:(p + 1) * LANES]
                    yp = []
                    for i in range(SUBLANES):
                        idx = base + (gi * SUBLANES + i)
                        col = lambda wi: _unpack_bf16_pair(
                            jnp.take_along_axis(col_scr[wi, b, p, half], idx, axis=1))
                        w_hi, w_lo = col(0)
                        kk, kka = col(1)
                        k, r = col(2)
                        vrow = v8[i:i + 1, p * LANES:(p + 1) * LANES]
                        sa = -jnp.sum(st * kk, axis=0, keepdims=True)
                        st = st * (w_hi + w_lo) + kka * sa + k * vrow
                        yp.append(jnp.sum(st * r, axis=0, keepdims=True))
                    s_scr[b, :, p * LANES:(p + 1) * LANES] = st
                    ys.append(jnp.concatenate(yp, axis=0))
                y_ref[b, pl.ds(row0, SUBLANES), :] = jnp.concatenate(ys, axis=1)
            return carry

        lax.fori_loop(0, n_groups, group, 0)

    @pl.when(c == pl.num_programs(1) - 1)
    def _():
        sf_ref[...] = s_scr[...]


def rwkv_scan(seqs, v, s0, nb):
    b, t, _ = v.shape
    sblk = min(t, RW_BLOCK)
    assert b % nb == 0 and t % sblk == 0 and sblk % SUBLANES == 0 and (sblk <= HEAD_DIM or sblk == RW_BLOCK)
    st_spec = pl.BlockSpec((nb, HEAD_DIM, 256), lambda bi, c: (bi, 0, 0))
    seq_spec = pl.BlockSpec((nb, sblk, 256), lambda bi, c: (bi, c, 0))
    return pl.pallas_call(
        functools.partial(_rwkv_scan_kernel, nb=nb, sblk=sblk),
        grid=(b // nb, t // sblk),
        in_specs=[seq_spec] * 6 + [st_spec],
        out_specs=[seq_spec, st_spec],
        out_shape=[jax.ShapeDtypeStruct((b, t, 256), F32), jax.ShapeDtypeStruct((b, HEAD_DIM, 256), F32)],
        scratch_shapes=[pltpu.VMEM((nb, HEAD_DIM, 256), F32),
                        pltpu.VMEM((3, nb, 2, 2, HEAD_DIM, LANES), jnp.int32)],
        compiler_params=_cparams("parallel", "arbitrary"),
        name="rwkv_scan",
    )(*seqs, v, s0)


def _mix_out_kernel(x_ref, oa_ref, ob_ref, oc_ref, y_ref, bv_ref, g_ref, e_ref, lg_ref, lb_ref, on_ref, w_ref,
                    o_ref):
    e = e_ref[...]
    y = y_ref[...]
    yc = y - _seg_sum(y, e)
    yn = yc * lax.rsqrt(_seg_sum(yc * yc, e) + RW_LN_EPS) * lg_ref[...] + lb_ref[...]
    od = (yn + bv_ref[...]) * g_ref[...]
    acc = x_ref[...]
    for gi, o in enumerate((oa_ref[...], ob_ref[...], oc_ref[...], od)):
        on = (_rms(o) * on_ref[:, gi * 256:(gi + 1) * 256]).astype(BF16)
        acc = acc + _dot(on, w_ref[pl.ds(gi * 256, 256), :])
    o_ref[...] = acc


def mix_out(x, oa, ob, oc, y, bv, g, wts):
    n, d = x.shape
    tm = _tile(n, 512)
    row = lambda wd: pl.BlockSpec((tm, wd), lambda i: (i, 0))
    return pl.pallas_call(
        _mix_out_kernel,
        grid=(n // tm,),
        in_specs=[row(d)] + [row(256)] * 6 + [_full(w.shape) for w in wts],
        out_specs=row(d),
        out_shape=jax.ShapeDtypeStruct((n, d), F32),
        compiler_params=_cparams("parallel"),
        name="mix_out",
    )(x, oa, ob, oc, y, bv, g, *wts)


def _norm_proj_kernel(x_ref, g_ref, w_ref, hg_ref, *o_refs, n_norm_heads, scale):
    hb = (_rms(x_ref[...]) * g_ref[...]).astype(BF16)
    y = _dot(hb, w_ref[...])
    for i, o_ref in enumerate(o_refs):
        parts = []
        for h in range(X_HEADS):
            blk = y[:, (i * X_HEADS + h) * X_HEAD_DIM:(i * X_HEADS + h + 1) * X_HEAD_DIM]
            if i * X_HEADS + h < n_norm_heads:
                blk = _rms(blk) * hg_ref[...] * scale
            parts.append(blk)
        o_ref[...] = jnp.concatenate(parts, axis=1).astype(o_ref.dtype)


def norm_proj(x, g, w, hg, n_norm_heads, scale, out_dtype):
    n, d = x.shape
    n_out = w.shape[1] // 512
    tm = _tile(n, 512)
    row = lambda wd: pl.BlockSpec((tm, wd), lambda i: (i, 0))
    return pl.pallas_call(
        functools.partial(_norm_proj_kernel, n_norm_heads=n_norm_heads, scale=scale),
        grid=(n // tm,),
        in_specs=[row(d), _full(g.shape), _full(w.shape), _full(hg.shape)],
        out_specs=[row(512)] * n_out,
        out_shape=[jax.ShapeDtypeStruct((n, 512), out_dtype)] * n_out,
        compiler_params=_cparams("parallel"),
        name="norm_proj",
    )(x, g, w, hg)


def _xattn_kernel(x_ref, q_ref, k_ref, v_ref, wo_ref, o_ref):
    outs = []
    for h in range(X_HEADS):
        sl = slice(h * X_HEAD_DIM, (h + 1) * X_HEAD_DIM)
        s = _dot_nt(q_ref[:, sl].astype(BF16), k_ref[0, :, sl].astype(BF16))
        p = jnp.exp(s - jnp.max(s, axis=-1, keepdims=True))
        p = p / jnp.sum(p, axis=-1, keepdims=True)
        outs.append(_dot(p.astype(BF16), v_ref[0, :, sl].astype(BF16)))
    xo = jnp.concatenate(outs, axis=1).astype(BF16)
    o_ref[...] = x_ref[...] + _dot(xo, wo_ref[...])


def xattn(x, q, mem_k, mem_v, wo, b, t):
    n, d = x.shape
    tq = _tile(t, 512)
    nq = t // tq
    n_mem = mem_k.shape[1]
    row = lambda wd: pl.BlockSpec((tq, wd), lambda bi, i: (bi * nq + i, 0))
    kv = pl.BlockSpec((1, n_mem, 512), lambda bi, i: (bi, 0, 0))
    return pl.pallas_call(
        _xattn_kernel,
        grid=(b, nq),
        in_specs=[row(d), row(512), kv, kv, _full(wo.shape)],
        out_specs=row(d),
        out_shape=jax.ShapeDtypeStruct((n, d), F32),
        compiler_params=_cparams("parallel", "arbitrary"),
        name="xattn",
    )(x, q, mem_k, mem_v, wo)


FFN_CHUNK = 256


def _ffn_kernel(x_ref, g_ref, wg_ref, wu_ref, wo_ref, o_ref, *, n_chunks):
    x = x_ref[...]
    hb = (_rms(x) * g_ref[...]).astype(BF16)
    acc = x
    for c in range(n_chunks):
        a = _dot(hb, wg_ref[c])
        u = _dot(hb, wu_ref[c])
        acc = acc + _dot((a * _sigmoid(a) * u).astype(BF16), wo_ref[c])
    o_ref[...] = acc


def ffn(x, g, wg, wu, wo):
    n, d = x.shape
    tm = _tile(n, 512)
    row = pl.BlockSpec((tm, d), lambda i: (i, 0))
    return pl.pallas_call(
        functools.partial(_ffn_kernel, n_chunks=wg.shape[0]),
        grid=(n // tm,),
        in_specs=[row, _full(g.shape), _full(wg.shape), _full(wu.shape), _full(wo.shape)],
        out_specs=row,
        out_shape=jax.ShapeDtypeStruct((n, d), F32),
        compiler_params=_cparams("parallel"),
        name="ffn",
    )(x, g, wg, wu, wo)


def _seg_matrix(seg_ids, seg_len):
    s = np.asarray(seg_ids)
    m = (s[:, None] == s[None, :]) & (s[:, None] >= 0)
    return jnp.asarray(m.astype(np.float32) / np.asarray(seg_len, np.float32)[None, :], BF16)


def _mla_segments():
    lane = np.arange(512)
    blk, off = lane // 128, lane % 128
    q_ids = np.where(off < 32, 2 * blk, np.where(off < 64, -1, 2 * blk + 1))
    q_len = np.where(off < 32, 32.0, 64.0)
    k_ids = np.where(off < 64, -1, blk)
    k_len = np.full(512, 64.0)
    return _seg_matrix(q_ids, q_len), _seg_matrix(k_ids, k_len)


def _head_matrix(scale_len):
    lane = np.arange(256)
    return _seg_matrix(lane // HEAD_DIM, np.full(256, scale_len))


def _rope_tables(pos):
    half = QK_ROPE // 2
    inv = jnp.power(ROPE_BASE, -jnp.arange(half, dtype=F32) / half)
    ang = pos.astype(F32)[:, None] * inv[None, :]
    cos, sin = jnp.cos(ang), jnp.sin(ang)
    n = pos.shape[0]
    z = lambda w: jnp.zeros((n, w), F32)
    c = jnp.concatenate([cos, cos, jnp.ones((n, LANES - QK_ROPE), F32)], axis=1)
    sa = jnp.concatenate([-sin, z(LANES - half)], axis=1)
    sb = jnp.concatenate([z(half), sin, z(LANES - QK_ROPE)], axis=1)
    return c, sa, sb


def _layer_weights(p):
    row = lambda v: v.reshape(1, -1).astype(F32)
    zc = lambda a, w: jnp.zeros((a.shape[0], w), a.dtype)
    w_in = p['w_in']
    c1, c2, c3 = 352, 352 + 512, 352 + 1024
    pa = w_in[:, :c1]
    w_in_p = jnp.concatenate(
        [pa[:, :Q_LORA], zc(pa, 64), pa[:, Q_LORA:Q_LORA + KV_LORA], pa[:, Q_LORA + KV_LORA:], zc(pa, 96),
         w_in[:, c1:c2], w_in[:, c2:c3], w_in[:, c3:]], axis=1).astype(BF16)

    wuq = p['mla_w_uq'].reshape(Q_LORA, N_HEADS, QK_NOPE + QK_ROPE)
    wuq = jnp.concatenate([wuq[:, :, QK_NOPE:], jnp.zeros((Q_LORA, N_HEADS, 32), F32), wuq[:, :, :QK_NOPE]], axis=2)
    wuq = jnp.concatenate([wuq.reshape(Q_LORA, 512), jnp.zeros((256 - Q_LORA, 512), F32)], axis=0).astype(BF16)
    wuk = p['mla_w_uk'].reshape(KV_LORA, N_HEADS, QK_NOPE)
    wuk_p = jnp.concatenate([jnp.zeros((KV_LORA, N_HEADS, 64), F32), wuk], axis=2).reshape(KV_LORA, 512).astype(BF16)
    blk = lambda a, b_, c_: jnp.tile(jnp.concatenate([a, b_, c_]), N_HEADS).reshape(1, 512)
    z32, z64 = jnp.zeros((32,), F32), jnp.zeros((64,), F32)
    gq = blk(p['mla_gq_rope'], z32, p['mla_gq_nope'])
    gk = blk(z32, z32, p['mla_gk_nope'])
    gqn = jnp.concatenate([p['mla_q_norm'], z64]).reshape(1, 256)
    gkr = jnp.concatenate([p['mla_gk_rope'], jnp.zeros((96,), F32)]).reshape(1, LANES)
    eq, ek = _mla_segments()
    mla = (gqn, row(p['mla_kv_norm']), wuq, eq, gq, gkr, wuk_p, ek, gk, p['mla_w_uv'].astype(BF16))

    e64 = _head_matrix(64.0)
    conv = (p['conv_w'], row(p['conv_b']), e64, row(p['conv_norm_g']), row(p['conv_norm_b']),
            p['conv_pw'].astype(BF16))

    z64r = jnp.zeros((64, GROUP_W), F32)
    rw = (row(p['rw_mu']),
          row(p['rw_w0']), jnp.concatenate([p['rw_w2'], z64r], axis=0).astype(BF16),
          row(p['rw_a0']), jnp.concatenate([z64r, p['rw_a2']], axis=0).astype(BF16),
          p['rw_g2'].astype(BF16), row(p['rw_kk']), row(p['rw_ka']), row(p['rw_rk']), _head_matrix(1.0))

    mix = (e64, row(p['rw_ln_g']), row(p['rw_ln_b']), row(p['out_norm']), p['w_out'].astype(BF16))

    d_ff = p['w_ffn_out'].shape[0]
    nck = d_ff // FFN_CHUNK
    d = w_in.shape[0]
    wg = p['w_ffn_in'][:, :d_ff].reshape(d, nck, FFN_CHUNK).transpose(1, 0, 2).astype(BF16)
    wu = p['w_ffn_in'][:, d_ff:].reshape(d, nck, FFN_CHUNK).transpose(1, 0, 2).astype(BF16)
    wo = p['w_ffn_out'].reshape(nck, FFN_CHUNK, d).astype(BF16)

    return dict(
        norm_mix=row(p['norm_mix']), w_in=w_in_p, mla=mla, conv=conv, rw=rw, mix=mix,
        wuk=p['mla_w_uk'].astype(BF16), wuv=p['mla_w_uv'].astype(BF16), gk_nope=p['mla_gk_nope'],
        sgu_ln=(row(p['sgu_norm_g']), row(p['sgu_norm_b'])), sgu_w=p['sgu_w'], sgu_b=p['sgu_b'],
        norm_x=row(p['norm_x']), wq=p['wq_x'].astype(BF16), xq_norm=row(p['xq_norm']),
        mem_norm=row(p['mem_norm']), wkv=jnp.concatenate([p['wk_x'], p['wv_x']], axis=1).astype(BF16),
        xk_norm=row(p['xk_norm']), wo_x=p['wo_x'].astype(BF16),
        norm_ffn=row(p['norm_ffn']), wg=wg, wu=wu, wo=wo)


def _sgu_weights(lw, t):
    l = min(t, CHUNK)
    w = lw['sgu_w'][:, :l, :l] * jnp.tril(jnp.ones((l, l), F32))
    reps = CHUNK // l
    if reps > 1:
        w = jnp.einsum('ab,hij->haibj', jnp.eye(reps, dtype=F32), w).reshape(N_HEADS, CHUNK, CHUNK)
    bias = jnp.tile(lw['sgu_b'][:, :l].T, (reps, 1))
    bias = jnp.repeat(bias, HEAD_DIM, axis=1)
    return lw['sgu_ln'] + (w.astype(BF16), bias)


def _trunk_layer(x, b, t, lw, tabs, n_tab_blocks, attend, mem_k, mem_v, conv_state, shift_state, wkv_state,
                 scan_nb):
    pa, pb, pc, pd = proj_in(x, lw['norm_mix'], lw['w_in'])
    q, k, v, ckv, kpe = mla_prep(pa, tabs, lw['mla'], n_tab_blocks)
    oa = attend(q, k, v, ckv, kpe)
    ob, conv_new = conv_module(pb, conv_state, lw['conv'], b, t)
    oc, v_sgu = sgu(pc, _sgu_weights(lw, t))
    r, w, k2, vv, kk, kka, g, bv = rwkv_prep(pd, shift_state.reshape(b, 1, PD_W), lw['rw'], b, t)
    seqs = [a.reshape(b, t, 256) for a in (w, kk, kka, k2, r)]
    s0 = wkv_state.transpose(0, 3, 1, 2).reshape(b, HEAD_DIM, 256)
    y, s_fin = rwkv_scan(seqs, vv.reshape(b, t, 256), s0, scan_nb)
    wkv_new = s_fin.reshape(b, HEAD_DIM, N_HEADS, HEAD_DIM).transpose(0, 2, 3, 1)
    shift_new = pd.reshape(b, t, PD_W)[:, -1]
    x = mix_out(x, oa, ob, oc, y.reshape(b * t, 256), bv, g, lw['mix'])
    q_dtype = BF16 if t % 16 == 0 else F32
    (qx,) = norm_proj(x, lw['norm_x'], lw['wq'], lw['xq_norm'], X_HEADS, X_HEAD_DIM ** -0.5, q_dtype)
    x = xattn(x, qx, mem_k, mem_v, lw['wo_x'], b, t)
    x = ffn(x, lw['norm_ffn'], lw['wg'], lw['wu'], lw['wo'])
    return x, ckv, kpe, conv_new, shift_new, wkv_new, v_sgu


def kernel(x_prompt, x_sample, mem_prompt, cache_ckv, cache_kpe, cache_mem_k, cache_mem_v, state_conv, state_shift, state_wkv, page_table, norm_mix, w_in, mla_q_norm, mla_kv_norm, mla_w_uq, mla_w_uk, mla_w_uv, mla_gq_nope, mla_gq_rope, mla_gk_nope, mla_gk_rope, conv_w, conv_b, conv_norm_g, conv_norm_b, conv_pw, sgu_norm_g, sgu_norm_b, sgu_w, sgu_b, rw_mu, rw_w0, rw_w2, rw_a0, rw_a2, rw_g2, rw_kk, rw_ka, rw_rk, rw_ln_g, rw_ln_b, out_norm, w_out, norm_x, mem_norm, wq_x, wk_x, wv_x, xq_norm, xk_norm, wo_x, norm_ffn, w_ffn_in, w_ffn_out):
    params = dict(
        norm_mix=norm_mix, w_in=w_in, mla_q_norm=mla_q_norm, mla_kv_norm=mla_kv_norm, mla_w_uq=mla_w_uq,
        mla_w_uk=mla_w_uk, mla_w_uv=mla_w_uv, mla_gq_nope=mla_gq_nope, mla_gq_rope=mla_gq_rope,
        mla_gk_nope=mla_gk_nope, mla_gk_rope=mla_gk_rope, conv_w=conv_w, conv_b=conv_b, conv_norm_g=conv_norm_g,
        conv_norm_b=conv_norm_b, conv_pw=conv_pw, sgu_norm_g=sgu_norm_g, sgu_norm_b=sgu_norm_b, sgu_w=sgu_w,
        sgu_b=sgu_b, rw_mu=rw_mu, rw_w0=rw_w0, rw_w2=rw_w2, rw_a0=rw_a0, rw_a2=rw_a2, rw_g2=rw_g2, rw_kk=rw_kk,
        rw_ka=rw_ka, rw_rk=rw_rk, rw_ln_g=rw_ln_g, rw_ln_b=rw_ln_b, out_norm=out_norm, w_out=w_out, norm_x=norm_x,
        mem_norm=mem_norm, wq_x=wq_x, wk_x=wk_x, wv_x=wv_x, xq_norm=xq_norm, xk_norm=xk_norm, wo_x=wo_x,
        norm_ffn=norm_ffn, w_ffn_in=w_ffn_in, w_ffn_out=w_ffn_out)
    depth = w_in.shape[0]
    bp, tp, d = x_prompt.shape
    bs, ts, _ = x_sample.shape
    n_mem = mem_prompt.shape[1]
    n_pages = page_table.shape[1]
    past_len = n_pages * PAGE

    tm_p = _tile(tp, 512)
    tabs_p = _rope_tables(jnp.arange(tp, dtype=jnp.int32))
    tm_s = _tile(bs * ts, 512)
    tabs_s = tuple(jnp.tile(a, (tm_s // ts, 1)) for a in _rope_tables(past_len + jnp.arange(ts, dtype=jnp.int32)))
    cache_kpe_t = jnp.swapaxes(cache_kpe, 2, 3)
    y_p = x_prompt.reshape(bp * tp, d)
    y_s = x_sample.reshape(bs * ts, d)
    mem_flat = mem_prompt.reshape(bp * n_mem, d)
    zeros_conv = jnp.zeros((bp, CONV_W - 1, GROUP_W), F32)
    zeros_shift = jnp.zeros((bp, PD_W), F32)
    zeros_wkv = jnp.zeros((bp, N_HEADS, HEAD_DIM, HEAD_DIM), F32)
    outs_p, outs_s, memk_l, memv_l = [], [], [], []
    for l in range(depth):
        lw = _layer_weights({k_: v_[l] for k_, v_ in params.items()})

        mk, mv = norm_proj(mem_flat, lw['mem_norm'], lw['wkv'], lw['xk_norm'], X_HEADS, 1.0, F32)
        memk_l.append(mk.reshape(bp, n_mem, X_HEADS, X_HEAD_DIM))
        memv_l.append(mv.reshape(bp, n_mem, X_HEADS, X_HEAD_DIM))

        def attend_p(q, k, v, ckv, kpe):
            return mla_attn_prompt(q, k, v, bp, tp)

        res = _trunk_layer(y_p, bp, tp, lw, tabs_p, tp // tm_p, attend_p, mk.reshape(bp, n_mem, 512),
                           mv.reshape(bp, n_mem, 512), zeros_conv, zeros_shift, zeros_wkv, _tile(bp, SCAN_NB_PROMPT))
        y_p = res[0]
        outs_p.append(res[1:])

        def attend_s(q, k, v, ckv, kpe, l=l, lw=lw):
            qf = q.astype(F32).reshape(bs, ts, N_HEADS, LANES)
            qn = (qf[..., 64:] * lw['gk_nope']).transpose(0, 2, 1, 3)
            qbd = jnp.einsum('bhtj,hg->bhtgj', qn, jnp.eye(N_HEADS, dtype=F32)).reshape(bs, N_HEADS * ts, 256)
            qpe = qf[..., :QK_ROPE].transpose(0, 2, 1, 3).reshape(bs, N_HEADS * ts, QK_ROPE)
            pad = lambda a: jnp.pad(a.reshape(bs, ts, -1), ((0, 0), (0, PAGE - ts), (0, 0)))
            o = mla_attn_sample(page_table, qbd.astype(BF16), qpe.astype(BF16), pad(ckv),
                                pad(kpe).transpose(0, 2, 1), lw['wuk'], lw['wuk'].T, lw['wuv'], cache_ckv,
                                cache_kpe_t, l)
            return o.reshape(bs * ts, 256)

        res = _trunk_layer(y_s, bs, ts, lw, tabs_s, 1, attend_s, cache_mem_k[l].reshape(bs, n_mem, 512),
                           cache_mem_v[l].reshape(bs, n_mem, 512), state_conv[l], state_shift[l], state_wkv[l],
                           _tile(bs, SCAN_NB_SAMPLE))
        y_s = res[0]
        outs_s.append(res[1:])

    n_pp = tp // PAGE
    stack = lambda outs, i, ax: jnp.stack([o[i] for o in outs], axis=ax)
    ckv_prompt = stack(outs_p, 0, 0).reshape(depth, bp, n_pp, PAGE, KV_LORA).transpose(1, 2, 0, 3, 4)
    kpe_prompt = stack(outs_p, 1, 0).reshape(depth, bp, n_pp, PAGE, QK_ROPE).transpose(1, 2, 0, 3, 4)
    ckv_sample = stack(outs_s, 0, 0).reshape(depth, bs, ts, KV_LORA).transpose(1, 0, 2, 3)
    kpe_sample = stack(outs_s, 1, 0).reshape(depth, bs, ts, QK_ROPE).transpose(1, 0, 2, 3)
    return (y_p.reshape(bp, tp, d), y_s.reshape(bs, ts, d), ckv_prompt, kpe_prompt, ckv_sample, kpe_sample,
            jnp.stack(memk_l, 0), jnp.stack(memv_l, 0),
            stack(outs_p, 2, 0), stack(outs_s, 2, 0), stack(outs_p, 3, 0), stack(outs_s, 3, 0),
            stack(outs_p, 4, 0), stack(outs_s, 4, 0),
            stack(outs_s, 5, 0).reshape(depth, bs, ts, GROUP_W))
```

```python
import functools

import numpy as np
import jax
import jax.numpy as jnp
from jax import lax
from jax.experimental import pallas as pl
from jax.experimental.pallas import tpu as pltpu

F32 = jnp.float32
BF16 = jnp.bfloat16

EPS = 1e-6
LN_EPS = 1e-5
RW_LN_EPS = 64e-5
NEG = -1e30
ROPE_BASE = 10000.0

LANES = 128
SUBLANES = 8
VMEM_LIMIT_BYTES = 56 * 1024 * 1024

GROUP_W = 256
HEAD_DIM = 64
N_HEADS = 4
Q_LORA = 192
KV_LORA = 128
QK_ROPE = 32
QK_NOPE = 64
CONV_W = 31
CHUNK = 128
PAGE = 128
X_HEADS = 4
X_HEAD_DIM = 128
MLA_SCALE = (QK_NOPE + QK_ROPE) ** -0.5
PA_W = 512
PD_W = 1024
PAGES_PER_STEP = 32
SCAN_NB_PROMPT = 4
SCAN_NB_SAMPLE = 8


def _cparams(*sem):
    return pltpu.CompilerParams(dimension_semantics=sem, vmem_limit_bytes=VMEM_LIMIT_BYTES)


def _dot(a, b):
    return jnp.dot(a, b, preferred_element_type=F32)


def _dot_nt(a, b):
    return lax.dot_general(a, b, (((1,), (1,)), ((), ())), preferred_element_type=F32)


def _seg_sum(x, e):
    hi = x.astype(BF16)
    lo = (x - hi.astype(F32)).astype(BF16)
    return _dot(hi, e) + _dot(lo, e)


def _rms(x, width=None):
    w = x.shape[-1] if width is None else width
    return x * lax.rsqrt(jnp.sum(x * x, axis=-1, keepdims=True) * (1.0 / w) + EPS)


def _sigmoid(x):
    return 1.0 / (1.0 + jnp.exp(-x))


def _tile(n, pref):
    t = min(n, pref)
    while n % t:
        t //= 2
    return t


def _full(shape):
    nd = len(shape)
    return pl.BlockSpec(shape, lambda *a: (0,) * nd)


def _proj_in_kernel(x_ref, g_ref, w_ref, pa_ref, pb_ref, pc_ref, pd_ref):
    hb = (_rms(x_ref[...]) * g_ref[...]).astype(BF16)
    pa_ref[...] = _dot(hb, w_ref[:, 0:512])
    pb_ref[...] = _dot(hb, w_ref[:, 512:1024])
    pc_ref[...] = _dot(hb, w_ref[:, 1024:1536])
    pd_ref[...] = _dot(hb, w_ref[:, 1536:2560])


def proj_in(x, g, w):
    n, d = x.shape
    tm = _tile(n, 512)
    row = lambda wd: pl.BlockSpec((tm, wd), lambda i: (i, 0))
    return pl.pallas_call(
        _proj_in_kernel,
        grid=(n // tm,),
        in_specs=[row(d), _full(g.shape), _full(w.shape)],
        out_specs=[row(512), row(512), row(512), row(1024)],
        out_shape=[jax.ShapeDtypeStruct((n, wd), F32) for wd in (512, 512, 512, 1024)],
        compiler_params=_cparams("parallel"),
        name="proj_in",
    )(x, g, w)


def _rope128(x, c, sa, sb):
    w = x.shape[-1]
    return x * c + pltpu.roll(x, w - 16, 1) * sa + pltpu.roll(x, 16, 1) * sb


def _mla_prep_kernel(pa_ref, c_ref, sa_ref, sb_ref, gqn_ref, gkv_ref, wuq_ref, eq_ref, gq_ref,
                     gkr_ref, wuk_ref, ek_ref, gk_ref, wuv_ref,
                     q_ref, k_ref, v_ref, ckv_ref, kpe_ref):
    pa = pa_ref[...]
    c, sa, sb = c_ref[...], sa_ref[...], sb_ref[...]
    c4 = jnp.concatenate([c] * 4, axis=1)
    sa4 = jnp.concatenate([sa] * 4, axis=1)
    sb4 = jnp.concatenate([sb] * 4, axis=1)
    cq = _rms(pa[:, 0:256], Q_LORA) * gqn_ref[...]
    q = _dot(cq.astype(BF16), wuq_ref[...])
    qn = q * lax.rsqrt(_seg_sum(q * q, eq_ref[...]) + EPS) * gq_ref[...]
    q_ref[...] = (_rope128(qn, c4, sa4, sb4) * MLA_SCALE).astype(BF16)
    ckv = _rms(pa[:, 256:384]) * gkv_ref[...]
    ckv_ref[...] = ckv
    kp = _rms(pa[:, 384:512], QK_ROPE) * gkr_ref[...]
    kr = _rope128(kp, c, sa, sb)
    kpe_ref[...] = kr[:, 0:QK_ROPE]
    ckv_b = ckv.astype(BF16)
    kn = _dot(ckv_b, wuk_ref[...])
    kn = kn * lax.rsqrt(_seg_sum(kn * kn, ek_ref[...]) + EPS) * gk_ref[...]
    k_ref[...] = (kn + jnp.concatenate([kr] * 4, axis=1)).astype(BF16)
    v_ref[...] = _dot(ckv_b, wuv_ref[...]).astype(BF16)


def mla_prep(pa, tabs, wts, n_tab_blocks):
    n = pa.shape[0]
    c, sa, sb = tabs
    tm = c.shape[0] // n_tab_blocks
    assert n % tm == 0
    row = lambda wd: pl.BlockSpec((tm, wd), lambda i: (i, 0))
    tab = pl.BlockSpec((tm, LANES), lambda i: (i % n_tab_blocks, 0))
    return pl.pallas_call(
        _mla_prep_kernel,
        grid=(n // tm,),
        in_specs=[row(PA_W), tab, tab, tab] + [_full(w.shape) for w in wts],
        out_specs=[row(512), row(512), row(256), row(KV_LORA), row(QK_ROPE)],
        out_shape=[jax.ShapeDtypeStruct((n, 512), BF16), jax.ShapeDtypeStruct((n, 512), BF16),
                   jax.ShapeDtypeStruct((n, 256), BF16), jax.ShapeDtypeStruct((n, KV_LORA), F32),
                   jax.ShapeDtypeStruct((n, QK_ROPE), F32)],
        compiler_params=_cparams("parallel"),
        name="mla_prep",
    )(pa, c, sa, sb, *wts)


def _mla_attn_prompt_kernel(q_ref, k_ref, v_ref, o_ref, *, tq):
    i = pl.program_id(1)
    outs = []
    for h0 in range(0, N_HEADS, 2):
        heads = (h0, h0 + 1)

        def blk(j, carry, masked, heads=heads):
            off = pl.multiple_of(j * tq, tq)
            new = []
            for n, h in enumerate(heads):
                m, l, acc = carry[3 * n:3 * n + 3]
                kb = k_ref[pl.ds(off, tq), h * LANES:(h + 1) * LANES]
                vb = v_ref[pl.ds(off, tq), h * HEAD_DIM:(h + 1) * HEAD_DIM]
                s = _dot_nt(q_ref[:, h * LANES:(h + 1) * LANES], kb)
                if masked:
                    r = lax.broadcasted_iota(jnp.int32, (tq, tq), 0)
                    cc = lax.broadcasted_iota(jnp.int32, (tq, tq), 1)
                    s = jnp.where(cc <= r, s, NEG)
                m_new = jnp.maximum(m, jnp.max(s, axis=-1, keepdims=True))
                corr = jnp.exp(m - m_new)
                p = jnp.exp(s - m_new)
                l = l * corr + jnp.sum(p, axis=-1, keepdims=True)
                acc = acc * corr + _dot(p.astype(BF16), vb)
                new += [m_new, l, acc]
            return tuple(new)

        init = (jnp.full((tq, 1), NEG, F32), jnp.zeros((tq, 1), F32), jnp.zeros((tq, HEAD_DIM), F32)) * 2
        carry = lax.fori_loop(0, i, functools.partial(blk, masked=False), init)
        carry = blk(i, carry, True)
        outs += [carry[2] / carry[1], carry[5] / carry[4]]
    o_ref[...] = jnp.concatenate(outs, axis=1)


def mla_attn_prompt(q, k, v, b, t):
    tq = _tile(t, 512)
    nq = t // tq
    return pl.pallas_call(
        functools.partial(_mla_attn_prompt_kernel, tq=tq),
        grid=(b, nq),
        in_specs=[pl.BlockSpec((tq, 512), lambda bi, i: (bi * nq + i, 0)),
                  pl.BlockSpec((t, 512), lambda bi, i: (bi, 0)),
                  pl.BlockSpec((t, 256), lambda bi, i: (bi, 0))],
        out_specs=pl.BlockSpec((tq, 256), lambda bi, i: (bi * nq + i, 0)),
        out_shape=jax.ShapeDtypeStruct((b * t, 256), F32),
        compiler_params=_cparams("parallel", "arbitrary"),
        name="mla_attn_prompt",
    )(q, k, v)


N_SUB = 8


def _mla_attn_sample_kernel(pt_ref, qbd_ref, qpe_ref, ckvn_ref, kpen_ref, wuk_ref, wukt_ref, wuv_ref, ckv_hbm,
                            kpe_hbm, o_ref, ckv_buf, kpe_buf, sem, m_scr, l_scr, acc_scr, lhs_scr, *, n_pg, tq,
                            layer):
    b_id = pl.program_id(0)
    p_id = pl.program_id(1)
    n_steps = pl.num_programs(1)
    step = b_id * n_steps + p_id
    slot = step % 2

    def page_copies(b, p, sl):
        copies = []
        for k in range(n_pg):
            phys = pt_ref[b, p * n_pg + k]
            copies.append(pltpu.make_async_copy(ckv_hbm.at[phys, layer], ckv_buf.at[sl, k], sem.at[sl, 0]))
            copies.append(pltpu.make_async_copy(kpe_hbm.at[phys, layer], kpe_buf.at[sl, k], sem.at[sl, 1]))
        return copies

    @pl.when(step == 0)
    def _():
        for c in page_copies(b_id, p_id, slot):
            c.start()

    @pl.when(step + 1 < pl.num_programs(0) * n_steps)
    def _():
        wrap = p_id + 1 == n_steps
        for c in page_copies(jnp.where(wrap, b_id + 1, b_id), jnp.where(wrap, 0, p_id + 1), 1 - slot):
            c.start()

    qpe = qpe_ref[0]
    nr = N_HEADS * tq

    @pl.when(p_id == 0)
    def _():
        lhs_scr[0:GROUP_W, :] = wukt_ref[...]
        lhs_scr[GROUP_W:GROUP_W + nr, :] = _dot_nt(qbd_ref[0], wuk_ref[...]).astype(BF16)

    def scores(ckv, kpe_t):
        ckv_b = ckv.astype(BF16)
        out = _dot_nt(lhs_scr[...], ckv_b)
        rinv = []
        for h in range(N_HEADS):
            kn = out[h * HEAD_DIM:(h + 1) * HEAD_DIM]
            ss = jnp.sum(kn * kn, axis=0, keepdims=True) * (1.0 / HEAD_DIM)
            rinv.append(jnp.broadcast_to(lax.rsqrt(ss + EPS), (tq, ss.shape[1])))
        s = out[GROUP_W:GROUP_W + nr] * jnp.concatenate(rinv, axis=0) + _dot(qpe, kpe_t.astype(BF16))
        return s, ckv_b

    @pl.when(p_id == 0)
    def _():
        s, ckv_b = scores(ckvn_ref[0], kpen_ref[0])
        r = lax.broadcasted_iota(jnp.int32, s.shape, 0) % tq
        cc = lax.broadcasted_iota(jnp.int32, s.shape, 1)
        s = jnp.where(cc <= r, s, NEG)
        m = jnp.max(s, axis=-1, keepdims=True)
        p = jnp.exp(s - m)
        m_scr[...] = m
        l_scr[...] = jnp.sum(p, axis=-1, keepdims=True)
        acc_scr[...] = _dot(p.astype(BF16), ckv_b)

    for c in page_copies(b_id, p_id, slot):
        c.wait()

    n_sub = min(N_SUB, n_pg)
    per = n_pg // n_sub
    s_parts, ckv_parts = [], []
    for g in range(n_sub):
        ckv = jnp.concatenate([ckv_buf[slot, g * per + k] for k in range(per)], axis=0)
        kpe_t = jnp.concatenate([kpe_buf[slot, g * per + k] for k in range(per)], axis=1)
        s, ckv_b = scores(ckv, kpe_t)
        s_parts.append(s)
        ckv_parts.append(ckv_b)
    m = m_scr[...]
    m_new = m
    for s in s_parts:
        m_new = jnp.maximum(m_new, jnp.max(s, axis=-1, keepdims=True))
    corr = jnp.exp(m - m_new)
    l = l_scr[...] * corr
    acc = acc_scr[...] * corr
    for s, ckv_b in zip(s_parts, ckv_parts):
        p = jnp.exp(s - m_new)
        l = l + jnp.sum(p, axis=-1, keepdims=True)
        acc = acc + _dot(p.astype(BF16), ckv_b)
    m_scr[...] = m_new
    l_scr[...] = l
    acc_scr[...] = acc

    @pl.when(p_id == n_steps - 1)
    def _():
        lat = (acc / l).astype(BF16)
        full = _dot(lat, wuv_ref[...])
        lane_head = lax.broadcasted_iota(jnp.int32, (tq, 256), 1) // HEAD_DIM
        out = jnp.zeros((tq, 256), F32)
        for h in range(N_HEADS):
            out = jnp.where(lane_head == h, full[h * tq:(h + 1) * tq, :], out)
        o_ref[0] = out


def mla_attn_sample(page_table, qbd, qpe, ckv_new, kpe_new_t, wuk, wukt, wuv, cache_ckv, cache_kpe_t, layer):
    bs, n_pages = page_table.shape
    tq = qbd.shape[1] // N_HEADS
    n_pg = min(PAGES_PER_STEP, n_pages)
    assert n_pages % n_pg == 0
    nr = N_HEADS * tq
    per_b = lambda shp: pl.BlockSpec((1,) + shp, lambda b, p, pt: (b, 0, 0))
    cst = lambda shp: pl.BlockSpec(shp, lambda b, p, pt: (0,) * len(shp))
    hbm = pl.BlockSpec(memory_space=pl.ANY)
    grid_spec = pltpu.PrefetchScalarGridSpec(
        num_scalar_prefetch=1,
        grid=(bs, n_pages // n_pg),
        in_specs=[per_b((nr, 256)), per_b((nr, QK_ROPE)), per_b((PAGE, KV_LORA)), per_b((QK_ROPE, PAGE)),
                  cst(wuk.shape), cst(wukt.shape), cst(wuv.shape), hbm, hbm],
        out_specs=pl.BlockSpec((1, tq, 256), lambda b, p, pt: (b, 0, 0)),
        scratch_shapes=[pltpu.VMEM((2, n_pg, PAGE, KV_LORA), F32), pltpu.VMEM((2, n_pg, QK_ROPE, PAGE), F32),
                        pltpu.SemaphoreType.DMA((2, 2)),
                        pltpu.VMEM((nr, 1), F32), pltpu.VMEM((nr, 1), F32), pltpu.VMEM((nr, KV_LORA), F32),
                        pltpu.VMEM((GROUP_W + nr, KV_LORA), BF16)],
    )
    return pl.pallas_call(
        functools.partial(_mla_attn_sample_kernel, n_pg=n_pg, tq=tq, layer=layer),
        grid_spec=grid_spec,
        out_shape=jax.ShapeDtypeStruct((bs, tq, 256), F32),
        compiler_params=_cparams("arbitrary", "arbitrary"),
        name="mla_attn_sample",
    )(page_table, qbd, qpe, ckv_new, kpe_new_t, wuk, wukt, wuv, cache_ckv, cache_kpe_t)


CONV_HALO = 32


def _conv_kernel(pb_ref, st_ref, cw_ref, cb_ref, e_ref, g_ref, b_ref, pw_ref, o_ref, st_out_ref, xbuf, *, tt):
    j = pl.program_id(1)
    lo = CONV_HALO - (CONV_W - 1)

    @pl.when(j == 0)
    def _():
        xbuf[pl.ds(lo, CONV_W - 1), :] = st_ref[0]

    pb = pb_ref[...]
    xbuf[pl.ds(CONV_HALO, tt), :] = pb[:, 0:256] * _sigmoid(pb[:, 256:512])
    y = jnp.zeros((tt, 256), F32) + cb_ref[...]
    for k in range(CONV_W):
        y = y + xbuf[pl.ds(lo + k, tt), :] * cw_ref[pl.ds(k, 1), :]
    new_state = xbuf[pl.ds(lo + tt, CONV_W - 1), :]
    xbuf[pl.ds(lo, CONV_W - 1), :] = new_state
    st_out_ref[0] = new_state
    e = e_ref[...]
    yc = y - _seg_sum(y, e)
    yn = yc * lax.rsqrt(_seg_sum(yc * yc, e) + LN_EPS) * g_ref[...] + b_ref[...]
    act = yn * _sigmoid(yn)
    o_ref[...] = _dot(act.astype(BF16), pw_ref[...])


def conv_module(pb, state, wts, b, t):
    tt = _tile(t, 256)
    nt = t // tt
    return pl.pallas_call(
        functools.partial(_conv_kernel, tt=tt),
        grid=(b, nt),
        in_specs=[pl.BlockSpec((tt, 512), lambda bi, j: (bi * nt + j, 0)),
                  pl.BlockSpec((1, CONV_W - 1, 256), lambda bi, j: (bi, 0, 0))]
                 + [_full(w.shape) for w in wts],
        out_specs=[pl.BlockSpec((tt, 256), lambda bi, j: (bi * nt + j, 0)),
                   pl.BlockSpec((1, CONV_W - 1, 256), lambda bi, j: (bi, 0, 0))],
        out_shape=[jax.ShapeDtypeStruct((b * t, 256), F32), jax.ShapeDtypeStruct((b, CONV_W - 1, 256), F32)],
        scratch_shapes=[pltpu.VMEM((CONV_HALO + tt, 256), F32)],
        compiler_params=_cparams("parallel", "arbitrary"),
        name="conv_module",
    )(pb, state, *wts)


def _sgu_kernel(pc_ref, g_ref, b_ref, w_ref, bias_ref, o_ref, v_ref, *, n_chunks):
    x = pc_ref[...]
    z = 0.5 * x * (1.0 + jnp.tanh(0.7978845608028654 * (x + 0.044715 * (x * x * x))))
    u = z[:, 0:256]
    v = z[:, 256:512]
    vc = v - jnp.mean(v, axis=-1, keepdims=True)
    v = vc * lax.rsqrt(jnp.mean(vc * vc, axis=-1, keepdims=True) + LN_EPS) * g_ref[...] + b_ref[...]
    v_ref[...] = v
    lane_head = lax.broadcasted_iota(jnp.int32, (CHUNK, 256), 1) // HEAD_DIM
    for c in range(n_chunks):
        vcb = v[c * CHUNK:(c + 1) * CHUNK, :]
        sv = bias_ref[...]
        for h in range(N_HEADS):
            sv = sv + _dot(w_ref[h], jnp.where(lane_head == h, vcb, 0.0).astype(BF16))
        o_ref[pl.ds(c * CHUNK, CHUNK), :] = u[c * CHUNK:(c + 1) * CHUNK, :] * sv


def sgu(pc, wts):
    n = pc.shape[0]
    tm = _tile(n, 512)
    row = lambda wd: pl.BlockSpec((tm, wd), lambda i: (i, 0))
    return pl.pallas_call(
        functools.partial(_sgu_kernel, n_chunks=tm // CHUNK),
        grid=(n // tm,),
        in_specs=[row(512)] + [_full(w.shape) for w in wts],
        out_specs=[row(256), row(256)],
        out_shape=[jax.ShapeDtypeStruct((n, 256), F32), jax.ShapeDtypeStruct((n, 256), F32)],
        compiler_params=_cparams("parallel"),
        name="sgu",
    )(pc, *wts)


RW_HALO = 8


def _rwkv_prep_kernel(pd_ref, sh_ref, mu_ref, w0_ref, w2_ref, a0_ref, a2_ref, g2_ref, kkp_ref, ka_ref, rk_ref,
                      e_ref, r_ref, w_ref, k_ref, v_ref, kk_ref, kka_ref, g_ref, bv_ref, xbuf, *, tt):
    j = pl.program_id(1)

    @pl.when(j == 0)
    def _():
        xbuf[pl.ds(RW_HALO - 1, 1), :] = sh_ref[0]

    pd = pd_ref[...]
    xbuf[pl.ds(RW_HALO, tt), :] = pd
    prev = xbuf[pl.ds(RW_HALO - 1, tt), :]
    xbuf[pl.ds(RW_HALO - 1, 1), :] = pd[tt - 1:tt, :]
    xs = pd + (prev - pd) * mu_ref[...]
    r = xs[:, 0:256]
    k = xs[:, 256:512]
    v = xs[:, 512:768]
    xwa = xs[:, 768:896]
    xg = xs[:, 896:1024]
    z = -(w0_ref[...] + _dot(jnp.tanh(xwa).astype(BF16), w2_ref[...]))
    softplus = jnp.maximum(z, 0.0) + jnp.log(1.0 + jnp.exp(-jnp.abs(z)))
    w_ref[...] = jnp.exp(-jnp.exp(-softplus - 0.5))
    a = _sigmoid(a0_ref[...] + _dot(xwa.astype(BF16), a2_ref[...]))
    g_ref[...] = _dot(_sigmoid(xg).astype(BF16), g2_ref[...])
    e = e_ref[...]
    kk = k * kkp_ref[...]
    kk = kk * lax.rsqrt(_seg_sum(kk * kk, e) + 1e-12)
    k2 = k * (1.0 + (a - 1.0) * ka_ref[...])
    r_ref[...] = r
    k_ref[...] = k2
    v_ref[...] = v
    kk_ref[...] = kk
    kka_ref[...] = kk * a
    bv_ref[...] = _seg_sum(r * k2 * rk_ref[...], e) * v


def rwkv_prep(pd, shift, wts, b, t):
    tt = _tile(t, 256)
    nt = t // tt
    row = lambda wd: pl.BlockSpec((tt, wd), lambda bi, j: (bi * nt + j, 0))
    return pl.pallas_call(
        functools.partial(_rwkv_prep_kernel, tt=tt),
        grid=(b, nt),
        in_specs=[row(PD_W), pl.BlockSpec((1, 1, PD_W), lambda bi, j: (bi, 0, 0))]
                 + [_full(w.shape) for w in wts],
        out_specs=[row(256)] * 8,
        out_shape=[jax.ShapeDtypeStruct((b * t, 256), F32)] * 8,
        scratch_shapes=[pltpu.VMEM((RW_HALO + tt, PD_W), F32)],
        compiler_params=_cparams("parallel", "arbitrary"),
        name="rwkv_prep",
    )(pd, shift, *wts)


RW_BLOCK = 128


def _pack_bf16_pair(a, b):
    ua = lax.bitcast_convert_type(a.astype(BF16).astype(F32), jnp.uint32)
    ub = lax.bitcast_convert_type(b.astype(BF16).astype(F32), jnp.uint32)
    return lax.bitcast_convert_type(ua | (ub >> 16), jnp.int32)


def _unpack_bf16_pair(word):
    u = lax.bitcast_convert_type(word, jnp.uint32)
    return (lax.bitcast_convert_type(u & jnp.uint32(0xFFFF0000), F32),
            lax.bitcast_convert_type(u << 16, F32))


def _rwkv_scan_kernel(w_ref, kk_ref, kka_ref, k_ref, r_ref, v_ref, s0_ref, e_ref, y_ref, sf_ref, s_scr, col_scr, *,
                      nb, sblk):
    c = pl.program_id(1)

    @pl.when(c == 0)
    def _():
        s_scr[...] = s0_ref[...]

    lane = lax.broadcasted_iota(jnp.int32, (HEAD_DIM, LANES), 1)
    low = lane < HEAD_DIM
    n_half = 2 if sblk > HEAD_DIM else 1
    for b in range(nb):
        for p in range(2):
            blk = lambda ref: ref[b, :, p * LANES:(p + 1) * LANES]
            x = _pack_bf16_pair(blk(kk_ref), blk(kka_ref))
            if sblk < LANES:
                x = jnp.concatenate([x, jnp.zeros((LANES - sblk, LANES), jnp.int32)], axis=0)
            xt = x.T
            h0, h1 = xt[0:HEAD_DIM], xt[HEAD_DIM:LANES]
            col_scr[b, p, 0] = jnp.where(low, h0, pltpu.roll(h1, HEAD_DIM, 1))
            if n_half == 2:
                col_scr[b, p, 1] = jnp.where(low, pltpu.roll(h0, HEAD_DIM, 1), h1)

    base = jnp.where(low, 0, HEAD_DIM)
    diag = (lax.broadcasted_iota(jnp.int32, (HEAD_DIM, 256), 1) % HEAD_DIM
            == lax.broadcasted_iota(jnp.int32, (HEAD_DIM, 256), 0))
    for half in range(n_half):
        n_groups = min(sblk - half * HEAD_DIM, HEAD_DIM) // SUBLANES

        def group(gi, carry, half=half):
            row0 = pl.multiple_of(half * HEAD_DIM + gi * SUBLANES, SUBLANES)
            for b in range(nb):
                v8 = v_ref[b, pl.ds(row0, SUBLANES), :]
                w8 = w_ref[b, pl.ds(row0, SUBLANES), :]
                w8_hi = w8.astype(BF16).astype(F32)
                rows8 = (k_ref[b, pl.ds(row0, SUBLANES), :], r_ref[b, pl.ds(row0, SUBLANES), :], w8_hi, w8 - w8_hi)
                on_diag = [jnp.where(diag, jnp.broadcast_to(x8[i:i + 1, :], (HEAD_DIM, 256)), 0.0).astype(BF16)
                           for i in range(SUBLANES) for x8 in rows8]
                spread = _dot(jnp.concatenate(on_diag, axis=0), e_ref[...])
                ys = []
                for p in range(2):
                    st = s_scr[b, :, p * LANES:(p + 1) * LANES]
                    yp = []
                    for i in range(SUBLANES):
                        idx = base + (gi * SUBLANES + i)
                        kk, kka = _unpack_bf16_pair(jnp.take_along_axis(col_scr[b, p, half], idx, axis=1))
                        part = lambda n: spread[(4 * i + n) * HEAD_DIM:(4 * i + n + 1) * HEAD_DIM,
                                                p * LANES:(p + 1) * LANES]
                        k, r, w = part(0), part(1), part(2) + part(3)
                        vrow = v8[i:i + 1, p * LANES:(p + 1) * LANES]
                        sa = -jnp.sum(st * kk, axis=0, keepdims=True)
                        st = st * w + kka * sa + k * vrow
                        yp.append(jnp.sum(st * r, axis=0, keepdims=True))
                    s_scr[b, :, p * LANES:(p + 1) * LANES] = st
                    ys.append(jnp.concatenate(yp, axis=0))
                y_ref[b, pl.ds(row0, SUBLANES), :] = jnp.concatenate(ys, axis=1)
            return carry

        lax.fori_loop(0, n_groups, group, 0)

    @pl.when(c == pl.num_programs(1) - 1)
    def _():
        sf_ref[...] = s_scr[...]


def rwkv_scan(seqs, v, s0, nb):
    b, t, _ = v.shape
    e_heads = _head_matrix(1.0)
    sblk = min(t, RW_BLOCK)
    assert b % nb == 0 and t % sblk == 0 and sblk % SUBLANES == 0 and (sblk <= HEAD_DIM or sblk == RW_BLOCK)
    st_spec = pl.BlockSpec((nb, HEAD_DIM, 256), lambda bi, c: (bi, 0, 0))
    seq_spec = pl.BlockSpec((nb, sblk, 256), lambda bi, c: (bi, c, 0))
    return pl.pallas_call(
        functools.partial(_rwkv_scan_kernel, nb=nb, sblk=sblk),
        grid=(b // nb, t // sblk),
        in_specs=[seq_spec] * 6 + [st_spec, _full(e_heads.shape)],
        out_specs=[seq_spec, st_spec],
        out_shape=[jax.ShapeDtypeStruct((b, t, 256), F32), jax.ShapeDtypeStruct((b, HEAD_DIM, 256), F32)],
        scratch_shapes=[pltpu.VMEM((nb, HEAD_DIM, 256), F32),
                        pltpu.VMEM((nb, 2, 2, HEAD_DIM, LANES), jnp.int32)],
        compiler_params=_cparams("parallel", "arbitrary"),
        name="rwkv_scan",
    )(*seqs, v, s0, e_heads)


def _mix_out_kernel(x_ref, oa_ref, ob_ref, oc_ref, y_ref, bv_ref, g_ref, e_ref, lg_ref, lb_ref, on_ref, w_ref,
                    o_ref):
    e = e_ref[...]
    y = y_ref[...]
    yc = y - _seg_sum(y, e)
    yn = yc * lax.rsqrt(_seg_sum(yc * yc, e) + RW_LN_EPS) * lg_ref[...] + lb_ref[...]
    od = (yn + bv_ref[...]) * g_ref[...]
    acc = x_ref[...]
    for gi, o in enumerate((oa_ref[...], ob_ref[...], oc_ref[...], od)):
        on = (_rms(o) * on_ref[:, gi * 256:(gi + 1) * 256]).astype(BF16)
        acc = acc + _dot(on, w_ref[pl.ds(gi * 256, 256), :])
    o_ref[...] = acc


def mix_out(x, oa, ob, oc, y, bv, g, wts):
    n, d = x.shape
    tm = _tile(n, 512)
    row = lambda wd: pl.BlockSpec((tm, wd), lambda i: (i, 0))
    return pl.pallas_call(
        _mix_out_kernel,
        grid=(n // tm,),
        in_specs=[row(d)] + [row(256)] * 6 + [_full(w.shape) for w in wts],
        out_specs=row(d),
        out_shape=jax.ShapeDtypeStruct((n, d), F32),
        compiler_params=_cparams("parallel"),
        name="mix_out",
    )(x, oa, ob, oc, y, bv, g, *wts)


def _norm_proj_kernel(x_ref, g_ref, w_ref, hg_ref, *o_refs, n_norm_heads, scale):
    hb = (_rms(x_ref[...]) * g_ref[...]).astype(BF16)
    y = _dot(hb, w_ref[...])
    for i, o_ref in enumerate(o_refs):
        parts = []
        for h in range(X_HEADS):
            blk = y[:, (i * X_HEADS + h) * X_HEAD_DIM:(i * X_HEADS + h + 1) * X_HEAD_DIM]
            if i * X_HEADS + h < n_norm_heads:
                blk = _rms(blk) * hg_ref[...] * scale
            parts.append(blk)
        o_ref[...] = jnp.concatenate(parts, axis=1).astype(o_ref.dtype)


def norm_proj(x, g, w, hg, n_norm_heads, scale, out_dtype):
    n, d = x.shape
    n_out = w.shape[1] // 512
    tm = _tile(n, 512)
    row = lambda wd: pl.BlockSpec((tm, wd), lambda i: (i, 0))
    return pl.pallas_call(
        functools.partial(_norm_proj_kernel, n_norm_heads=n_norm_heads, scale=scale),
        grid=(n // tm,),
        in_specs=[row(d), _full(g.shape), _full(w.shape), _full(hg.shape)],
        out_specs=[row(512)] * n_out,
        out_shape=[jax.ShapeDtypeStruct((n, 512), out_dtype)] * n_out,
        compiler_params=_cparams("parallel"),
        name="norm_proj",
    )(x, g, w, hg)


XATTN_SHORT_ROWS = 64


def _xattn_kernel(x_ref, q_ref, k_ref, v_ref, wo_ref, o_ref, *, n_seq, rows, n_mem):
    seq_outs = []
    for s in range(n_seq):
        qs = [q_ref[s * rows:(s + 1) * rows, h * X_HEAD_DIM:(h + 1) * X_HEAD_DIM].astype(BF16)
              for h in range(X_HEADS)]
        if n_seq == 1:
            outs = []
            for h in range(X_HEADS):
                head_rows = pl.ds(h, n_mem, stride=X_HEADS)
                sc = _dot_nt(qs[h], k_ref[s, head_rows, :].astype(BF16))
                p = jnp.exp(sc - jnp.max(sc, axis=-1, keepdims=True))
                p = p / jnp.sum(p, axis=-1, keepdims=True)
                outs.append(_dot(p.astype(BF16), v_ref[s, head_rows, :].astype(BF16)))
        else:
            sc = _dot_nt(jnp.concatenate(qs, axis=0), k_ref[s].astype(BF16))
            row_head = lax.broadcasted_iota(jnp.int32, sc.shape, 0) // rows
            col_head = lax.broadcasted_iota(jnp.int32, sc.shape, 1) % X_HEADS
            sc = jnp.where(row_head == col_head, sc, NEG)
            p = jnp.exp(sc - jnp.max(sc, axis=-1, keepdims=True))
            p = p / jnp.sum(p, axis=-1, keepdims=True)
            o = _dot(p.astype(BF16), v_ref[s].astype(BF16))
            outs = [o[h * rows:(h + 1) * rows] for h in range(X_HEADS)]
        seq_outs.append(jnp.concatenate(outs, axis=1))
    xo = jnp.concatenate(seq_outs, axis=0).astype(BF16)
    o_ref[...] = x_ref[...] + _dot(xo, wo_ref[...])


def xattn(x, q, mem_k, mem_v, wo, b, t, kv_seq0):
    n, d = x.shape
    n_mem = mem_k.shape[1] // X_HEADS
    if t >= XATTN_SHORT_ROWS:
        n_seq, rows = 1, _tile(t, 512)
    else:
        n_seq, rows = _tile(b, XATTN_SHORT_ROWS // t), t
    nq = t // rows
    assert kv_seq0 % n_seq == 0
    row = lambda wd: pl.BlockSpec((n_seq * rows, wd), lambda bi, i: (bi * nq + i, 0))
    kv = pl.BlockSpec((n_seq, n_mem * X_HEADS, X_HEAD_DIM), lambda bi, i: (kv_seq0 // n_seq + bi, 0, 0))
    return pl.pallas_call(
        functools.partial(_xattn_kernel, n_seq=n_seq, rows=rows, n_mem=n_mem),
        grid=(b // n_seq, nq),
        in_specs=[row(d), row(512), kv, kv, _full(wo.shape)],
        out_specs=row(d),
        out_shape=jax.ShapeDtypeStruct((n, d), F32),
        compiler_params=_cparams("parallel", "arbitrary"),
        name="xattn",
    )(x, q, mem_k, mem_v, wo)


def _mem_kv_kernel(x_ref, g_ref, w_ref, hg_ref, k_ref, v_ref, *, tm):
    hb = (_rms(x_ref[...]) * g_ref[...]).astype(BF16)
    y = _dot(hb, w_ref[...])
    for h in range(X_HEADS):
        head_rows = pl.ds(h, tm, stride=X_HEADS)
        k_ref[head_rows, :] = _rms(y[:, h * X_HEAD_DIM:(h + 1) * X_HEAD_DIM]) * hg_ref[...]
        v_ref[head_rows, :] = y[:, (X_HEADS + h) * X_HEAD_DIM:(X_HEADS + h + 1) * X_HEAD_DIM]


def mem_kv(x, g, w, hg):
    n, d = x.shape
    tm = _tile(n, 512)
    out = pl.BlockSpec((tm * X_HEADS, X_HEAD_DIM), lambda i: (i, 0))
    return pl.pallas_call(
        functools.partial(_mem_kv_kernel, tm=tm),
        grid=(n // tm,),
        in_specs=[pl.BlockSpec((tm, d), lambda i: (i, 0)), _full(g.shape), _full(w.shape), _full(hg.shape)],
        out_specs=[out, out],
        out_shape=[jax.ShapeDtypeStruct((n * X_HEADS, X_HEAD_DIM), F32)] * 2,
        compiler_params=_cparams("parallel"),
        name="mem_kv",
    )(x, g, w, hg)


FFN_CHUNK = 256


def _ffn_kernel(x_ref, g_ref, wg_ref, wu_ref, wo_ref, o_ref, *, n_chunks):
    x = x_ref[...]
    hb = (_rms(x) * g_ref[...]).astype(BF16)
    acc = x
    for c in range(n_chunks):
        a = _dot(hb, wg_ref[c])
        u = _dot(hb, wu_ref[c])
        acc = acc + _dot((a * _sigmoid(a) * u).astype(BF16), wo_ref[c])
    o_ref[...] = acc


def ffn(x, g, wg, wu, wo):
    n, d = x.shape
    tm = _tile(n, 512)
    row = pl.BlockSpec((tm, d), lambda i: (i, 0))
    return pl.pallas_call(
        functools.partial(_ffn_kernel, n_chunks=wg.shape[0]),
        grid=(n // tm,),
        in_specs=[row, _full(g.shape), _full(wg.shape), _full(wu.shape), _full(wo.shape)],
        out_specs=row,
        out_shape=jax.ShapeDtypeStruct((n, d), F32),
        compiler_params=_cparams("parallel"),
        name="ffn",
    )(x, g, wg, wu, wo)


def _seg_matrix(seg_ids, seg_len):
    s = np.asarray(seg_ids)
    m = (s[:, None] == s[None, :]) & (s[:, None] >= 0)
    return jnp.asarray(m.astype(np.float32) / np.asarray(seg_len, np.float32)[None, :], BF16)


def _mla_segments():
    lane = np.arange(512)
    blk, off = lane // 128, lane % 128
    q_ids = np.where(off < 32, 2 * blk, np.where(off < 64, -1, 2 * blk + 1))
    q_len = np.where(off < 32, 32.0, 64.0)
    k_ids = np.where(off < 64, -1, blk)
    k_len = np.full(512, 64.0)
    return _seg_matrix(q_ids, q_len), _seg_matrix(k_ids, k_len)


def _head_matrix(scale_len):
    lane = np.arange(256)
    return _seg_matrix(lane // HEAD_DIM, np.full(256, scale_len))


def _rope_tables(pos):
    half = QK_ROPE // 2
    inv = jnp.power(ROPE_BASE, -jnp.arange(half, dtype=F32) / half)
    ang = pos.astype(F32)[:, None] * inv[None, :]
    cos, sin = jnp.cos(ang), jnp.sin(ang)
    n = pos.shape[0]
    z = lambda w: jnp.zeros((n, w), F32)
    c = jnp.concatenate([cos, cos, jnp.ones((n, LANES - QK_ROPE), F32)], axis=1)
    sa = jnp.concatenate([-sin, z(LANES - half)], axis=1)
    sb = jnp.concatenate([z(half), sin, z(LANES - QK_ROPE)], axis=1)
    return c, sa, sb


def _layer_weights(p):
    row = lambda v: v.reshape(1, -1).astype(F32)
    zc = lambda a, w: jnp.zeros((a.shape[0], w), a.dtype)
    w_in = p['w_in']
    c1, c2, c3 = 352, 352 + 512, 352 + 1024
    pa = w_in[:, :c1]
    w_in_p = jnp.concatenate(
        [pa[:, :Q_LORA], zc(pa, 64), pa[:, Q_LORA:Q_LORA + KV_LORA], pa[:, Q_LORA + KV_LORA:], zc(pa, 96),
         w_in[:, c1:c2], w_in[:, c2:c3], w_in[:, c3:]], axis=1).astype(BF16)

    wuq = p['mla_w_uq'].reshape(Q_LORA, N_HEADS, QK_NOPE + QK_ROPE)
    wuq = jnp.concatenate([wuq[:, :, QK_NOPE:], jnp.zeros((Q_LORA, N_HEADS, 32), F32), wuq[:, :, :QK_NOPE]], axis=2)
    wuq = jnp.concatenate([wuq.reshape(Q_LORA, 512), jnp.zeros((256 - Q_LORA, 512), F32)], axis=0).astype(BF16)
    wuk = p['mla_w_uk'].reshape(KV_LORA, N_HEADS, QK_NOPE)
    wuk_p = jnp.concatenate([jnp.zeros((KV_LORA, N_HEADS, 64), F32), wuk], axis=2).reshape(KV_LORA, 512).astype(BF16)
    blk = lambda a, b_, c_: jnp.tile(jnp.concatenate([a, b_, c_]), N_HEADS).reshape(1, 512)
    z32, z64 = jnp.zeros((32,), F32), jnp.zeros((64,), F32)
    gq = blk(p['mla_gq_rope'], z32, p['mla_gq_nope'])
    gk = blk(z32, z32, p['mla_gk_nope'])
    gqn = jnp.concatenate([p['mla_q_norm'], z64]).reshape(1, 256)
    gkr = jnp.concatenate([p['mla_gk_rope'], jnp.zeros((96,), F32)]).reshape(1, LANES)
    eq, ek = _mla_segments()
    mla = (gqn, row(p['mla_kv_norm']), wuq, eq, gq, gkr, wuk_p, ek, gk, p['mla_w_uv'].astype(BF16))

    e64 = _head_matrix(64.0)
    conv = (p['conv_w'], row(p['conv_b']), e64, row(p['conv_norm_g']), row(p['conv_norm_b']),
            p['conv_pw'].astype(BF16))

    z64r = jnp.zeros((64, GROUP_W), F32)
    rw = (row(p['rw_mu']),
          row(p['rw_w0']), jnp.concatenate([p['rw_w2'], z64r], axis=0).astype(BF16),
          row(p['rw_a0']), jnp.concatenate([z64r, p['rw_a2']], axis=0).astype(BF16),
          p['rw_g2'].astype(BF16), row(p['rw_kk']), row(p['rw_ka']), row(p['rw_rk']), _head_matrix(1.0))

    mix = (e64, row(p['rw_ln_g']), row(p['rw_ln_b']), row(p['out_norm']), p['w_out'].astype(BF16))

    d_ff = p['w_ffn_out'].shape[0]
    nck = d_ff // FFN_CHUNK
    d = w_in.shape[0]
    wg = p['w_ffn_in'][:, :d_ff].reshape(d, nck, FFN_CHUNK).transpose(1, 0, 2).astype(BF16)
    wu = p['w_ffn_in'][:, d_ff:].reshape(d, nck, FFN_CHUNK).transpose(1, 0, 2).astype(BF16)
    wo = p['w_ffn_out'].reshape(nck, FFN_CHUNK, d).astype(BF16)

    return dict(
        norm_mix=row(p['norm_mix']), w_in=w_in_p, mla=mla, conv=conv, rw=rw, mix=mix,
        wuk=p['mla_w_uk'].astype(BF16), wuv=p['mla_w_uv'].astype(BF16), gk_nope=p['mla_gk_nope'],
        sgu_ln=(row(p['sgu_norm_g']), row(p['sgu_norm_b'])), sgu_w=p['sgu_w'], sgu_b=p['sgu_b'],
        norm_x=row(p['norm_x']), wq=p['wq_x'].astype(BF16), xq_norm=row(p['xq_norm']),
        mem_norm=row(p['mem_norm']), wkv=jnp.concatenate([p['wk_x'], p['wv_x']], axis=1).astype(BF16),
        xk_norm=row(p['xk_norm']), wo_x=p['wo_x'].astype(BF16),
        norm_ffn=row(p['norm_ffn']), wg=wg, wu=wu, wo=wo)


def _sgu_weights(lw, t):
    l = min(t, CHUNK)
    w = lw['sgu_w'][:, :l, :l] * jnp.tril(jnp.ones((l, l), F32))
    reps = CHUNK // l
    if reps > 1:
        w = jnp.einsum('ab,hij->haibj', jnp.eye(reps, dtype=F32), w).reshape(N_HEADS, CHUNK, CHUNK)
    bias = jnp.tile(lw['sgu_b'][:, :l].T, (reps, 1))
    bias = jnp.repeat(bias, HEAD_DIM, axis=1)
    return lw['sgu_ln'] + (w.astype(BF16), bias)


def _trunk_layer(x, b, t, lw, tabs, n_tab_blocks, attend, mem_k, mem_v, kv_seq0, conv_state, shift_state, wkv_state,
                 scan_nb):
    pa, pb, pc, pd = proj_in(x, lw['norm_mix'], lw['w_in'])
    q, k, v, ckv, kpe = mla_prep(pa, tabs, lw['mla'], n_tab_blocks)
    oa = attend(q, k, v, ckv, kpe)
    ob, conv_new = conv_module(pb, conv_state, lw['conv'], b, t)
    oc, v_sgu = sgu(pc, _sgu_weights(lw, t))
    r, w, k2, vv, kk, kka, g, bv = rwkv_prep(pd, shift_state.reshape(b, 1, PD_W), lw['rw'], b, t)
    seqs = [a.reshape(b, t, 256) for a in (w, kk, kka, k2, r)]
    s0 = wkv_state.transpose(0, 3, 1, 2).reshape(b, HEAD_DIM, 256)
    y, s_fin = rwkv_scan(seqs, vv.reshape(b, t, 256), s0, scan_nb)
    wkv_new = s_fin.reshape(b, HEAD_DIM, N_HEADS, HEAD_DIM).transpose(0, 2, 3, 1)
    shift_new = pd.reshape(b, t, PD_W)[:, -1]
    x = mix_out(x, oa, ob, oc, y.reshape(b * t, 256), bv, g, lw['mix'])
    q_dtype = BF16 if t % 16 == 0 else F32
    (qx,) = norm_proj(x, lw['norm_x'], lw['wq'], lw['xq_norm'], X_HEADS, X_HEAD_DIM ** -0.5, q_dtype)
    x = xattn(x, qx, mem_k, mem_v, lw['wo_x'], b, t, kv_seq0)
    x = ffn(x, lw['norm_ffn'], lw['wg'], lw['wu'], lw['wo'])
    return x, ckv, kpe, conv_new, shift_new, wkv_new, v_sgu


def kernel(x_prompt, x_sample, mem_prompt, cache_ckv, cache_kpe, cache_mem_k, cache_mem_v, state_conv, state_shift, state_wkv, page_table, norm_mix, w_in, mla_q_norm, mla_kv_norm, mla_w_uq, mla_w_uk, mla_w_uv, mla_gq_nope, mla_gq_rope, mla_gk_nope, mla_gk_rope, conv_w, conv_b, conv_norm_g, conv_norm_b, conv_pw, sgu_norm_g, sgu_norm_b, sgu_w, sgu_b, rw_mu, rw_w0, rw_w2, rw_a0, rw_a2, rw_g2, rw_kk, rw_ka, rw_rk, rw_ln_g, rw_ln_b, out_norm, w_out, norm_x, mem_norm, wq_x, wk_x, wv_x, xq_norm, xk_norm, wo_x, norm_ffn, w_ffn_in, w_ffn_out):
    params = dict(
        norm_mix=norm_mix, w_in=w_in, mla_q_norm=mla_q_norm, mla_kv_norm=mla_kv_norm, mla_w_uq=mla_w_uq,
        mla_w_uk=mla_w_uk, mla_w_uv=mla_w_uv, mla_gq_nope=mla_gq_nope, mla_gq_rope=mla_gq_rope,
        mla_gk_nope=mla_gk_nope, mla_gk_rope=mla_gk_rope, conv_w=conv_w, conv_b=conv_b, conv_norm_g=conv_norm_g,
        conv_norm_b=conv_norm_b, conv_pw=conv_pw, sgu_norm_g=sgu_norm_g, sgu_norm_b=sgu_norm_b, sgu_w=sgu_w,
        sgu_b=sgu_b, rw_mu=rw_mu, rw_w0=rw_w0, rw_w2=rw_w2, rw_a0=rw_a0, rw_a2=rw_a2, rw_g2=rw_g2, rw_kk=rw_kk,
        rw_ka=rw_ka, rw_rk=rw_rk, rw_ln_g=rw_ln_g, rw_ln_b=rw_ln_b, out_norm=out_norm, w_out=w_out, norm_x=norm_x,
        mem_norm=mem_norm, wq_x=wq_x, wk_x=wk_x, wv_x=wv_x, xq_norm=xq_norm, xk_norm=xk_norm, wo_x=wo_x,
        norm_ffn=norm_ffn, w_ffn_in=w_ffn_in, w_ffn_out=w_ffn_out)
    depth = w_in.shape[0]
    bp, tp, d = x_prompt.shape
    bs, ts, _ = x_sample.shape
    n_mem = mem_prompt.shape[1]
    n_pages = page_table.shape[1]
    past_len = n_pages * PAGE

    tm_p = _tile(tp, 512)
    tabs_p = _rope_tables(jnp.arange(tp, dtype=jnp.int32))
    tm_s = _tile(bs * ts, 512)
    tabs_s = tuple(jnp.tile(a, (tm_s // ts, 1)) for a in _rope_tables(past_len + jnp.arange(ts, dtype=jnp.int32)))
    cache_kpe_t = jnp.swapaxes(cache_kpe, 2, 3)
    cache_k_rows = cache_mem_k.reshape(depth * bs, n_mem * X_HEADS, X_HEAD_DIM)
    cache_v_rows = cache_mem_v.reshape(depth * bs, n_mem * X_HEADS, X_HEAD_DIM)
    y_p = x_prompt.reshape(bp * tp, d)
    y_s = x_sample.reshape(bs * ts, d)
    mem_flat = mem_prompt.reshape(bp * n_mem, d)
    zeros_conv = jnp.zeros((bp, CONV_W - 1, GROUP_W), F32)
    zeros_shift = jnp.zeros((bp, PD_W), F32)
    zeros_wkv = jnp.zeros((bp, N_HEADS, HEAD_DIM, HEAD_DIM), F32)
    outs_p, outs_s, memk_l, memv_l = [], [], [], []
    for l in range(depth):
        lw = _layer_weights({k_: v_[l] for k_, v_ in params.items()})

        mk, mv = mem_kv(mem_flat, lw['mem_norm'], lw['wkv'], lw['xk_norm'])
        memk_l.append(mk.reshape(bp, n_mem, X_HEADS, X_HEAD_DIM))
        memv_l.append(mv.reshape(bp, n_mem, X_HEADS, X_HEAD_DIM))

        def attend_p(q, k, v, ckv, kpe):
            return mla_attn_prompt(q, k, v, bp, tp)

        kv_rows = n_mem * X_HEADS
        res = _trunk_layer(y_p, bp, tp, lw, tabs_p, tp // tm_p, attend_p, mk.reshape(bp, kv_rows, X_HEAD_DIM),
                           mv.reshape(bp, kv_rows, X_HEAD_DIM), 0, zeros_conv, zeros_shift, zeros_wkv,
                           _tile(bp, SCAN_NB_PROMPT))
        y_p = res[0]
        outs_p.append(res[1:])

        def attend_s(q, k, v, ckv, kpe, l=l, lw=lw):
            qf = q.astype(F32).reshape(bs, ts, N_HEADS, LANES)
            qn = (qf[..., 64:] * lw['gk_nope']).transpose(0, 2, 1, 3)
            qbd = jnp.einsum('bhtj,hg->bhtgj', qn, jnp.eye(N_HEADS, dtype=F32)).reshape(bs, N_HEADS * ts, 256)
            qpe = qf[..., :QK_ROPE].transpose(0, 2, 1, 3).reshape(bs, N_HEADS * ts, QK_ROPE)
            pad = lambda a: jnp.pad(a.reshape(bs, ts, -1), ((0, 0), (0, PAGE - ts), (0, 0)))
            o = mla_attn_sample(page_table, qbd.astype(BF16), qpe.astype(BF16), pad(ckv),
                                pad(kpe).transpose(0, 2, 1), lw['wuk'], lw['wuk'].T, lw['wuv'], cache_ckv,
                                cache_kpe_t, l)
            return o.reshape(bs * ts, 256)

        res = _trunk_layer(y_s, bs, ts, lw, tabs_s, 1, attend_s, cache_k_rows, cache_v_rows, l * bs,
                           state_conv[l], state_shift[l], state_wkv[l], _tile(bs, SCAN_NB_SAMPLE))
        y_s = res[0]
        outs_s.append(res[1:])

    n_pp = tp // PAGE
    stack = lambda outs, i, ax: jnp.stack([o[i] for o in outs], axis=ax)
    ckv_prompt = stack(outs_p, 0, 0).reshape(depth, bp, n_pp, PAGE, KV_LORA).transpose(1, 2, 0, 3, 4)
    kpe_prompt = stack(outs_p, 1, 0).reshape(depth, bp, n_pp, PAGE, QK_ROPE).transpose(1, 2, 0, 3, 4)
    ckv_sample = stack(outs_s, 0, 0).reshape(depth, bs, ts, KV_LORA).transpose(1, 0, 2, 3)
    kpe_sample = stack(outs_s, 1, 0).reshape(depth, bs, ts, QK_ROPE).transpose(1, 0, 2, 3)
    return (y_p.reshape(bp, tp, d), y_s.reshape(bs, ts, d), ckv_prompt, kpe_prompt, ckv_sample, kpe_sample,
            jnp.stack(memk_l, 0), jnp.stack(memv_l, 0),
            stack(outs_p, 2, 0), stack(outs_s, 2, 0), stack(outs_p, 3, 0), stack(outs_s, 3, 0),
            stack(outs_p, 4, 0), stack(outs_s, 4, 0),
            stack(outs_s, 5, 0).reshape(depth, bs, ts, GROUP_W))
```

```python
import functools

import numpy as np
import jax
import jax.numpy as jnp
from jax import lax
from jax.experimental import pallas as pl
from jax.experimental.pallas import tpu as pltpu

F32 = jnp.float32
BF16 = jnp.bfloat16

EPS = 1e-6
LN_EPS = 1e-5
RW_LN_EPS = 64e-5
NEG = -1e30
ROPE_BASE = 10000.0

LANES = 128
SUBLANES = 8
VMEM_LIMIT_BYTES = 56 * 1024 * 1024

GROUP_W = 256
HEAD_DIM = 64
N_HEADS = 4
Q_LORA = 192
KV_LORA = 128
QK_ROPE = 32
QK_NOPE = 64
CONV_W = 31
CHUNK = 128
PAGE = 128
X_HEADS = 4
X_HEAD_DIM = 128
MLA_SCALE = (QK_NOPE + QK_ROPE) ** -0.5
PA_W = 512
PD_W = 1024
PAGES_PER_STEP = 32
SCAN_NB_PROMPT = 4
SCAN_NB_SAMPLE = 8


def _cparams(*sem):
    return pltpu.CompilerParams(dimension_semantics=sem, vmem_limit_bytes=VMEM_LIMIT_BYTES)


def _dot(a, b):
    return jnp.dot(a, b, preferred_element_type=F32)


def _dot_nt(a, b):
    return lax.dot_general(a, b, (((1,), (1,)), ((), ())), preferred_element_type=F32)


def _seg_sum(x, e):
    hi = x.astype(BF16)
    lo = (x - hi.astype(F32)).astype(BF16)
    return _dot(hi, e) + _dot(lo, e)


def _rms(x, width=None):
    w = x.shape[-1] if width is None else width
    return x * lax.rsqrt(jnp.sum(x * x, axis=-1, keepdims=True) * (1.0 / w) + EPS)


def _sigmoid(x):
    return 1.0 / (1.0 + jnp.exp(-x))


def _tile(n, pref):
    t = min(n, pref)
    while n % t:
        t //= 2
    return t


def _full(shape):
    nd = len(shape)
    return pl.BlockSpec(shape, lambda *a: (0,) * nd)


def _proj_in_kernel(x_ref, g_ref, w_ref, pa_ref, pb_ref, pc_ref, pd_ref):
    hb = (_rms(x_ref[...]) * g_ref[...]).astype(BF16)
    pa_ref[...] = _dot(hb, w_ref[:, 0:512])
    pb_ref[...] = _dot(hb, w_ref[:, 512:1024])
    pc_ref[...] = _dot(hb, w_ref[:, 1024:1536])
    pd_ref[...] = _dot(hb, w_ref[:, 1536:2560])


def proj_in(x, g, w):
    n, d = x.shape
    tm = _tile(n, 512)
    row = lambda wd: pl.BlockSpec((tm, wd), lambda i: (i, 0))
    return pl.pallas_call(
        _proj_in_kernel,
        grid=(n // tm,),
        in_specs=[row(d), _full(g.shape), _full(w.shape)],
        out_specs=[row(512), row(512), row(512), row(1024)],
        out_shape=[jax.ShapeDtypeStruct((n, wd), F32) for wd in (512, 512, 512, 1024)],
        compiler_params=_cparams("parallel"),
        name="proj_in",
    )(x, g, w)


def _rope128(x, c, sa, sb):
    w = x.shape[-1]
    return x * c + pltpu.roll(x, w - 16, 1) * sa + pltpu.roll(x, 16, 1) * sb


def _mla_prep_kernel(pa_ref, c_ref, sa_ref, sb_ref, gqn_ref, gkv_ref, wuq_ref, eq_ref, gq_ref,
                     gkr_ref, wuk_ref, ek_ref, gk_ref, wuv_ref,
                     q_ref, k_ref, v_ref, ckv_ref, kpe_ref):
    pa = pa_ref[...]
    c, sa, sb = c_ref[...], sa_ref[...], sb_ref[...]
    c4 = jnp.concatenate([c] * 4, axis=1)
    sa4 = jnp.concatenate([sa] * 4, axis=1)
    sb4 = jnp.concatenate([sb] * 4, axis=1)
    cq = _rms(pa[:, 0:256], Q_LORA) * gqn_ref[...]
    q = _dot(cq.astype(BF16), wuq_ref[...])
    qn = q * lax.rsqrt(_seg_sum(q * q, eq_ref[...]) + EPS) * gq_ref[...]
    q_ref[...] = (_rope128(qn, c4, sa4, sb4) * MLA_SCALE).astype(BF16)
    ckv = _rms(pa[:, 256:384]) * gkv_ref[...]
    ckv_ref[...] = ckv
    kp = _rms(pa[:, 384:512], QK_ROPE) * gkr_ref[...]
    kr = _rope128(kp, c, sa, sb)
    kpe_ref[...] = kr[:, 0:QK_ROPE]
    ckv_b = ckv.astype(BF16)
    kn = _dot(ckv_b, wuk_ref[...])
    kn = kn * lax.rsqrt(_seg_sum(kn * kn, ek_ref[...]) + EPS) * gk_ref[...]
    k_ref[...] = (kn + jnp.concatenate([kr] * 4, axis=1)).astype(BF16)
    v_ref[...] = _dot(ckv_b, wuv_ref[...]).astype(BF16)


def mla_prep(pa, tabs, wts, n_tab_blocks):
    n = pa.shape[0]
    c, sa, sb = tabs
    tm = c.shape[0] // n_tab_blocks
    assert n % tm == 0
    row = lambda wd: pl.BlockSpec((tm, wd), lambda i: (i, 0))
    tab = pl.BlockSpec((tm, LANES), lambda i: (i % n_tab_blocks, 0))
    return pl.pallas_call(
        _mla_prep_kernel,
        grid=(n // tm,),
        in_specs=[row(PA_W), tab, tab, tab] + [_full(w.shape) for w in wts],
        out_specs=[row(512), row(512), row(256), row(KV_LORA), row(QK_ROPE)],
        out_shape=[jax.ShapeDtypeStruct((n, 512), BF16), jax.ShapeDtypeStruct((n, 512), BF16),
                   jax.ShapeDtypeStruct((n, 256), BF16), jax.ShapeDtypeStruct((n, KV_LORA), F32),
                   jax.ShapeDtypeStruct((n, QK_ROPE), F32)],
        compiler_params=_cparams("parallel"),
        name="mla_prep",
    )(pa, c, sa, sb, *wts)


def _mla_attn_prompt_kernel(q_ref, k_ref, v_ref, o_ref, *, tq):
    i = pl.program_id(1)
    outs = []
    for h0 in range(0, N_HEADS, 2):
        heads = (h0, h0 + 1)

        def blk(j, carry, masked, heads=heads):
            off = pl.multiple_of(j * tq, tq)
            new = []
            for n, h in enumerate(heads):
                m, l, acc = carry[3 * n:3 * n + 3]
                kb = k_ref[pl.ds(off, tq), h * LANES:(h + 1) * LANES]
                vb = v_ref[pl.ds(off, tq), h * HEAD_DIM:(h + 1) * HEAD_DIM]
                s = _dot_nt(q_ref[:, h * LANES:(h + 1) * LANES], kb)
                if masked:
                    r = lax.broadcasted_iota(jnp.int32, (tq, tq), 0)
                    cc = lax.broadcasted_iota(jnp.int32, (tq, tq), 1)
                    s = jnp.where(cc <= r, s, NEG)
                m_new = jnp.maximum(m, jnp.max(s, axis=-1, keepdims=True))
                corr = jnp.exp(m - m_new)
                p = jnp.exp(s - m_new)
                l = l * corr + jnp.sum(p, axis=-1, keepdims=True)
                acc = acc * corr + _dot(p.astype(BF16), vb)
                new += [m_new, l, acc]
            return tuple(new)

        init = (jnp.full((tq, 1), NEG, F32), jnp.zeros((tq, 1), F32), jnp.zeros((tq, HEAD_DIM), F32)) * 2
        carry = lax.fori_loop(0, i, functools.partial(blk, masked=False), init)
        carry = blk(i, carry, True)
        outs += [carry[2] / carry[1], carry[5] / carry[4]]
    o_ref[...] = jnp.concatenate(outs, axis=1)


def mla_attn_prompt(q, k, v, b, t):
    tq = _tile(t, 512)
    nq = t // tq
    return pl.pallas_call(
        functools.partial(_mla_attn_prompt_kernel, tq=tq),
        grid=(b, nq),
        in_specs=[pl.BlockSpec((tq, 512), lambda bi, i: (bi * nq + i, 0)),
                  pl.BlockSpec((t, 512), lambda bi, i: (bi, 0)),
                  pl.BlockSpec((t, 256), lambda bi, i: (bi, 0))],
        out_specs=pl.BlockSpec((tq, 256), lambda bi, i: (bi * nq + i, 0)),
        out_shape=jax.ShapeDtypeStruct((b * t, 256), F32),
        compiler_params=_cparams("parallel", "arbitrary"),
        name="mla_attn_prompt",
    )(q, k, v)


N_SUB = 8


def _mla_attn_sample_kernel(pt_ref, qbd_ref, qpe_ref, ckvn_ref, kpen_ref, wuk_ref, wukt_ref, wuv_ref, ckv_hbm,
                            kpe_hbm, o_ref, ckv_buf, kpe_buf, sem, m_scr, l_scr, acc_scr, lhs_scr, *, n_pg, tq,
                            layer):
    b_id = pl.program_id(0)
    p_id = pl.program_id(1)
    n_steps = pl.num_programs(1)
    step = b_id * n_steps + p_id
    slot = step % 2

    def page_copies(b, p, sl):
        copies = []
        for k in range(n_pg):
            phys = pt_ref[b, p * n_pg + k]
            copies.append(pltpu.make_async_copy(ckv_hbm.at[phys, layer], ckv_buf.at[sl, k], sem.at[sl, 0]))
            copies.append(pltpu.make_async_copy(kpe_hbm.at[phys, layer], kpe_buf.at[sl, k], sem.at[sl, 1]))
        return copies

    def start_all(copies):
        for n, c in enumerate(copies):
            c.start(priority=(n // 2) % 2)

    @pl.when(step == 0)
    def _():
        start_all(page_copies(b_id, p_id, slot))

    @pl.when(step + 1 < pl.num_programs(0) * n_steps)
    def _():
        wrap = p_id + 1 == n_steps
        start_all(page_copies(jnp.where(wrap, b_id + 1, b_id), jnp.where(wrap, 0, p_id + 1), 1 - slot))

    qpe = qpe_ref[0]
    nr = N_HEADS * tq

    @pl.when(p_id == 0)
    def _():
        lhs_scr[0:GROUP_W, :] = wukt_ref[...]
        lhs_scr[GROUP_W:GROUP_W + nr, :] = _dot_nt(qbd_ref[0], wuk_ref[...]).astype(BF16)

    def scores(ckv, kpe_t):
        ckv_b = ckv.astype(BF16)
        out = _dot_nt(lhs_scr[...], ckv_b)
        rinv = []
        for h in range(N_HEADS):
            kn = out[h * HEAD_DIM:(h + 1) * HEAD_DIM]
            ss = jnp.sum(kn * kn, axis=0, keepdims=True) * (1.0 / HEAD_DIM)
            rinv.append(jnp.broadcast_to(lax.rsqrt(ss + EPS), (tq, ss.shape[1])))
        s = out[GROUP_W:GROUP_W + nr] * jnp.concatenate(rinv, axis=0) + _dot(qpe, kpe_t.astype(BF16))
        return s, ckv_b

    @pl.when(p_id == 0)
    def _():
        s, ckv_b = scores(ckvn_ref[0], kpen_ref[0])
        r = lax.broadcasted_iota(jnp.int32, s.shape, 0) % tq
        cc = lax.broadcasted_iota(jnp.int32, s.shape, 1)
        s = jnp.where(cc <= r, s, NEG)
        m = jnp.max(s, axis=-1, keepdims=True)
        p = jnp.exp(s - m)
        m_scr[...] = m
        l_scr[...] = jnp.sum(p, axis=-1, keepdims=True)
        acc_scr[...] = _dot(p.astype(BF16), ckv_b)

    for c in page_copies(b_id, p_id, slot):
        c.wait()

    n_sub = min(N_SUB, n_pg)
    per = n_pg // n_sub
    s_parts, ckv_parts = [], []
    for g in range(n_sub):
        ckv = jnp.concatenate([ckv_buf[slot, g * per + k] for k in range(per)], axis=0)
        kpe_t = jnp.concatenate([kpe_buf[slot, g * per + k] for k in range(per)], axis=1)
        s, ckv_b = scores(ckv, kpe_t)
        s_parts.append(s)
        ckv_parts.append(ckv_b)
    m = m_scr[...]
    m_new = m
    for s in s_parts:
        m_new = jnp.maximum(m_new, jnp.max(s, axis=-1, keepdims=True))
    corr = jnp.exp(m - m_new)
    l = l_scr[...] * corr
    acc = acc_scr[...] * corr
    for s, ckv_b in zip(s_parts, ckv_parts):
        p = jnp.exp(s - m_new)
        l = l + jnp.sum(p, axis=-1, keepdims=True)
        acc = acc + _dot(p.astype(BF16), ckv_b)
    m_scr[...] = m_new
    l_scr[...] = l
    acc_scr[...] = acc

    @pl.when(p_id == n_steps - 1)
    def _():
        lat = (acc / l).astype(BF16)
        full = _dot(lat, wuv_ref[...])
        lane_head = lax.broadcasted_iota(jnp.int32, (tq, 256), 1) // HEAD_DIM
        out = jnp.zeros((tq, 256), F32)
        for h in range(N_HEADS):
            out = jnp.where(lane_head == h, full[h * tq:(h + 1) * tq, :], out)
        o_ref[0] = out


def mla_attn_sample(page_table, qbd, qpe, ckv_new, kpe_new_t, wuk, wukt, wuv, cache_ckv, cache_kpe_t, layer):
    bs, n_pages = page_table.shape
    tq = qbd.shape[1] // N_HEADS
    n_pg = min(PAGES_PER_STEP, n_pages)
    assert n_pages % n_pg == 0
    nr = N_HEADS * tq
    per_b = lambda shp: pl.BlockSpec((1,) + shp, lambda b, p, pt: (b, 0, 0))
    cst = lambda shp: pl.BlockSpec(shp, lambda b, p, pt: (0,) * len(shp))
    hbm = pl.BlockSpec(memory_space=pl.ANY)
    grid_spec = pltpu.PrefetchScalarGridSpec(
        num_scalar_prefetch=1,
        grid=(bs, n_pages // n_pg),
        in_specs=[per_b((nr, 256)), per_b((nr, QK_ROPE)), per_b((PAGE, KV_LORA)), per_b((QK_ROPE, PAGE)),
                  cst(wuk.shape), cst(wukt.shape), cst(wuv.shape), hbm, hbm],
        out_specs=pl.BlockSpec((1, tq, 256), lambda b, p, pt: (b, 0, 0)),
        scratch_shapes=[pltpu.VMEM((2, n_pg, PAGE, KV_LORA), F32), pltpu.VMEM((2, n_pg, QK_ROPE, PAGE), F32),
                        pltpu.SemaphoreType.DMA((2, 2)),
                        pltpu.VMEM((nr, 1), F32), pltpu.VMEM((nr, 1), F32), pltpu.VMEM((nr, KV_LORA), F32),
                        pltpu.VMEM((GROUP_W + nr, KV_LORA), BF16)],
    )
    return pl.pallas_call(
        functools.partial(_mla_attn_sample_kernel, n_pg=n_pg, tq=tq, layer=layer),
        grid_spec=grid_spec,
        out_shape=jax.ShapeDtypeStruct((bs, tq, 256), F32),
        compiler_params=_cparams("arbitrary", "arbitrary"),
        name="mla_attn_sample",
    )(page_table, qbd, qpe, ckv_new, kpe_new_t, wuk, wukt, wuv, cache_ckv, cache_kpe_t)


CONV_HALO = 32


def _conv_kernel(pb_ref, st_ref, cw_ref, cb_ref, e_ref, g_ref, b_ref, pw_ref, o_ref, st_out_ref, xbuf, *, tt):
    j = pl.program_id(1)
    lo = CONV_HALO - (CONV_W - 1)

    @pl.when(j == 0)
    def _():
        xbuf[pl.ds(lo, CONV_W - 1), :] = st_ref[0]

    pb = pb_ref[...]
    xbuf[pl.ds(CONV_HALO, tt), :] = pb[:, 0:256] * _sigmoid(pb[:, 256:512])
    y = jnp.zeros((tt, 256), F32) + cb_ref[...]
    for k in range(CONV_W):
        y = y + xbuf[pl.ds(lo + k, tt), :] * cw_ref[pl.ds(k, 1), :]
    new_state = xbuf[pl.ds(lo + tt, CONV_W - 1), :]
    xbuf[pl.ds(lo, CONV_W - 1), :] = new_state
    st_out_ref[0] = new_state
    e = e_ref[...]
    yc = y - _seg_sum(y, e)
    yn = yc * lax.rsqrt(_seg_sum(yc * yc, e) + LN_EPS) * g_ref[...] + b_ref[...]
    act = yn * _sigmoid(yn)
    o_ref[...] = _dot(act.astype(BF16), pw_ref[...])


def conv_module(pb, state, wts, b, t):
    tt = _tile(t, 256)
    nt = t // tt
    return pl.pallas_call(
        functools.partial(_conv_kernel, tt=tt),
        grid=(b, nt),
        in_specs=[pl.BlockSpec((tt, 512), lambda bi, j: (bi * nt + j, 0)),
                  pl.BlockSpec((1, CONV_W - 1, 256), lambda bi, j: (bi, 0, 0))]
                 + [_full(w.shape) for w in wts],
        out_specs=[pl.BlockSpec((tt, 256), lambda bi, j: (bi * nt + j, 0)),
                   pl.BlockSpec((1, CONV_W - 1, 256), lambda bi, j: (bi, 0, 0))],
        out_shape=[jax.ShapeDtypeStruct((b * t, 256), F32), jax.ShapeDtypeStruct((b, CONV_W - 1, 256), F32)],
        scratch_shapes=[pltpu.VMEM((CONV_HALO + tt, 256), F32)],
        compiler_params=_cparams("parallel", "arbitrary"),
        name="conv_module",
    )(pb, state, *wts)


def _sgu_kernel(pc_ref, g_ref, b_ref, w_ref, bias_ref, o_ref, v_ref, *, n_chunks):
    x = pc_ref[...]
    z = 0.5 * x * (1.0 + jnp.tanh(0.7978845608028654 * (x + 0.044715 * (x * x * x))))
    u = z[:, 0:256]
    v = z[:, 256:512]
    vc = v - jnp.mean(v, axis=-1, keepdims=True)
    v = vc * lax.rsqrt(jnp.mean(vc * vc, axis=-1, keepdims=True) + LN_EPS) * g_ref[...] + b_ref[...]
    v_ref[...] = v
    lane_head = lax.broadcasted_iota(jnp.int32, (CHUNK, 256), 1) // HEAD_DIM
    for c in range(n_chunks):
        vcb = v[c * CHUNK:(c + 1) * CHUNK, :]
        sv = bias_ref[...]
        for h in range(N_HEADS):
            sv = sv + _dot(w_ref[h], jnp.where(lane_head == h, vcb, 0.0).astype(BF16))
        o_ref[pl.ds(c * CHUNK, CHUNK), :] = u[c * CHUNK:(c + 1) * CHUNK, :] * sv


def sgu(pc, wts):
    n = pc.shape[0]
    tm = _tile(n, 512)
    row = lambda wd: pl.BlockSpec((tm, wd), lambda i: (i, 0))
    return pl.pallas_call(
        functools.partial(_sgu_kernel, n_chunks=tm // CHUNK),
        grid=(n // tm,),
        in_specs=[row(512)] + [_full(w.shape) for w in wts],
        out_specs=[row(256), row(256)],
        out_shape=[jax.ShapeDtypeStruct((n, 256), F32), jax.ShapeDtypeStruct((n, 256), F32)],
        compiler_params=_cparams("parallel"),
        name="sgu",
    )(pc, *wts)


RW_HALO = 8


def _rwkv_prep_kernel(pd_ref, sh_ref, mu_ref, w0_ref, w2_ref, a0_ref, a2_ref, g2_ref, kkp_ref, ka_ref, rk_ref,
                      e_ref, r_ref, w_ref, k_ref, v_ref, kk_ref, kka_ref, g_ref, bv_ref, xbuf, *, tt):
    j = pl.program_id(1)

    @pl.when(j == 0)
    def _():
        xbuf[pl.ds(RW_HALO - 1, 1), :] = sh_ref[0]

    pd = pd_ref[...]
    xbuf[pl.ds(RW_HALO, tt), :] = pd
    prev = xbuf[pl.ds(RW_HALO - 1, tt), :]
    xbuf[pl.ds(RW_HALO - 1, 1), :] = pd[tt - 1:tt, :]
    xs = pd + (prev - pd) * mu_ref[...]
    r = xs[:, 0:256]
    k = xs[:, 256:512]
    v = xs[:, 512:768]
    xwa = xs[:, 768:896]
    xg = xs[:, 896:1024]
    z = -(w0_ref[...] + _dot(jnp.tanh(xwa).astype(BF16), w2_ref[...]))
    softplus = jnp.maximum(z, 0.0) + jnp.log(1.0 + jnp.exp(-jnp.abs(z)))
    w_ref[...] = jnp.exp(-jnp.exp(-softplus - 0.5))
    a = _sigmoid(a0_ref[...] + _dot(xwa.astype(BF16), a2_ref[...]))
    g_ref[...] = _dot(_sigmoid(xg).astype(BF16), g2_ref[...])
    e = e_ref[...]
    kk = k * kkp_ref[...]
    kk = kk * lax.rsqrt(_seg_sum(kk * kk, e) + 1e-12)
    k2 = k * (1.0 + (a - 1.0) * ka_ref[...])
    r_ref[...] = r
    k_ref[...] = k2
    v_ref[...] = v
    kk_ref[...] = kk
    kka_ref[...] = kk * a
    bv_ref[...] = _seg_sum(r * k2 * rk_ref[...], e) * v


def rwkv_prep(pd, shift, wts, b, t):
    tt = _tile(t, 256)
    nt = t // tt
    row = lambda wd: pl.BlockSpec((tt, wd), lambda bi, j: (bi * nt + j, 0))
    return pl.pallas_call(
        functools.partial(_rwkv_prep_kernel, tt=tt),
        grid=(b, nt),
        in_specs=[row(PD_W), pl.BlockSpec((1, 1, PD_W), lambda bi, j: (bi, 0, 0))]
                 + [_full(w.shape) for w in wts],
        out_specs=[row(256)] * 8,
        out_shape=[jax.ShapeDtypeStruct((b * t, 256), F32)] * 8,
        scratch_shapes=[pltpu.VMEM((RW_HALO + tt, PD_W), F32)],
        compiler_params=_cparams("parallel", "arbitrary"),
        name="rwkv_prep",
    )(pd, shift, *wts)


RW_BLOCK = 128


def _pack_bf16_pair(a, b):
    ua = lax.bitcast_convert_type(a.astype(BF16).astype(F32), jnp.uint32)
    ub = lax.bitcast_convert_type(b.astype(BF16).astype(F32), jnp.uint32)
    return lax.bitcast_convert_type(ua | (ub >> 16), jnp.int32)


def _unpack_bf16_pair(word):
    u = lax.bitcast_convert_type(word, jnp.uint32)
    return (lax.bitcast_convert_type(u & jnp.uint32(0xFFFF0000), F32),
            lax.bitcast_convert_type(u << 16, F32))


def _rwkv_scan_kernel(w_ref, kk_ref, kka_ref, k_ref, r_ref, v_ref, s0_ref, e_ref, y_ref, sf_ref, s_scr, col_scr, *,
                      nb, sblk):
    c = pl.program_id(1)

    @pl.when(c == 0)
    def _():
        s_scr[...] = s0_ref[...]

    lane = lax.broadcasted_iota(jnp.int32, (HEAD_DIM, LANES), 1)
    low = lane < HEAD_DIM
    n_half = 2 if sblk > HEAD_DIM else 1
    for b in range(nb):
        for p in range(2):
            blk = lambda ref: ref[b, :, p * LANES:(p + 1) * LANES]
            x = _pack_bf16_pair(blk(kk_ref), blk(kka_ref))
            if sblk < LANES:
                x = jnp.concatenate([x, jnp.zeros((LANES - sblk, LANES), jnp.int32)], axis=0)
            xt = x.T
            h0, h1 = xt[0:HEAD_DIM], xt[HEAD_DIM:LANES]
            col_scr[b, p, 0] = jnp.where(low, h0, pltpu.roll(h1, HEAD_DIM, 1))
            if n_half == 2:
                col_scr[b, p, 1] = jnp.where(low, pltpu.roll(h0, HEAD_DIM, 1), h1)

    base = jnp.where(low, 0, HEAD_DIM)
    diag = (lax.broadcasted_iota(jnp.int32, (HEAD_DIM, 256), 1) % HEAD_DIM
            == lax.broadcasted_iota(jnp.int32, (HEAD_DIM, 256), 0))
    for half in range(n_half):
        n_groups = min(sblk - half * HEAD_DIM, HEAD_DIM) // SUBLANES

        def group(gi, carry, half=half):
            row0 = pl.multiple_of(half * HEAD_DIM + gi * SUBLANES, SUBLANES)
            for b in range(nb):
                v8 = v_ref[b, pl.ds(row0, SUBLANES), :]
                rows8 = (k_ref[b, pl.ds(row0, SUBLANES), :], r_ref[b, pl.ds(row0, SUBLANES), :],
                         1.0 - w_ref[b, pl.ds(row0, SUBLANES), :])
                on_diag = [jnp.where(diag, jnp.broadcast_to(x8[i:i + 1, :], (HEAD_DIM, 256)), 0.0).astype(BF16)
                           for i in range(SUBLANES) for x8 in rows8]
                spread = _dot(jnp.concatenate(on_diag, axis=0), e_ref[...])
                ys = []
                for p in range(2):
                    st = s_scr[b, :, p * LANES:(p + 1) * LANES]
                    yp = []
                    for i in range(SUBLANES):
                        idx = base + (gi * SUBLANES + i)
                        kk, kka = _unpack_bf16_pair(jnp.take_along_axis(col_scr[b, p, half], idx, axis=1))
                        part = lambda n: spread[(3 * i + n) * HEAD_DIM:(3 * i + n + 1) * HEAD_DIM,
                                                p * LANES:(p + 1) * LANES]
                        k, r, forget = part(0), part(1), part(2)
                        vrow = v8[i:i + 1, p * LANES:(p + 1) * LANES]
                        sa = -jnp.sum(st * kk, axis=0, keepdims=True)
                        st = st - st * forget + kka * sa + k * vrow
                        yp.append(jnp.sum(st * r, axis=0, keepdims=True))
                    s_scr[b, :, p * LANES:(p + 1) * LANES] = st
                    ys.append(jnp.concatenate(yp, axis=0))
                y_ref[b, pl.ds(row0, SUBLANES), :] = jnp.concatenate(ys, axis=1)
            return carry

        lax.fori_loop(0, n_groups, group, 0)

    @pl.when(c == pl.num_programs(1) - 1)
    def _():
        sf_ref[...] = s_scr[...]


def rwkv_scan(seqs, v, s0, nb):
    b, t, _ = v.shape
    e_heads = _head_matrix(1.0)
    sblk = min(t, RW_BLOCK)
    assert b % nb == 0 and t % sblk == 0 and sblk % SUBLANES == 0 and (sblk <= HEAD_DIM or sblk == RW_BLOCK)
    st_spec = pl.BlockSpec((nb, HEAD_DIM, 256), lambda bi, c: (bi, 0, 0))
    seq_spec = pl.BlockSpec((nb, sblk, 256), lambda bi, c: (bi, c, 0))
    return pl.pallas_call(
        functools.partial(_rwkv_scan_kernel, nb=nb, sblk=sblk),
        grid=(b // nb, t // sblk),
        in_specs=[seq_spec] * 6 + [st_spec, _full(e_heads.shape)],
        out_specs=[seq_spec, st_spec],
        out_shape=[jax.ShapeDtypeStruct((b, t, 256), F32), jax.ShapeDtypeStruct((b, HEAD_DIM, 256), F32)],
        scratch_shapes=[pltpu.VMEM((nb, HEAD_DIM, 256), F32),
                        pltpu.VMEM((nb, 2, 2, HEAD_DIM, LANES), jnp.int32)],
        compiler_params=_cparams("parallel", "arbitrary"),
        name="rwkv_scan",
    )(*seqs, v, s0, e_heads)


def _mix_out_kernel(x_ref, oa_ref, ob_ref, oc_ref, y_ref, bv_ref, g_ref, e_ref, lg_ref, lb_ref, on_ref, w_ref,
                    nx_ref, wq_ref, qg_ref, o_ref, q_ref):
    e = e_ref[...]
    y = y_ref[...]
    yc = y - _seg_sum(y, e)
    yn = yc * lax.rsqrt(_seg_sum(yc * yc, e) + RW_LN_EPS) * lg_ref[...] + lb_ref[...]
    od = (yn + bv_ref[...]) * g_ref[...]
    acc = x_ref[...]
    for gi, o in enumerate((oa_ref[...], ob_ref[...], oc_ref[...], od)):
        on = (_rms(o) * on_ref[:, gi * 256:(gi + 1) * 256]).astype(BF16)
        acc = acc + _dot(on, w_ref[pl.ds(gi * 256, 256), :])
    o_ref[...] = acc
    q = _dot((_rms(acc) * nx_ref[...]).astype(BF16), wq_ref[...])
    heads = [_rms(q[:, h * X_HEAD_DIM:(h + 1) * X_HEAD_DIM]) * qg_ref[...] * X_HEAD_DIM ** -0.5
             for h in range(X_HEADS)]
    q_ref[...] = jnp.concatenate(heads, axis=1).astype(q_ref.dtype)


def mix_out(x, oa, ob, oc, y, bv, g, wts, q_dtype):
    n, d = x.shape
    tm = _tile(n, 512)
    row = lambda wd: pl.BlockSpec((tm, wd), lambda i: (i, 0))
    return pl.pallas_call(
        _mix_out_kernel,
        grid=(n // tm,),
        in_specs=[row(d)] + [row(256)] * 6 + [_full(w.shape) for w in wts],
        out_specs=[row(d), row(512)],
        out_shape=[jax.ShapeDtypeStruct((n, d), F32), jax.ShapeDtypeStruct((n, 512), q_dtype)],
        compiler_params=_cparams("parallel"),
        name="mix_out",
    )(x, oa, ob, oc, y, bv, g, *wts)


XATTN_SHORT_ROWS = 64


def _xattn_kernel(x_ref, q_ref, k_ref, v_ref, wo_ref, o_ref, *, n_seq, rows, n_mem):
    seq_outs = []
    for s in range(n_seq):
        qs = [q_ref[s * rows:(s + 1) * rows, h * X_HEAD_DIM:(h + 1) * X_HEAD_DIM].astype(BF16)
              for h in range(X_HEADS)]
        if n_seq == 1:
            outs = []
            for h in range(X_HEADS):
                head_rows = pl.ds(h, n_mem, stride=X_HEADS)
                sc = _dot_nt(qs[h], k_ref[s, head_rows, :].astype(BF16))
                p = jnp.exp(sc - jnp.max(sc, axis=-1, keepdims=True))
                p = p / jnp.sum(p, axis=-1, keepdims=True)
                outs.append(_dot(p.astype(BF16), v_ref[s, head_rows, :].astype(BF16)))
        else:
            sc = _dot_nt(jnp.concatenate(qs, axis=0), k_ref[s].astype(BF16))
            row_head = lax.broadcasted_iota(jnp.int32, sc.shape, 0) // rows
            col_head = lax.broadcasted_iota(jnp.int32, sc.shape, 1) % X_HEADS
            sc = jnp.where(row_head == col_head, sc, NEG)
            p = jnp.exp(sc - jnp.max(sc, axis=-1, keepdims=True))
            p = p / jnp.sum(p, axis=-1, keepdims=True)
            o = _dot(p.astype(BF16), v_ref[s].astype(BF16))
            outs = [o[h * rows:(h + 1) * rows] for h in range(X_HEADS)]
        seq_outs.append(jnp.concatenate(outs, axis=1))
    xo = jnp.concatenate(seq_outs, axis=0).astype(BF16)
    o_ref[...] = x_ref[...] + _dot(xo, wo_ref[...])


def xattn(x, q, mem_k, mem_v, wo, b, t, kv_seq0):
    n, d = x.shape
    n_mem = mem_k.shape[1] // X_HEADS
    if t >= XATTN_SHORT_ROWS:
        n_seq, rows = 1, _tile(t, 512)
    else:
        n_seq, rows = _tile(b, XATTN_SHORT_ROWS // t), t
    nq = t // rows
    assert kv_seq0 % n_seq == 0
    row = lambda wd: pl.BlockSpec((n_seq * rows, wd), lambda bi, i: (bi * nq + i, 0))
    kv = pl.BlockSpec((n_seq, n_mem * X_HEADS, X_HEAD_DIM), lambda bi, i: (kv_seq0 // n_seq + bi, 0, 0))
    return pl.pallas_call(
        functools.partial(_xattn_kernel, n_seq=n_seq, rows=rows, n_mem=n_mem),
        grid=(b // n_seq, nq),
        in_specs=[row(d), row(512), kv, kv, _full(wo.shape)],
        out_specs=row(d),
        out_shape=jax.ShapeDtypeStruct((n, d), F32),
        compiler_params=_cparams("parallel", "arbitrary"),
        name="xattn",
    )(x, q, mem_k, mem_v, wo)


def _mem_kv_kernel(x_ref, g_ref, w_ref, hg_ref, k_ref, v_ref, *, tm):
    hb = (_rms(x_ref[...]) * g_ref[...]).astype(BF16)
    y = _dot(hb, w_ref[...])
    for h in range(X_HEADS):
        head_rows = pl.ds(h, tm, stride=X_HEADS)
        k_ref[head_rows, :] = _rms(y[:, h * X_HEAD_DIM:(h + 1) * X_HEAD_DIM]) * hg_ref[...]
        v_ref[head_rows, :] = y[:, (X_HEADS + h) * X_HEAD_DIM:(X_HEADS + h + 1) * X_HEAD_DIM]


def mem_kv(x, g, w, hg):
    n, d = x.shape
    tm = _tile(n, 512)
    out = pl.BlockSpec((tm * X_HEADS, X_HEAD_DIM), lambda i: (i, 0))
    return pl.pallas_call(
        functools.partial(_mem_kv_kernel, tm=tm),
        grid=(n // tm,),
        in_specs=[pl.BlockSpec((tm, d), lambda i: (i, 0)), _full(g.shape), _full(w.shape), _full(hg.shape)],
        out_specs=[out, out],
        out_shape=[jax.ShapeDtypeStruct((n * X_HEADS, X_HEAD_DIM), F32)] * 2,
        compiler_params=_cparams("parallel"),
        name="mem_kv",
    )(x, g, w, hg)


FFN_CHUNK = 256


def _ffn_kernel(x_ref, g_ref, wg_ref, wu_ref, wo_ref, o_ref, *, n_chunks):
    x = x_ref[...]
    hb = (_rms(x) * g_ref[...]).astype(BF16)
    acc = x
    for c in range(n_chunks):
        a = _dot(hb, wg_ref[c])
        u = _dot(hb, wu_ref[c])
        acc = acc + _dot((a * _sigmoid(a) * u).astype(BF16), wo_ref[c])
    o_ref[...] = acc


def ffn(x, g, wg, wu, wo):
    n, d = x.shape
    tm = _tile(n, 512)
    row = pl.BlockSpec((tm, d), lambda i: (i, 0))
    return pl.pallas_call(
        functools.partial(_ffn_kernel, n_chunks=wg.shape[0]),
        grid=(n // tm,),
        in_specs=[row, _full(g.shape), _full(wg.shape), _full(wu.shape), _full(wo.shape)],
        out_specs=row,
        out_shape=jax.ShapeDtypeStruct((n, d), F32),
        compiler_params=_cparams("parallel"),
        name="ffn",
    )(x, g, wg, wu, wo)


def _seg_matrix(seg_ids, seg_len):
    s = np.asarray(seg_ids)
    m = (s[:, None] == s[None, :]) & (s[:, None] >= 0)
    return jnp.asarray(m.astype(np.float32) / np.asarray(seg_len, np.float32)[None, :], BF16)


def _mla_segments():
    lane = np.arange(512)
    blk, off = lane // 128, lane % 128
    q_ids = np.where(off < 32, 2 * blk, np.where(off < 64, -1, 2 * blk + 1))
    q_len = np.where(off < 32, 32.0, 64.0)
    k_ids = np.where(off < 64, -1, blk)
    k_len = np.full(512, 64.0)
    return _seg_matrix(q_ids, q_len), _seg_matrix(k_ids, k_len)


def _head_matrix(scale_len):
    lane = np.arange(256)
    return _seg_matrix(lane // HEAD_DIM, np.full(256, scale_len))


def _rope_tables(pos):
    half = QK_ROPE // 2
    inv = jnp.power(ROPE_BASE, -jnp.arange(half, dtype=F32) / half)
    ang = pos.astype(F32)[:, None] * inv[None, :]
    cos, sin = jnp.cos(ang), jnp.sin(ang)
    n = pos.shape[0]
    z = lambda w: jnp.zeros((n, w), F32)
    c = jnp.concatenate([cos, cos, jnp.ones((n, LANES - QK_ROPE), F32)], axis=1)
    sa = jnp.concatenate([-sin, z(LANES - half)], axis=1)
    sb = jnp.concatenate([z(half), sin, z(LANES - QK_ROPE)], axis=1)
    return c, sa, sb


def _layer_weights(p):
    row = lambda v: v.reshape(1, -1).astype(F32)
    zc = lambda a, w: jnp.zeros((a.shape[0], w), a.dtype)
    w_in = p['w_in']
    c1, c2, c3 = 352, 352 + 512, 352 + 1024
    pa = w_in[:, :c1]
    w_in_p = jnp.concatenate(
        [pa[:, :Q_LORA], zc(pa, 64), pa[:, Q_LORA:Q_LORA + KV_LORA], pa[:, Q_LORA + KV_LORA:], zc(pa, 96),
         w_in[:, c1:c2], w_in[:, c2:c3], w_in[:, c3:]], axis=1).astype(BF16)

    wuq = p['mla_w_uq'].reshape(Q_LORA, N_HEADS, QK_NOPE + QK_ROPE)
    wuq = jnp.concatenate([wuq[:, :, QK_NOPE:], jnp.zeros((Q_LORA, N_HEADS, 32), F32), wuq[:, :, :QK_NOPE]], axis=2)
    wuq = jnp.concatenate([wuq.reshape(Q_LORA, 512), jnp.zeros((256 - Q_LORA, 512), F32)], axis=0).astype(BF16)
    wuk = p['mla_w_uk'].reshape(KV_LORA, N_HEADS, QK_NOPE)
    wuk_p = jnp.concatenate([jnp.zeros((KV_LORA, N_HEADS, 64), F32), wuk], axis=2).reshape(KV_LORA, 512).astype(BF16)
    blk = lambda a, b_, c_: jnp.tile(jnp.concatenate([a, b_, c_]), N_HEADS).reshape(1, 512)
    z32, z64 = jnp.zeros((32,), F32), jnp.zeros((64,), F32)
    gq = blk(p['mla_gq_rope'], z32, p['mla_gq_nope'])
    gk = blk(z32, z32, p['mla_gk_nope'])
    gqn = jnp.concatenate([p['mla_q_norm'], z64]).reshape(1, 256)
    gkr = jnp.concatenate([p['mla_gk_rope'], jnp.zeros((96,), F32)]).reshape(1, LANES)
    eq, ek = _mla_segments()
    mla = (gqn, row(p['mla_kv_norm']), wuq, eq, gq, gkr, wuk_p, ek, gk, p['mla_w_uv'].astype(BF16))

    e64 = _head_matrix(64.0)
    conv = (p['conv_w'], row(p['conv_b']), e64, row(p['conv_norm_g']), row(p['conv_norm_b']),
            p['conv_pw'].astype(BF16))

    z64r = jnp.zeros((64, GROUP_W), F32)
    rw = (row(p['rw_mu']),
          row(p['rw_w0']), jnp.concatenate([p['rw_w2'], z64r], axis=0).astype(BF16),
          row(p['rw_a0']), jnp.concatenate([z64r, p['rw_a2']], axis=0).astype(BF16),
          p['rw_g2'].astype(BF16), row(p['rw_kk']), row(p['rw_ka']), row(p['rw_rk']), _head_matrix(1.0))

    mix = (e64, row(p['rw_ln_g']), row(p['rw_ln_b']), row(p['out_norm']), p['w_out'].astype(BF16))

    d_ff = p['w_ffn_out'].shape[0]
    nck = d_ff // FFN_CHUNK
    d = w_in.shape[0]
    wg = p['w_ffn_in'][:, :d_ff].reshape(d, nck, FFN_CHUNK).transpose(1, 0, 2).astype(BF16)
    wu = p['w_ffn_in'][:, d_ff:].reshape(d, nck, FFN_CHUNK).transpose(1, 0, 2).astype(BF16)
    wo = p['w_ffn_out'].reshape(nck, FFN_CHUNK, d).astype(BF16)

    return dict(
        norm_mix=row(p['norm_mix']), w_in=w_in_p, mla=mla, conv=conv, rw=rw, mix=mix,
        wuk=p['mla_w_uk'].astype(BF16), wuv=p['mla_w_uv'].astype(BF16), gk_nope=p['mla_gk_nope'],
        sgu_ln=(row(p['sgu_norm_g']), row(p['sgu_norm_b'])), sgu_w=p['sgu_w'], sgu_b=p['sgu_b'],
        norm_x=row(p['norm_x']), wq=p['wq_x'].astype(BF16), xq_norm=row(p['xq_norm']),
        mem_norm=row(p['mem_norm']), wkv=jnp.concatenate([p['wk_x'], p['wv_x']], axis=1).astype(BF16),
        xk_norm=row(p['xk_norm']), wo_x=p['wo_x'].astype(BF16),
        norm_ffn=row(p['norm_ffn']), wg=wg, wu=wu, wo=wo)


def _sgu_weights(lw, t):
    l = min(t, CHUNK)
    w = lw['sgu_w'][:, :l, :l] * jnp.tril(jnp.ones((l, l), F32))
    reps = CHUNK // l
    if reps > 1:
        w = jnp.einsum('ab,hij->haibj', jnp.eye(reps, dtype=F32), w).reshape(N_HEADS, CHUNK, CHUNK)
    bias = jnp.tile(lw['sgu_b'][:, :l].T, (reps, 1))
    bias = jnp.repeat(bias, HEAD_DIM, axis=1)
    return lw['sgu_ln'] + (w.astype(BF16), bias)


def _trunk_layer(x, b, t, lw, tabs, n_tab_blocks, attend, mem_k, mem_v, kv_seq0, conv_state, shift_state, wkv_state,
                 scan_nb):
    pa, pb, pc, pd = proj_in(x, lw['norm_mix'], lw['w_in'])
    q, k, v, ckv, kpe = mla_prep(pa, tabs, lw['mla'], n_tab_blocks)
    oa = attend(q, k, v, ckv, kpe)
    ob, conv_new = conv_module(pb, conv_state, lw['conv'], b, t)
    oc, v_sgu = sgu(pc, _sgu_weights(lw, t))
    r, w, k2, vv, kk, kka, g, bv = rwkv_prep(pd, shift_state.reshape(b, 1, PD_W), lw['rw'], b, t)
    seqs = [a.reshape(b, t, 256) for a in (w, kk, kka, k2, r)]
    s0 = wkv_state.transpose(0, 3, 1, 2).reshape(b, HEAD_DIM, 256)
    y, s_fin = rwkv_scan(seqs, vv.reshape(b, t, 256), s0, scan_nb)
    wkv_new = s_fin.reshape(b, HEAD_DIM, N_HEADS, HEAD_DIM).transpose(0, 2, 3, 1)
    shift_new = pd.reshape(b, t, PD_W)[:, -1]
    q_dtype = BF16 if t % 16 == 0 else F32
    x, qx = mix_out(x, oa, ob, oc, y.reshape(b * t, 256), bv, g,
                    lw['mix'] + (lw['norm_x'], lw['wq'], lw['xq_norm']), q_dtype)
    x = xattn(x, qx, mem_k, mem_v, lw['wo_x'], b, t, kv_seq0)
    x = ffn(x, lw['norm_ffn'], lw['wg'], lw['wu'], lw['wo'])
    return x, ckv, kpe, conv_new, shift_new, wkv_new, v_sgu


def kernel(x_prompt, x_sample, mem_prompt, cache_ckv, cache_kpe, cache_mem_k, cache_mem_v, state_conv, state_shift, state_wkv, page_table, norm_mix, w_in, mla_q_norm, mla_kv_norm, mla_w_uq, mla_w_uk, mla_w_uv, mla_gq_nope, mla_gq_rope, mla_gk_nope, mla_gk_rope, conv_w, conv_b, conv_norm_g, conv_norm_b, conv_pw, sgu_norm_g, sgu_norm_b, sgu_w, sgu_b, rw_mu, rw_w0, rw_w2, rw_a0, rw_a2, rw_g2, rw_kk, rw_ka, rw_rk, rw_ln_g, rw_ln_b, out_norm, w_out, norm_x, mem_norm, wq_x, wk_x, wv_x, xq_norm, xk_norm, wo_x, norm_ffn, w_ffn_in, w_ffn_out):
    params = dict(
        norm_mix=norm_mix, w_in=w_in, mla_q_norm=mla_q_norm, mla_kv_norm=mla_kv_norm, mla_w_uq=mla_w_uq,
        mla_w_uk=mla_w_uk, mla_w_uv=mla_w_uv, mla_gq_nope=mla_gq_nope, mla_gq_rope=mla_gq_rope,
        mla_gk_nope=mla_gk_nope, mla_gk_rope=mla_gk_rope, conv_w=conv_w, conv_b=conv_b, conv_norm_g=conv_norm_g,
        conv_norm_b=conv_norm_b, conv_pw=conv_pw, sgu_norm_g=sgu_norm_g, sgu_norm_b=sgu_norm_b, sgu_w=sgu_w,
        sgu_b=sgu_b, rw_mu=rw_mu, rw_w0=rw_w0, rw_w2=rw_w2, rw_a0=rw_a0, rw_a2=rw_a2, rw_g2=rw_g2, rw_kk=rw_kk,
        rw_ka=rw_ka, rw_rk=rw_rk, rw_ln_g=rw_ln_g, rw_ln_b=rw_ln_b, out_norm=out_norm, w_out=w_out, norm_x=norm_x,
        mem_norm=mem_norm, wq_x=wq_x, wk_x=wk_x, wv_x=wv_x, xq_norm=xq_norm, xk_norm=xk_norm, wo_x=wo_x,
        norm_ffn=norm_ffn, w_ffn_in=w_ffn_in, w_ffn_out=w_ffn_out)
    depth = w_in.shape[0]
    bp, tp, d = x_prompt.shape
    bs, ts, _ = x_sample.shape
    n_mem = mem_prompt.shape[1]
    n_pages = page_table.shape[1]
    past_len = n_pages * PAGE

    tm_p = _tile(tp, 512)
    tabs_p = _rope_tables(jnp.arange(tp, dtype=jnp.int32))
    tm_s = _tile(bs * ts, 512)
    tabs_s = tuple(jnp.tile(a, (tm_s // ts, 1)) for a in _rope_tables(past_len + jnp.arange(ts, dtype=jnp.int32)))
    cache_kpe_t = jnp.swapaxes(cache_kpe, 2, 3)
    cache_k_rows = cache_mem_k.reshape(depth * bs, n_mem * X_HEADS, X_HEAD_DIM)
    cache_v_rows = cache_mem_v.reshape(depth * bs, n_mem * X_HEADS, X_HEAD_DIM)
    y_p = x_prompt.reshape(bp * tp, d)
    y_s = x_sample.reshape(bs * ts, d)
    mem_flat = mem_prompt.reshape(bp * n_mem, d)
    zeros_conv = jnp.zeros((bp, CONV_W - 1, GROUP_W), F32)
    zeros_shift = jnp.zeros((bp, PD_W), F32)
    zeros_wkv = jnp.zeros((bp, N_HEADS, HEAD_DIM, HEAD_DIM), F32)
    outs_p, outs_s, memk_l, memv_l = [], [], [], []
    for l in range(depth):
        lw = _layer_weights({k_: v_[l] for k_, v_ in params.items()})

        mk, mv = mem_kv(mem_flat, lw['mem_norm'], lw['wkv'], lw['xk_norm'])
        memk_l.append(mk.reshape(bp, n_mem, X_HEADS, X_HEAD_DIM))
        memv_l.append(mv.reshape(bp, n_mem, X_HEADS, X_HEAD_DIM))

        def attend_p(q, k, v, ckv, kpe):
            return mla_attn_prompt(q, k, v, bp, tp)

        kv_rows = n_mem * X_HEADS
        res = _trunk_layer(y_p, bp, tp, lw, tabs_p, tp // tm_p, attend_p, mk.reshape(bp, kv_rows, X_HEAD_DIM),
                           mv.reshape(bp, kv_rows, X_HEAD_DIM), 0, zeros_conv, zeros_shift, zeros_wkv,
                           _tile(bp, SCAN_NB_PROMPT))
        y_p = res[0]
        outs_p.append(res[1:])

        def attend_s(q, k, v, ckv, kpe, l=l, lw=lw):
            qf = q.astype(F32).reshape(bs, ts, N_HEADS, LANES)
            qn = (qf[..., 64:] * lw['gk_nope']).transpose(0, 2, 1, 3)
            qbd = jnp.einsum('bhtj,hg->bhtgj', qn, jnp.eye(N_HEADS, dtype=F32)).reshape(bs, N_HEADS * ts, 256)
            qpe = qf[..., :QK_ROPE].transpose(0, 2, 1, 3).reshape(bs, N_HEADS * ts, QK_ROPE)
            pad = lambda a: jnp.pad(a.reshape(bs, ts, -1), ((0, 0), (0, PAGE - ts), (0, 0)))
            o = mla_attn_sample(page_table, qbd.astype(BF16), qpe.astype(BF16), pad(ckv),
                                pad(kpe).transpose(0, 2, 1), lw['wuk'], lw['wuk'].T, lw['wuv'], cache_ckv,
                                cache_kpe_t, l)
            return o.reshape(bs * ts, 256)

        res = _trunk_layer(y_s, bs, ts, lw, tabs_s, 1, attend_s, cache_k_rows, cache_v_rows, l * bs,
                           state_conv[l], state_shift[l], state_wkv[l], _tile(bs, SCAN_NB_SAMPLE))
        y_s = res[0]
        outs_s.append(res[1:])

    n_pp = tp // PAGE
    stack = lambda outs, i, ax: jnp.stack([o[i] for o in outs], axis=ax)
    ckv_prompt = stack(outs_p, 0, 0).reshape(depth, bp, n_pp, PAGE, KV_LORA).transpose(1, 2, 0, 3, 4)
    kpe_prompt = stack(outs_p, 1, 0).reshape(depth, bp, n_pp, PAGE, QK_ROPE).transpose(1, 2, 0, 3, 4)
    ckv_sample = stack(outs_s, 0, 0).reshape(depth, bs, ts, KV_LORA).transpose(1, 0, 2, 3)
    kpe_sample = stack(outs_s, 1, 0).reshape(depth, bs, ts, QK_ROPE).transpose(1, 0, 2, 3)
    return (y_p.reshape(bp, tp, d), y_s.reshape(bs, ts, d), ckv_prompt, kpe_prompt, ckv_sample, kpe_sample,
            jnp.stack(memk_l, 0), jnp.stack(memv_l, 0),
            stack(outs_p, 2, 0), stack(outs_s, 2, 0), stack(outs_p, 3, 0), stack(outs_s, 3, 0),
            stack(outs_p, 4, 0), stack(outs_s, 4, 0),
            stack(outs_s, 5, 0).reshape(depth, bs, ts, GROUP_W))
```

```python
import functools

import numpy as np
import jax
import jax.numpy as jnp
from jax import lax
from jax.experimental import pallas as pl
from jax.experimental.pallas import tpu as pltpu

F32 = jnp.float32
BF16 = jnp.bfloat16

EPS = 1e-6
LN_EPS = 1e-5
RW_LN_EPS = 64e-5
NEG = -1e30
ROPE_BASE = 10000.0

LANES = 128
SUBLANES = 8
VMEM_LIMIT_BYTES = 56 * 1024 * 1024

GROUP_W = 256
HEAD_DIM = 64
N_HEADS = 4
Q_LORA = 192
KV_LORA = 128
QK_ROPE = 32
QK_NOPE = 64
CONV_W = 31
CHUNK = 128
PAGE = 128
X_HEADS = 4
X_HEAD_DIM = 128
MLA_SCALE = (QK_NOPE + QK_ROPE) ** -0.5
PA_W = 512
PD_W = 1024
PAGES_PER_STEP = 32
SCAN_NB_PROMPT = 4
SCAN_NB_SAMPLE = 8


def _cparams(*sem):
    return pltpu.CompilerParams(dimension_semantics=sem, vmem_limit_bytes=VMEM_LIMIT_BYTES)


def _dot(a, b):
    return jnp.dot(a, b, preferred_element_type=F32)


def _dot_nt(a, b):
    return lax.dot_general(a, b, (((1,), (1,)), ((), ())), preferred_element_type=F32)


def _seg_sum(x, e):
    hi = x.astype(BF16)
    lo = (x - hi.astype(F32)).astype(BF16)
    return _dot(hi, e) + _dot(lo, e)


def _rms(x, width=None):
    w = x.shape[-1] if width is None else width
    return x * lax.rsqrt(jnp.sum(x * x, axis=-1, keepdims=True) * (1.0 / w) + EPS)


def _sigmoid(x):
    return 1.0 / (1.0 + jnp.exp(-x))


def _tile(n, pref):
    t = min(n, pref)
    while n % t:
        t //= 2
    return t


def _full(shape):
    nd = len(shape)
    return pl.BlockSpec(shape, lambda *a: (0,) * nd)


def _proj_in_kernel(x_ref, g_ref, w_ref, pa_ref, pb_ref, pc_ref, pd_ref):
    hb = (_rms(x_ref[...]) * g_ref[...]).astype(BF16)
    pa_ref[...] = _dot(hb, w_ref[:, 0:512])
    pb_ref[...] = _dot(hb, w_ref[:, 512:1024])
    pc_ref[...] = _dot(hb, w_ref[:, 1024:1536])
    pd_ref[...] = _dot(hb, w_ref[:, 1536:2560])


def proj_in(x, g, w):
    n, d = x.shape
    tm = _tile(n, 512)
    row = lambda wd: pl.BlockSpec((tm, wd), lambda i: (i, 0))
    return pl.pallas_call(
        _proj_in_kernel,
        grid=(n // tm,),
        in_specs=[row(d), _full(g.shape), _full(w.shape)],
        out_specs=[row(512), row(512), row(512), row(1024)],
        out_shape=[jax.ShapeDtypeStruct((n, wd), F32) for wd in (512, 512, 512, 1024)],
        compiler_params=_cparams("parallel"),
        name="proj_in",
    )(x, g, w)


def _rope128(x, c, sa, sb):
    w = x.shape[-1]
    return x * c + pltpu.roll(x, w - 16, 1) * sa + pltpu.roll(x, 16, 1) * sb


def _mla_prep_kernel(pa_ref, c_ref, sa_ref, sb_ref, gqn_ref, gkv_ref, wuq_ref, eq_ref, gq_ref,
                     gkr_ref, wuk_ref, ek_ref, gk_ref, wuv_ref,
                     q_ref, k_ref, v_ref, ckv_ref, kpe_ref):
    pa = pa_ref[...]
    c, sa, sb = c_ref[...], sa_ref[...], sb_ref[...]
    c4 = jnp.concatenate([c] * 4, axis=1)
    sa4 = jnp.concatenate([sa] * 4, axis=1)
    sb4 = jnp.concatenate([sb] * 4, axis=1)
    cq = _rms(pa[:, 0:256], Q_LORA) * gqn_ref[...]
    q = _dot(cq.astype(BF16), wuq_ref[...])
    qn = q * lax.rsqrt(_seg_sum(q * q, eq_ref[...]) + EPS) * gq_ref[...]
    q_ref[...] = (_rope128(qn, c4, sa4, sb4) * MLA_SCALE).astype(BF16)
    ckv = _rms(pa[:, 256:384]) * gkv_ref[...]
    ckv_ref[...] = ckv
    kp = _rms(pa[:, 384:512], QK_ROPE) * gkr_ref[...]
    kr = _rope128(kp, c, sa, sb)
    kpe_ref[...] = kr[:, 0:QK_ROPE]
    ckv_b = ckv.astype(BF16)
    kn = _dot(ckv_b, wuk_ref[...])
    kn = kn * lax.rsqrt(_seg_sum(kn * kn, ek_ref[...]) + EPS) * gk_ref[...]
    k_ref[...] = (kn + jnp.concatenate([kr] * 4, axis=1)).astype(BF16)
    v_ref[...] = _dot(ckv_b, wuv_ref[...]).astype(BF16)


def mla_prep(pa, tabs, wts, n_tab_blocks):
    n = pa.shape[0]
    c, sa, sb = tabs
    tm = c.shape[0] // n_tab_blocks
    assert n % tm == 0
    row = lambda wd: pl.BlockSpec((tm, wd), lambda i: (i, 0))
    tab = pl.BlockSpec((tm, LANES), lambda i: (i % n_tab_blocks, 0))
    return pl.pallas_call(
        _mla_prep_kernel,
        grid=(n // tm,),
        in_specs=[row(PA_W), tab, tab, tab] + [_full(w.shape) for w in wts],
        out_specs=[row(512), row(512), row(256), row(KV_LORA), row(QK_ROPE)],
        out_shape=[jax.ShapeDtypeStruct((n, 512), BF16), jax.ShapeDtypeStruct((n, 512), BF16),
                   jax.ShapeDtypeStruct((n, 256), BF16), jax.ShapeDtypeStruct((n, KV_LORA), F32),
                   jax.ShapeDtypeStruct((n, QK_ROPE), F32)],
        compiler_params=_cparams("parallel"),
        name="mla_prep",
    )(pa, c, sa, sb, *wts)


def _mla_attn_prompt_kernel(q_ref, k_ref, v_ref, o_ref, *, tq):
    i = pl.program_id(1)
    outs = []
    for h0 in range(0, N_HEADS, 2):
        heads = (h0, h0 + 1)

        def blk(j, carry, masked, heads=heads):
            off = pl.multiple_of(j * tq, tq)
            new = []
            for n, h in enumerate(heads):
                m, l, acc = carry[3 * n:3 * n + 3]
                kb = k_ref[pl.ds(off, tq), h * LANES:(h + 1) * LANES]
                vb = v_ref[pl.ds(off, tq), h * HEAD_DIM:(h + 1) * HEAD_DIM]
                s = _dot_nt(q_ref[:, h * LANES:(h + 1) * LANES], kb)
                if masked:
                    r = lax.broadcasted_iota(jnp.int32, (tq, tq), 0)
                    cc = lax.broadcasted_iota(jnp.int32, (tq, tq), 1)
                    s = jnp.where(cc <= r, s, NEG)
                m_new = jnp.maximum(m, jnp.max(s, axis=-1, keepdims=True))
                corr = jnp.exp(m - m_new)
                p = jnp.exp(s - m_new)
                l = l * corr + jnp.sum(p, axis=-1, keepdims=True)
                acc = acc * corr + _dot(p.astype(BF16), vb)
                new += [m_new, l, acc]
            return tuple(new)

        init = (jnp.full((tq, 1), NEG, F32), jnp.zeros((tq, 1), F32), jnp.zeros((tq, HEAD_DIM), F32)) * 2
        carry = lax.fori_loop(0, i, functools.partial(blk, masked=False), init)
        carry = blk(i, carry, True)
        outs += [carry[2] / carry[1], carry[5] / carry[4]]
    o_ref[...] = jnp.concatenate(outs, axis=1)


def mla_attn_prompt(q, k, v, b, t):
    tq = _tile(t, 512)
    nq = t // tq
    return pl.pallas_call(
        functools.partial(_mla_attn_prompt_kernel, tq=tq),
        grid=(b, nq),
        in_specs=[pl.BlockSpec((tq, 512), lambda bi, i: (bi * nq + i, 0)),
                  pl.BlockSpec((t, 512), lambda bi, i: (bi, 0)),
                  pl.BlockSpec((t, 256), lambda bi, i: (bi, 0))],
        out_specs=pl.BlockSpec((tq, 256), lambda bi, i: (bi * nq + i, 0)),
        out_shape=jax.ShapeDtypeStruct((b * t, 256), F32),
        compiler_params=_cparams("parallel", "arbitrary"),
        name="mla_attn_prompt",
    )(q, k, v)


N_SUB = 8


def _mla_attn_sample_kernel(pt_ref, qbd_ref, qpe_ref, ckvn_ref, kpen_ref, wuk_ref, wukt_ref, wuv_ref, ckv_hbm,
                            kpe_hbm, o_ref, ckv_buf, kpe_buf, sem, m_scr, l_scr, acc_scr, lhs_scr, *, n_pg, tq,
                            layer):
    b_id = pl.program_id(0)
    p_id = pl.program_id(1)
    n_steps = pl.num_programs(1)
    step = b_id * n_steps + p_id
    slot = step % 2

    def page_copies(b, p, sl):
        copies = []
        for k in range(n_pg):
            phys = pt_ref[b, p * n_pg + k]
            copies.append(pltpu.make_async_copy(ckv_hbm.at[phys, layer], ckv_buf.at[sl, pl.ds(k * PAGE, PAGE), :],
                                                sem.at[sl, 0]))
            copies.append(pltpu.make_async_copy(kpe_hbm.at[phys, layer], kpe_buf.at[sl, :, pl.ds(k * PAGE, PAGE)],
                                                sem.at[sl, 1]))
        return copies

    @pl.when(step == 0)
    def _():
        for c in page_copies(b_id, p_id, slot):
            c.start()

    @pl.when(step + 1 < pl.num_programs(0) * n_steps)
    def _():
        wrap = p_id + 1 == n_steps
        for c in page_copies(jnp.where(wrap, b_id + 1, b_id), jnp.where(wrap, 0, p_id + 1), 1 - slot):
            c.start()

    qpe = qpe_ref[0]
    nr = N_HEADS * tq

    @pl.when(p_id == 0)
    def _():
        lhs_scr[0:GROUP_W, :] = wukt_ref[...]
        lhs_scr[GROUP_W:GROUP_W + nr, :] = _dot_nt(qbd_ref[0], wuk_ref[...]).astype(BF16)

    def scores(ckv, kpe_t):
        ckv_b = ckv.astype(BF16)
        out = _dot_nt(lhs_scr[...], ckv_b)
        rinv = []
        for h in range(N_HEADS):
            kn = out[h * HEAD_DIM:(h + 1) * HEAD_DIM]
            ss = jnp.sum(kn * kn, axis=0, keepdims=True) * (1.0 / HEAD_DIM)
            rinv.append(jnp.broadcast_to(lax.rsqrt(ss + EPS), (tq, ss.shape[1])))
        s = out[GROUP_W:GROUP_W + nr] * jnp.concatenate(rinv, axis=0) + _dot(qpe, kpe_t.astype(BF16))
        return s, ckv_b

    @pl.when(p_id == 0)
    def _():
        s, ckv_b = scores(ckvn_ref[0], kpen_ref[0])
        r = lax.broadcasted_iota(jnp.int32, s.shape, 0) % tq
        cc = lax.broadcasted_iota(jnp.int32, s.shape, 1)
        s = jnp.where(cc <= r, s, NEG)
        m = jnp.max(s, axis=-1, keepdims=True)
        p = jnp.exp(s - m)
        m_scr[...] = m
        l_scr[...] = jnp.sum(p, axis=-1, keepdims=True)
        acc_scr[...] = _dot(p.astype(BF16), ckv_b)

    for c in page_copies(b_id, p_id, slot):
        c.wait()

    n_sub = min(N_SUB, n_pg)
    keys = (n_pg // n_sub) * PAGE
    s_parts, ckv_parts = [], []
    for g in range(n_sub):
        s, ckv_b = scores(ckv_buf[slot, g * keys:(g + 1) * keys, :], kpe_buf[slot, :, g * keys:(g + 1) * keys])
        s_parts.append(s)
        ckv_parts.append(ckv_b)
    m = m_scr[...]
    m_new = m
    for s in s_parts:
        m_new = jnp.maximum(m_new, jnp.max(s, axis=-1, keepdims=True))
    corr = jnp.exp(m - m_new)
    l = l_scr[...] * corr
    acc = acc_scr[...] * corr
    for s, ckv_b in zip(s_parts, ckv_parts):
        p = jnp.exp(s - m_new)
        l = l + jnp.sum(p, axis=-1, keepdims=True)
        acc = acc + _dot(p.astype(BF16), ckv_b)
    m_scr[...] = m_new
    l_scr[...] = l
    acc_scr[...] = acc

    @pl.when(p_id == n_steps - 1)
    def _():
        lat = (acc / l).astype(BF16)
        full = _dot(lat, wuv_ref[...])
        lane_head = lax.broadcasted_iota(jnp.int32, (tq, 256), 1) // HEAD_DIM
        out = jnp.zeros((tq, 256), F32)
        for h in range(N_HEADS):
            out = jnp.where(lane_head == h, full[h * tq:(h + 1) * tq, :], out)
        o_ref[0] = out


def mla_attn_sample(page_table, qbd, qpe, ckv_new, kpe_new_t, wuk, wukt, wuv, cache_ckv, cache_kpe_t, layer):
    bs, n_pages = page_table.shape
    tq = qbd.shape[1] // N_HEADS
    n_pg = min(PAGES_PER_STEP, n_pages)
    assert n_pages % n_pg == 0
    nr = N_HEADS * tq
    per_b = lambda shp: pl.BlockSpec((1,) + shp, lambda b, p, pt: (b, 0, 0))
    cst = lambda shp: pl.BlockSpec(shp, lambda b, p, pt: (0,) * len(shp))
    hbm = pl.BlockSpec(memory_space=pl.ANY)
    grid_spec = pltpu.PrefetchScalarGridSpec(
        num_scalar_prefetch=1,
        grid=(bs, n_pages // n_pg),
        in_specs=[per_b((nr, 256)), per_b((nr, QK_ROPE)), per_b((PAGE, KV_LORA)), per_b((QK_ROPE, PAGE)),
                  cst(wuk.shape), cst(wukt.shape), cst(wuv.shape), hbm, hbm],
        out_specs=pl.BlockSpec((1, tq, 256), lambda b, p, pt: (b, 0, 0)),
        scratch_shapes=[pltpu.VMEM((2, n_pg * PAGE, KV_LORA), F32), pltpu.VMEM((2, QK_ROPE, n_pg * PAGE), F32),
                        pltpu.SemaphoreType.DMA((2, 2)),
                        pltpu.VMEM((nr, 1), F32), pltpu.VMEM((nr, 1), F32), pltpu.VMEM((nr, KV_LORA), F32),
                        pltpu.VMEM((GROUP_W + nr, KV_LORA), BF16)],
    )
    return pl.pallas_call(
        functools.partial(_mla_attn_sample_kernel, n_pg=n_pg, tq=tq, layer=layer),
        grid_spec=grid_spec,
        out_shape=jax.ShapeDtypeStruct((bs, tq, 256), F32),
        compiler_params=_cparams("arbitrary", "arbitrary"),
        name="mla_attn_sample",
    )(page_table, qbd, qpe, ckv_new, kpe_new_t, wuk, wukt, wuv, cache_ckv, cache_kpe_t)


CONV_HALO = 32


def _conv_kernel(pb_ref, st_ref, cw_ref, cb_ref, e_ref, g_ref, b_ref, pw_ref, o_ref, st_out_ref, xbuf, *, tt):
    j = pl.program_id(1)
    lo = CONV_HALO - (CONV_W - 1)

    @pl.when(j == 0)
    def _():
        xbuf[pl.ds(lo, CONV_W - 1), :] = st_ref[0]

    pb = pb_ref[...]
    xbuf[pl.ds(CONV_HALO, tt), :] = pb[:, 0:256] * _sigmoid(pb[:, 256:512])
    y = jnp.zeros((tt, 256), F32) + cb_ref[...]
    for k in range(CONV_W):
        y = y + xbuf[pl.ds(lo + k, tt), :] * cw_ref[pl.ds(k, 1), :]
    new_state = xbuf[pl.ds(lo + tt, CONV_W - 1), :]
    xbuf[pl.ds(lo, CONV_W - 1), :] = new_state
    st_out_ref[0] = new_state
    e = e_ref[...]
    yc = y - _seg_sum(y, e)
    yn = yc * lax.rsqrt(_seg_sum(yc * yc, e) + LN_EPS) * g_ref[...] + b_ref[...]
    act = yn * _sigmoid(yn)
    o_ref[...] = _dot(act.astype(BF16), pw_ref[...])


def conv_module(pb, state, wts, b, t):
    tt = _tile(t, 256)
    nt = t // tt
    return pl.pallas_call(
        functools.partial(_conv_kernel, tt=tt),
        grid=(b, nt),
        in_specs=[pl.BlockSpec((tt, 512), lambda bi, j: (bi * nt + j, 0)),
                  pl.BlockSpec((1, CONV_W - 1, 256), lambda bi, j: (bi, 0, 0))]
                 + [_full(w.shape) for w in wts],
        out_specs=[pl.BlockSpec((tt, 256), lambda bi, j: (bi * nt + j, 0)),
                   pl.BlockSpec((1, CONV_W - 1, 256), lambda bi, j: (bi, 0, 0))],
        out_shape=[jax.ShapeDtypeStruct((b * t, 256), F32), jax.ShapeDtypeStruct((b, CONV_W - 1, 256), F32)],
        scratch_shapes=[pltpu.VMEM((CONV_HALO + tt, 256), F32)],
        compiler_params=_cparams("parallel", "arbitrary"),
        name="conv_module",
    )(pb, state, *wts)


def _sgu_kernel(pc_ref, g_ref, b_ref, w_ref, bias_ref, o_ref, v_ref, *, n_chunks):
    x = pc_ref[...]
    z = 0.5 * x * (1.0 + jnp.tanh(0.7978845608028654 * (x + 0.044715 * (x * x * x))))
    u = z[:, 0:256]
    v = z[:, 256:512]
    vc = v - jnp.mean(v, axis=-1, keepdims=True)
    v = vc * lax.rsqrt(jnp.mean(vc * vc, axis=-1, keepdims=True) + LN_EPS) * g_ref[...] + b_ref[...]
    v_ref[...] = v
    lane_head = lax.broadcasted_iota(jnp.int32, (CHUNK, 256), 1) // HEAD_DIM
    for c in range(n_chunks):
        vcb = v[c * CHUNK:(c + 1) * CHUNK, :]
        sv = bias_ref[...]
        for h in range(N_HEADS):
            sv = sv + _dot(w_ref[h], jnp.where(lane_head == h, vcb, 0.0).astype(BF16))
        o_ref[pl.ds(c * CHUNK, CHUNK), :] = u[c * CHUNK:(c + 1) * CHUNK, :] * sv


def sgu(pc, wts):
    n = pc.shape[0]
    tm = _tile(n, 512)
    row = lambda wd: pl.BlockSpec((tm, wd), lambda i: (i, 0))
    return pl.pallas_call(
        functools.partial(_sgu_kernel, n_chunks=tm // CHUNK),
        grid=(n // tm,),
        in_specs=[row(512)] + [_full(w.shape) for w in wts],
        out_specs=[row(256), row(256)],
        out_shape=[jax.ShapeDtypeStruct((n, 256), F32), jax.ShapeDtypeStruct((n, 256), F32)],
        compiler_params=_cparams("parallel"),
        name="sgu",
    )(pc, *wts)


RW_HALO = 8


def _rwkv_prep_kernel(pd_ref, sh_ref, mu_ref, w0_ref, w2_ref, a0_ref, a2_ref, g2_ref, kkp_ref, ka_ref, rk_ref,
                      e_ref, r_ref, w_ref, k_ref, v_ref, kk_ref, kka_ref, g_ref, bv_ref, xbuf, *, tt):
    j = pl.program_id(1)

    @pl.when(j == 0)
    def _():
        xbuf[pl.ds(RW_HALO - 1, 1), :] = sh_ref[0]

    pd = pd_ref[...]
    xbuf[pl.ds(RW_HALO, tt), :] = pd
    prev = xbuf[pl.ds(RW_HALO - 1, tt), :]
    xbuf[pl.ds(RW_HALO - 1, 1), :] = pd[tt - 1:tt, :]
    xs = pd + (prev - pd) * mu_ref[...]
    r = xs[:, 0:256]
    k = xs[:, 256:512]
    v = xs[:, 512:768]
    xwa = xs[:, 768:896]
    xg = xs[:, 896:1024]
    z = -(w0_ref[...] + _dot(jnp.tanh(xwa).astype(BF16), w2_ref[...]))
    softplus = jnp.maximum(z, 0.0) + jnp.log(1.0 + jnp.exp(-jnp.abs(z)))
    w_ref[...] = jnp.exp(-jnp.exp(-softplus - 0.5))
    a = _sigmoid(a0_ref[...] + _dot(xwa.astype(BF16), a2_ref[...]))
    g_ref[...] = _dot(_sigmoid(xg).astype(BF16), g2_ref[...])
    e = e_ref[...]
    kk = k * kkp_ref[...]
    kk = kk * lax.rsqrt(_seg_sum(kk * kk, e) + 1e-12)
    k2 = k * (1.0 + (a - 1.0) * ka_ref[...])
    r_ref[...] = r
    k_ref[...] = k2
    v_ref[...] = v
    kk_ref[...] = kk
    kka_ref[...] = kk * a
    bv_ref[...] = _seg_sum(r * k2 * rk_ref[...], e) * v


def rwkv_prep(pd, shift, wts, b, t):
    tt = _tile(t, 256)
    nt = t // tt
    row = lambda wd: pl.BlockSpec((tt, wd), lambda bi, j: (bi * nt + j, 0))
    return pl.pallas_call(
        functools.partial(_rwkv_prep_kernel, tt=tt),
        grid=(b, nt),
        in_specs=[row(PD_W), pl.BlockSpec((1, 1, PD_W), lambda bi, j: (bi, 0, 0))]
                 + [_full(w.shape) for w in wts],
        out_specs=[row(256)] * 8,
        out_shape=[jax.ShapeDtypeStruct((b * t, 256), F32)] * 8,
        scratch_shapes=[pltpu.VMEM((RW_HALO + tt, PD_W), F32)],
        compiler_params=_cparams("parallel", "arbitrary"),
        name="rwkv_prep",
    )(pd, shift, *wts)


RW_BLOCK = 128


def _pack_bf16_pair(a, b):
    ua = lax.bitcast_convert_type(a.astype(BF16).astype(F32), jnp.uint32)
    ub = lax.bitcast_convert_type(b.astype(BF16).astype(F32), jnp.uint32)
    return lax.bitcast_convert_type(ua | (ub >> 16), jnp.int32)


def _unpack_bf16_pair(word):
    u = lax.bitcast_convert_type(word, jnp.uint32)
    return (lax.bitcast_convert_type(u & jnp.uint32(0xFFFF0000), F32),
            lax.bitcast_convert_type(u << 16, F32))


def _rwkv_scan_kernel(w_ref, kk_ref, kka_ref, k_ref, r_ref, v_ref, s0_ref, e_ref, y_ref, sf_ref, s_scr, col_scr, *,
                      nb, sblk):
    c = pl.program_id(1)

    @pl.when(c == 0)
    def _():
        s_scr[...] = s0_ref[...]

    lane = lax.broadcasted_iota(jnp.int32, (HEAD_DIM, LANES), 1)
    low = lane < HEAD_DIM
    n_half = 2 if sblk > HEAD_DIM else 1
    for b in range(nb):
        for p in range(2):
            blk = lambda ref: ref[b, :, p * LANES:(p + 1) * LANES]
            x = _pack_bf16_pair(blk(kk_ref), blk(kka_ref))
            if sblk < LANES:
                x = jnp.concatenate([x, jnp.zeros((LANES - sblk, LANES), jnp.int32)], axis=0)
            xt = x.T
            h0, h1 = xt[0:HEAD_DIM], xt[HEAD_DIM:LANES]
            col_scr[b, p, 0] = jnp.where(low, h0, pltpu.roll(h1, HEAD_DIM, 1))
            if n_half == 2:
                col_scr[b, p, 1] = jnp.where(low, pltpu.roll(h0, HEAD_DIM, 1), h1)

    base = jnp.where(low, 0, HEAD_DIM)
    diag = (lax.broadcasted_iota(jnp.int32, (HEAD_DIM, 256), 1) % HEAD_DIM
            == lax.broadcasted_iota(jnp.int32, (HEAD_DIM, 256), 0))
    for half in range(n_half):
        n_groups = min(sblk - half * HEAD_DIM, HEAD_DIM) // SUBLANES

        def group(gi, carry, half=half):
            row0 = pl.multiple_of(half * HEAD_DIM + gi * SUBLANES, SUBLANES)
            for b in range(nb):
                v8 = v_ref[b, pl.ds(row0, SUBLANES), :]
                rows8 = (k_ref[b, pl.ds(row0, SUBLANES), :], r_ref[b, pl.ds(row0, SUBLANES), :],
                         1.0 - w_ref[b, pl.ds(row0, SUBLANES), :])
                on_diag = [jnp.where(diag, jnp.broadcast_to(x8[i:i + 1, :], (HEAD_DIM, 256)), 0.0).astype(BF16)
                           for i in range(SUBLANES) for x8 in rows8]
                spread = _dot(jnp.concatenate(on_diag, axis=0), e_ref[...])
                ys = []
                for p in range(2):
                    st = s_scr[b, :, p * LANES:(p + 1) * LANES]
                    yp = []
                    for i in range(SUBLANES):
                        idx = base + (gi * SUBLANES + i)
                        kk, kka = _unpack_bf16_pair(jnp.take_along_axis(col_scr[b, p, half], idx, axis=1))
                        part = lambda n: spread[(3 * i + n) * HEAD_DIM:(3 * i + n + 1) * HEAD_DIM,
                                                p * LANES:(p + 1) * LANES]
                        k, r, w = part(0), part(1), 1.0 - part(2)
                        vrow = v8[i:i + 1, p * LANES:(p + 1) * LANES]
                        sa = -jnp.sum(st * kk, axis=0, keepdims=True)
                        st = st * w + kka * sa + k * vrow
                        yp.append(jnp.sum(st * r, axis=0, keepdims=True))
                    s_scr[b, :, p * LANES:(p + 1) * LANES] = st
                    ys.append(jnp.concatenate(yp, axis=0))
                y_ref[b, pl.ds(row0, SUBLANES), :] = jnp.concatenate(ys, axis=1)
            return carry

        lax.fori_loop(0, n_groups, group, 0)

    @pl.when(c == pl.num_programs(1) - 1)
    def _():
        sf_ref[...] = s_scr[...]


def rwkv_scan(seqs, v, s0, nb):
    b, t, _ = v.shape
    e_heads = _head_matrix(1.0)
    sblk = min(t, RW_BLOCK)
    assert b % nb == 0 and t % sblk == 0 and sblk % SUBLANES == 0 and (sblk <= HEAD_DIM or sblk == RW_BLOCK)
    st_spec = pl.BlockSpec((nb, HEAD_DIM, 256), lambda bi, c: (bi, 0, 0))
    seq_spec = pl.BlockSpec((nb, sblk, 256), lambda bi, c: (bi, c, 0))
    return pl.pallas_call(
        functools.partial(_rwkv_scan_kernel, nb=nb, sblk=sblk),
        grid=(b // nb, t // sblk),
        in_specs=[seq_spec] * 6 + [st_spec, _full(e_heads.shape)],
        out_specs=[seq_spec, st_spec],
        out_shape=[jax.ShapeDtypeStruct((b, t, 256), F32), jax.ShapeDtypeStruct((b, HEAD_DIM, 256), F32)],
        scratch_shapes=[pltpu.VMEM((nb, HEAD_DIM, 256), F32),
                        pltpu.VMEM((nb, 2, 2, HEAD_DIM, LANES), jnp.int32)],
        compiler_params=_cparams("parallel", "arbitrary"),
        name="rwkv_scan",
    )(*seqs, v, s0, e_heads)


def _mix_out_kernel(x_ref, oa_ref, ob_ref, oc_ref, y_ref, bv_ref, g_ref, e_ref, lg_ref, lb_ref, on_ref, w_ref,
                    nx_ref, wq_ref, qg_ref, o_ref, q_ref):
    e = e_ref[...]
    y = y_ref[...]
    yc = y - _seg_sum(y, e)
    yn = yc * lax.rsqrt(_seg_sum(yc * yc, e) + RW_LN_EPS) * lg_ref[...] + lb_ref[...]
    od = (yn + bv_ref[...]) * g_ref[...]
    acc = x_ref[...]
    for gi, o in enumerate((oa_ref[...], ob_ref[...], oc_ref[...], od)):
        on = (_rms(o) * on_ref[:, gi * 256:(gi + 1) * 256]).astype(BF16)
        acc = acc + _dot(on, w_ref[pl.ds(gi * 256, 256), :])
    o_ref[...] = acc
    q = _dot((_rms(acc) * nx_ref[...]).astype(BF16), wq_ref[...])
    heads = [_rms(q[:, h * X_HEAD_DIM:(h + 1) * X_HEAD_DIM]) * qg_ref[...] * X_HEAD_DIM ** -0.5
             for h in range(X_HEADS)]
    q_ref[...] = jnp.concatenate(heads, axis=1).astype(q_ref.dtype)


def mix_out(x, oa, ob, oc, y, bv, g, wts, q_dtype):
    n, d = x.shape
    tm = _tile(n, 512)
    row = lambda wd: pl.BlockSpec((tm, wd), lambda i: (i, 0))
    return pl.pallas_call(
        _mix_out_kernel,
        grid=(n // tm,),
        in_specs=[row(d)] + [row(256)] * 6 + [_full(w.shape) for w in wts],
        out_specs=[row(d), row(512)],
        out_shape=[jax.ShapeDtypeStruct((n, d), F32), jax.ShapeDtypeStruct((n, 512), q_dtype)],
        compiler_params=_cparams("parallel"),
        name="mix_out",
    )(x, oa, ob, oc, y, bv, g, *wts)


XATTN_SHORT_ROWS = 64


def _xattn_kernel(x_ref, q_ref, k_ref, v_ref, wo_ref, o_ref, *, n_seq, rows, n_mem):
    seq_outs = []
    for s in range(n_seq):
        qs = [q_ref[s * rows:(s + 1) * rows, h * X_HEAD_DIM:(h + 1) * X_HEAD_DIM].astype(BF16)
              for h in range(X_HEADS)]
        if n_seq == 1:
            outs = []
            for h in range(X_HEADS):
                head_rows = pl.ds(h, n_mem, stride=X_HEADS)
                sc = _dot_nt(qs[h], k_ref[s, head_rows, :].astype(BF16))
                p = jnp.exp(sc - jnp.max(sc, axis=-1, keepdims=True))
                p = p / jnp.sum(p, axis=-1, keepdims=True)
                outs.append(_dot(p.astype(BF16), v_ref[s, head_rows, :].astype(BF16)))
        else:
            sc = _dot_nt(jnp.concatenate(qs, axis=0), k_ref[s].astype(BF16))
            row_head = lax.broadcasted_iota(jnp.int32, sc.shape, 0) // rows
            col_head = lax.broadcasted_iota(jnp.int32, sc.shape, 1) % X_HEADS
            sc = jnp.where(row_head == col_head, sc, NEG)
            p = jnp.exp(sc - jnp.max(sc, axis=-1, keepdims=True))
            p = p / jnp.sum(p, axis=-1, keepdims=True)
            o = _dot(p.astype(BF16), v_ref[s].astype(BF16))
            outs = [o[h * rows:(h + 1) * rows] for h in range(X_HEADS)]
        seq_outs.append(jnp.concatenate(outs, axis=1))
    xo = jnp.concatenate(seq_outs, axis=0).astype(BF16)
    o_ref[...] = x_ref[...] + _dot(xo, wo_ref[...])


def xattn(x, q, mem_k, mem_v, wo, b, t, kv_seq0):
    n, d = x.shape
    n_mem = mem_k.shape[1] // X_HEADS
    if t >= XATTN_SHORT_ROWS:
        n_seq, rows = 1, _tile(t, 512)
    else:
        n_seq, rows = _tile(b, XATTN_SHORT_ROWS // t), t
    nq = t // rows
    assert kv_seq0 % n_seq == 0
    row = lambda wd: pl.BlockSpec((n_seq * rows, wd), lambda bi, i: (bi * nq + i, 0))
    kv = pl.BlockSpec((n_seq, n_mem * X_HEADS, X_HEAD_DIM), lambda bi, i: (kv_seq0 // n_seq + bi, 0, 0))
    return pl.pallas_call(
        functools.partial(_xattn_kernel, n_seq=n_seq, rows=rows, n_mem=n_mem),
        grid=(b // n_seq, nq),
        in_specs=[row(d), row(512), kv, kv, _full(wo.shape)],
        out_specs=row(d),
        out_shape=jax.ShapeDtypeStruct((n, d), F32),
        compiler_params=_cparams("parallel", "arbitrary"),
        name="xattn",
    )(x, q, mem_k, mem_v, wo)


def _mem_kv_kernel(x_ref, g_ref, w_ref, hg_ref, k_ref, v_ref, *, tm):
    hb = (_rms(x_ref[...]) * g_ref[...]).astype(BF16)
    y = _dot(hb, w_ref[...])
    for h in range(X_HEADS):
        head_rows = pl.ds(h, tm, stride=X_HEADS)
        k_ref[head_rows, :] = _rms(y[:, h * X_HEAD_DIM:(h + 1) * X_HEAD_DIM]) * hg_ref[...]
        v_ref[head_rows, :] = y[:, (X_HEADS + h) * X_HEAD_DIM:(X_HEADS + h + 1) * X_HEAD_DIM]


def mem_kv(x, g, w, hg):
    n, d = x.shape
    tm = _tile(n, 512)
    out = pl.BlockSpec((tm * X_HEADS, X_HEAD_DIM), lambda i: (i, 0))
    return pl.pallas_call(
        functools.partial(_mem_kv_kernel, tm=tm),
        grid=(n // tm,),
        in_specs=[pl.BlockSpec((tm, d), lambda i: (i, 0)), _full(g.shape), _full(w.shape), _full(hg.shape)],
        out_specs=[out, out],
        out_shape=[jax.ShapeDtypeStruct((n * X_HEADS, X_HEAD_DIM), F32)] * 2,
        compiler_params=_cparams("parallel"),
        name="mem_kv",
    )(x, g, w, hg)


FFN_CHUNK = 256


def _ffn_kernel(x_ref, g_ref, wg_ref, wu_ref, wo_ref, o_ref, *, n_chunks):
    x = x_ref[...]
    hb = (_rms(x) * g_ref[...]).astype(BF16)
    acc = x
    for c in range(n_chunks):
        a = _dot(hb, wg_ref[c])
        u = _dot(hb, wu_ref[c])
        acc = acc + _dot((a * _sigmoid(a) * u).astype(BF16), wo_ref[c])
    o_ref[...] = acc


def ffn(x, g, wg, wu, wo):
    n, d = x.shape
    tm = _tile(n, 512)
    row = pl.BlockSpec((tm, d), lambda i: (i, 0))
    return pl.pallas_call(
        functools.partial(_ffn_kernel, n_chunks=wg.shape[0]),
        grid=(n // tm,),
        in_specs=[row, _full(g.shape), _full(wg.shape), _full(wu.shape), _full(wo.shape)],
        out_specs=row,
        out_shape=jax.ShapeDtypeStruct((n, d), F32),
        compiler_params=_cparams("parallel"),
        name="ffn",
    )(x, g, wg, wu, wo)


def _seg_matrix(seg_ids, seg_len):
    s = np.asarray(seg_ids)
    m = (s[:, None] == s[None, :]) & (s[:, None] >= 0)
    return jnp.asarray(m.astype(np.float32) / np.asarray(seg_len, np.float32)[None, :], BF16)


def _mla_segments():
    lane = np.arange(512)
    blk, off = lane // 128, lane % 128
    q_ids = np.where(off < 32, 2 * blk, np.where(off < 64, -1, 2 * blk + 1))
    q_len = np.where(off < 32, 32.0, 64.0)
    k_ids = np.where(off < 64, -1, blk)
    k_len = np.full(512, 64.0)
    return _seg_matrix(q_ids, q_len), _seg_matrix(k_ids, k_len)


def _head_matrix(scale_len):
    lane = np.arange(256)
    return _seg_matrix(lane // HEAD_DIM, np.full(256, scale_len))


def _rope_tables(pos):
    half = QK_ROPE // 2
    inv = jnp.power(ROPE_BASE, -jnp.arange(half, dtype=F32) / half)
    ang = pos.astype(F32)[:, None] * inv[None, :]
    cos, sin = jnp.cos(ang), jnp.sin(ang)
    n = pos.shape[0]
    z = lambda w: jnp.zeros((n, w), F32)
    c = jnp.concatenate([cos, cos, jnp.ones((n, LANES - QK_ROPE), F32)], axis=1)
    sa = jnp.concatenate([-sin, z(LANES - half)], axis=1)
    sb = jnp.concatenate([z(half), sin, z(LANES - QK_ROPE)], axis=1)
    return c, sa, sb


def _layer_weights(p):
    row = lambda v: v.reshape(1, -1).astype(F32)
    zc = lambda a, w: jnp.zeros((a.shape[0], w), a.dtype)
    w_in = p['w_in']
    c1, c2, c3 = 352, 352 + 512, 352 + 1024
    pa = w_in[:, :c1]
    w_in_p = jnp.concatenate(
        [pa[:, :Q_LORA], zc(pa, 64), pa[:, Q_LORA:Q_LORA + KV_LORA], pa[:, Q_LORA + KV_LORA:], zc(pa, 96),
         w_in[:, c1:c2], w_in[:, c2:c3], w_in[:, c3:]], axis=1).astype(BF16)

    wuq = p['mla_w_uq'].reshape(Q_LORA, N_HEADS, QK_NOPE + QK_ROPE)
    wuq = jnp.concatenate([wuq[:, :, QK_NOPE:], jnp.zeros((Q_LORA, N_HEADS, 32), F32), wuq[:, :, :QK_NOPE]], axis=2)
    wuq = jnp.concatenate([wuq.reshape(Q_LORA, 512), jnp.zeros((256 - Q_LORA, 512), F32)], axis=0).astype(BF16)
    wuk = p['mla_w_uk'].reshape(KV_LORA, N_HEADS, QK_NOPE)
    wuk_p = jnp.concatenate([jnp.zeros((KV_LORA, N_HEADS, 64), F32), wuk], axis=2).reshape(KV_LORA, 512).astype(BF16)
    blk = lambda a, b_, c_: jnp.tile(jnp.concatenate([a, b_, c_]), N_HEADS).reshape(1, 512)
    z32, z64 = jnp.zeros((32,), F32), jnp.zeros((64,), F32)
    gq = blk(p['mla_gq_rope'], z32, p['mla_gq_nope'])
    gk = blk(z32, z32, p['mla_gk_nope'])
    gqn = jnp.concatenate([p['mla_q_norm'], z64]).reshape(1, 256)
    gkr = jnp.concatenate([p['mla_gk_rope'], jnp.zeros((96,), F32)]).reshape(1, LANES)
    eq, ek = _mla_segments()
    mla = (gqn, row(p['mla_kv_norm']), wuq, eq, gq, gkr, wuk_p, ek, gk, p['mla_w_uv'].astype(BF16))

    e64 = _head_matrix(64.0)
    conv = (p['conv_w'], row(p['conv_b']), e64, row(p['conv_norm_g']), row(p['conv_norm_b']),
            p['conv_pw'].astype(BF16))

    z64r = jnp.zeros((64, GROUP_W), F32)
    rw = (row(p['rw_mu']),
          row(p['rw_w0']), jnp.concatenate([p['rw_w2'], z64r], axis=0).astype(BF16),
          row(p['rw_a0']), jnp.concatenate([z64r, p['rw_a2']], axis=0).astype(BF16),
          p['rw_g2'].astype(BF16), row(p['rw_kk']), row(p['rw_ka']), row(p['rw_rk']), _head_matrix(1.0))

    mix = (e64, row(p['rw_ln_g']), row(p['rw_ln_b']), row(p['out_norm']), p['w_out'].astype(BF16))

    d_ff = p['w_ffn_out'].shape[0]
    nck = d_ff // FFN_CHUNK
    d = w_in.shape[0]
    wg = p['w_ffn_in'][:, :d_ff].reshape(d, nck, FFN_CHUNK).transpose(1, 0, 2).astype(BF16)
    wu = p['w_ffn_in'][:, d_ff:].reshape(d, nck, FFN_CHUNK).transpose(1, 0, 2).astype(BF16)
    wo = p['w_ffn_out'].reshape(nck, FFN_CHUNK, d).astype(BF16)

    return dict(
        norm_mix=row(p['norm_mix']), w_in=w_in_p, mla=mla, conv=conv, rw=rw, mix=mix,
        wuk=p['mla_w_uk'].astype(BF16), wuv=p['mla_w_uv'].astype(BF16), gk_nope=p['mla_gk_nope'],
        sgu_ln=(row(p['sgu_norm_g']), row(p['sgu_norm_b'])), sgu_w=p['sgu_w'], sgu_b=p['sgu_b'],
        norm_x=row(p['norm_x']), wq=p['wq_x'].astype(BF16), xq_norm=row(p['xq_norm']),
        mem_norm=row(p['mem_norm']), wkv=jnp.concatenate([p['wk_x'], p['wv_x']], axis=1).astype(BF16),
        xk_norm=row(p['xk_norm']), wo_x=p['wo_x'].astype(BF16),
        norm_ffn=row(p['norm_ffn']), wg=wg, wu=wu, wo=wo)


def _sgu_weights(lw, t):
    l = min(t, CHUNK)
    w = lw['sgu_w'][:, :l, :l] * jnp.tril(jnp.ones((l, l), F32))
    reps = CHUNK // l
    if reps > 1:
        w = jnp.einsum('ab,hij->haibj', jnp.eye(reps, dtype=F32), w).reshape(N_HEADS, CHUNK, CHUNK)
    bias = jnp.tile(lw['sgu_b'][:, :l].T, (reps, 1))
    bias = jnp.repeat(bias, HEAD_DIM, axis=1)
    return lw['sgu_ln'] + (w.astype(BF16), bias)


def _trunk_layer(x, b, t, lw, tabs, n_tab_blocks, attend, mem_k, mem_v, kv_seq0, conv_state, shift_state, wkv_state,
                 scan_nb):
    pa, pb, pc, pd = proj_in(x, lw['norm_mix'], lw['w_in'])
    q, k, v, ckv, kpe = mla_prep(pa, tabs, lw['mla'], n_tab_blocks)
    oa = attend(q, k, v, ckv, kpe)
    ob, conv_new = conv_module(pb, conv_state, lw['conv'], b, t)
    oc, v_sgu = sgu(pc, _sgu_weights(lw, t))
    r, w, k2, vv, kk, kka, g, bv = rwkv_prep(pd, shift_state.reshape(b, 1, PD_W), lw['rw'], b, t)
    seqs = [a.reshape(b, t, 256) for a in (w, kk, kka, k2, r)]
    s0 = wkv_state.transpose(0, 3, 1, 2).reshape(b, HEAD_DIM, 256)
    y, s_fin = rwkv_scan(seqs, vv.reshape(b, t, 256), s0, scan_nb)
    wkv_new = s_fin.reshape(b, HEAD_DIM, N_HEADS, HEAD_DIM).transpose(0, 2, 3, 1)
    shift_new = pd.reshape(b, t, PD_W)[:, -1]
    q_dtype = BF16 if t % 16 == 0 else F32
    x, qx = mix_out(x, oa, ob, oc, y.reshape(b * t, 256), bv, g,
                    lw['mix'] + (lw['norm_x'], lw['wq'], lw['xq_norm']), q_dtype)
    x = xattn(x, qx, mem_k, mem_v, lw['wo_x'], b, t, kv_seq0)
    x = ffn(x, lw['norm_ffn'], lw['wg'], lw['wu'], lw['wo'])
    return x, ckv, kpe, conv_new, shift_new, wkv_new, v_sgu


def kernel(x_prompt, x_sample, mem_prompt, cache_ckv, cache_kpe, cache_mem_k, cache_mem_v, state_conv, state_shift, state_wkv, page_table, norm_mix, w_in, mla_q_norm, mla_kv_norm, mla_w_uq, mla_w_uk, mla_w_uv, mla_gq_nope, mla_gq_rope, mla_gk_nope, mla_gk_rope, conv_w, conv_b, conv_norm_g, conv_norm_b, conv_pw, sgu_norm_g, sgu_norm_b, sgu_w, sgu_b, rw_mu, rw_w0, rw_w2, rw_a0, rw_a2, rw_g2, rw_kk, rw_ka, rw_rk, rw_ln_g, rw_ln_b, out_norm, w_out, norm_x, mem_norm, wq_x, wk_x, wv_x, xq_norm, xk_norm, wo_x, norm_ffn, w_ffn_in, w_ffn_out):
    params = dict(
        norm_mix=norm_mix, w_in=w_in, mla_q_norm=mla_q_norm, mla_kv_norm=mla_kv_norm, mla_w_uq=mla_w_uq,
        mla_w_uk=mla_w_uk, mla_w_uv=mla_w_uv, mla_gq_nope=mla_gq_nope, mla_gq_rope=mla_gq_rope,
        mla_gk_nope=mla_gk_nope, mla_gk_rope=mla_gk_rope, conv_w=conv_w, conv_b=conv_b, conv_norm_g=conv_norm_g,
        conv_norm_b=conv_norm_b, conv_pw=conv_pw, sgu_norm_g=sgu_norm_g, sgu_norm_b=sgu_norm_b, sgu_w=sgu_w,
        sgu_b=sgu_b, rw_mu=rw_mu, rw_w0=rw_w0, rw_w2=rw_w2, rw_a0=rw_a0, rw_a2=rw_a2, rw_g2=rw_g2, rw_kk=rw_kk,
        rw_ka=rw_ka, rw_rk=rw_rk, rw_ln_g=rw_ln_g, rw_ln_b=rw_ln_b, out_norm=out_norm, w_out=w_out, norm_x=norm_x,
        mem_norm=mem_norm, wq_x=wq_x, wk_x=wk_x, wv_x=wv_x, xq_norm=xq_norm, xk_norm=xk_norm, wo_x=wo_x,
        norm_ffn=norm_ffn, w_ffn_in=w_ffn_in, w_ffn_out=w_ffn_out)
    depth = w_in.shape[0]
    bp, tp, d = x_prompt.shape
    bs, ts, _ = x_sample.shape
    n_mem = mem_prompt.shape[1]
    n_pages = page_table.shape[1]
    past_len = n_pages * PAGE

    tm_p = _tile(tp, 512)
    tabs_p = _rope_tables(jnp.arange(tp, dtype=jnp.int32))
    tm_s = _tile(bs * ts, 512)
    tabs_s = tuple(jnp.tile(a, (tm_s // ts, 1)) for a in _rope_tables(past_len + jnp.arange(ts, dtype=jnp.int32)))
    cache_kpe_t = jnp.swapaxes(cache_kpe, 2, 3)
    cache_k_rows = cache_mem_k.reshape(depth * bs, n_mem * X_HEADS, X_HEAD_DIM)
    cache_v_rows = cache_mem_v.reshape(depth * bs, n_mem * X_HEADS, X_HEAD_DIM)
    y_p = x_prompt.reshape(bp * tp, d)
    y_s = x_sample.reshape(bs * ts, d)
    mem_flat = mem_prompt.reshape(bp * n_mem, d)
    zeros_conv = jnp.zeros((bp, CONV_W - 1, GROUP_W), F32)
    zeros_shift = jnp.zeros((bp, PD_W), F32)
    zeros_wkv = jnp.zeros((bp, N_HEADS, HEAD_DIM, HEAD_DIM), F32)
    outs_p, outs_s, memk_l, memv_l = [], [], [], []
    for l in range(depth):
        lw = _layer_weights({k_: v_[l] for k_, v_ in params.items()})

        mk, mv = mem_kv(mem_flat, lw['mem_norm'], lw['wkv'], lw['xk_norm'])
        memk_l.append(mk.reshape(bp, n_mem, X_HEADS, X_HEAD_DIM))
        memv_l.append(mv.reshape(bp, n_mem, X_HEADS, X_HEAD_DIM))

        def attend_p(q, k, v, ckv, kpe):
            return mla_attn_prompt(q, k, v, bp, tp)

        kv_rows = n_mem * X_HEADS
        res = _trunk_layer(y_p, bp, tp, lw, tabs_p, tp // tm_p, attend_p, mk.reshape(bp, kv_rows, X_HEAD_DIM),
                           mv.reshape(bp, kv_rows, X_HEAD_DIM), 0, zeros_conv, zeros_shift, zeros_wkv,
                           _tile(bp, SCAN_NB_PROMPT))
        y_p = res[0]
        outs_p.append(res[1:])

        def attend_s(q, k, v, ckv, kpe, l=l, lw=lw):
            qf = q.astype(F32).reshape(bs, ts, N_HEADS, LANES)
            qn = (qf[..., 64:] * lw['gk_nope']).transpose(0, 2, 1, 3)
            qbd = jnp.einsum('bhtj,hg->bhtgj', qn, jnp.eye(N_HEADS, dtype=F32)).reshape(bs, N_HEADS * ts, 256)
            qpe = qf[..., :QK_ROPE].transpose(0, 2, 1, 3).reshape(bs, N_HEADS * ts, QK_ROPE)
            pad = lambda a: jnp.pad(a.reshape(bs, ts, -1), ((0, 0), (0, PAGE - ts), (0, 0)))
            o = mla_attn_sample(page_table, qbd.astype(BF16), qpe.astype(BF16), pad(ckv),
                                pad(kpe).transpose(0, 2, 1), lw['wuk'], lw['wuk'].T, lw['wuv'], cache_ckv,
                                cache_kpe_t, l)
            return o.reshape(bs * ts, 256)

        res = _trunk_layer(y_s, bs, ts, lw, tabs_s, 1, attend_s, cache_k_rows, cache_v_rows, l * bs,
                           state_conv[l], state_shift[l], state_wkv[l], _tile(bs, SCAN_NB_SAMPLE))
        y_s = res[0]
        outs_s.append(res[1:])

    n_pp = tp // PAGE
    stack = lambda outs, i, ax: jnp.stack([o[i] for o in outs], axis=ax)
    ckv_prompt = stack(outs_p, 0, 0).reshape(depth, bp, n_pp, PAGE, KV_LORA).transpose(1, 2, 0, 3, 4)
    kpe_prompt = stack(outs_p, 1, 0).reshape(depth, bp, n_pp, PAGE, QK_ROPE).transpose(1, 2, 0, 3, 4)
    ckv_sample = stack(outs_s, 0, 0).reshape(depth, bs, ts, KV_LORA).transpose(1, 0, 2, 3)
    kpe_sample = stack(outs_s, 1, 0).reshape(depth, bs, ts, QK_ROPE).transpose(1, 0, 2, 3)
    return (y_p.reshape(bp, tp, d), y_s.reshape(bs, ts, d), ckv_prompt, kpe_prompt, ckv_sample, kpe_sample,
            jnp.stack(memk_l, 0), jnp.stack(memv_l, 0),
            stack(outs_p, 2, 0), stack(outs_s, 2, 0), stack(outs_p, 3, 0), stack(outs_s, 3, 0),
            stack(outs_p, 4, 0), stack(outs_s, 4, 0),
            stack(outs_s, 5, 0).reshape(depth, bs, ts, GROUP_W))
```

```python
import functools

import numpy as np
import jax
import jax.numpy as jnp
from jax import lax
from jax.experimental import pallas as pl
from jax.experimental.pallas import tpu as pltpu

F32 = jnp.float32
BF16 = jnp.bfloat16

EPS = 1e-6
LN_EPS = 1e-5
RW_LN_EPS = 64e-5
NEG = -1e30
ROPE_BASE = 10000.0

LANES = 128
SUBLANES = 8
VMEM_LIMIT_BYTES = 56 * 1024 * 1024

GROUP_W = 256
HEAD_DIM = 64
N_HEADS = 4
Q_LORA = 192
KV_LORA = 128
QK_ROPE = 32
QK_NOPE = 64
CONV_W = 31
CHUNK = 128
PAGE = 128
X_HEADS = 4
X_HEAD_DIM = 128
MLA_SCALE = (QK_NOPE + QK_ROPE) ** -0.5
PA_W = 512
PD_W = 1024
PAGES_PER_STEP = 32
SCAN_NB_PROMPT = 4
SCAN_NB_SAMPLE = 8


def _cparams(*sem):
    return pltpu.CompilerParams(dimension_semantics=sem, vmem_limit_bytes=VMEM_LIMIT_BYTES)


def _dot(a, b):
    return jnp.dot(a, b, preferred_element_type=F32)


def _dot_nt(a, b):
    return lax.dot_general(a, b, (((1,), (1,)), ((), ())), preferred_element_type=F32)


def _seg_sum(x, e):
    hi = x.astype(BF16)
    lo = (x - hi.astype(F32)).astype(BF16)
    return _dot(hi, e) + _dot(lo, e)


def _rms(x, width=None):
    w = x.shape[-1] if width is None else width
    return x * lax.rsqrt(jnp.sum(x * x, axis=-1, keepdims=True) * (1.0 / w) + EPS)


def _sigmoid(x):
    return 1.0 / (1.0 + jnp.exp(-x))


def _tile(n, pref):
    t = min(n, pref)
    while n % t:
        t //= 2
    return t


def _full(shape):
    nd = len(shape)
    return pl.BlockSpec(shape, lambda *a: (0,) * nd)


def _proj_in_kernel(x_ref, g_ref, w_ref, pa_ref, pb_ref, pc_ref, pd_ref):
    hb = (_rms(x_ref[...]) * g_ref[...]).astype(BF16)
    pa_ref[...] = _dot(hb, w_ref[:, 0:512])
    pb_ref[...] = _dot(hb, w_ref[:, 512:1024])
    pc_ref[...] = _dot(hb, w_ref[:, 1024:1536])
    pd_ref[...] = _dot(hb, w_ref[:, 1536:2560])


def proj_in(x, g, w):
    n, d = x.shape
    tm = _tile(n, 512)
    row = lambda wd: pl.BlockSpec((tm, wd), lambda i: (i, 0))
    return pl.pallas_call(
        _proj_in_kernel,
        grid=(n // tm,),
        in_specs=[row(d), _full(g.shape), _full(w.shape)],
        out_specs=[row(512), row(512), row(512), row(1024)],
        out_shape=[jax.ShapeDtypeStruct((n, wd), F32) for wd in (512, 512, 512, 1024)],
        compiler_params=_cparams("parallel"),
        name="proj_in",
    )(x, g, w)


def _rope128(x, c, sa, sb):
    w = x.shape[-1]
    return x * c + pltpu.roll(x, w - 16, 1) * sa + pltpu.roll(x, 16, 1) * sb


def _mla_prep_kernel(pa_ref, c_ref, sa_ref, sb_ref, gqn_ref, gkv_ref, wuq_ref, eq_ref, gq_ref,
                     gkr_ref, wuk_ref, ek_ref, gk_ref, wuv_ref,
                     q_ref, k_ref, v_ref, ckv_ref, kpe_ref):
    pa = pa_ref[...]
    c, sa, sb = c_ref[...], sa_ref[...], sb_ref[...]
    c4 = jnp.concatenate([c] * 4, axis=1)
    sa4 = jnp.concatenate([sa] * 4, axis=1)
    sb4 = jnp.concatenate([sb] * 4, axis=1)
    cq = _rms(pa[:, 0:256], Q_LORA) * gqn_ref[...]
    q = _dot(cq.astype(BF16), wuq_ref[...])
    qn = q * lax.rsqrt(_seg_sum(q * q, eq_ref[...]) + EPS) * gq_ref[...]
    q_ref[...] = (_rope128(qn, c4, sa4, sb4) * MLA_SCALE).astype(BF16)
    ckv = _rms(pa[:, 256:384]) * gkv_ref[...]
    ckv_ref[...] = ckv
    kp = _rms(pa[:, 384:512], QK_ROPE) * gkr_ref[...]
    kr = _rope128(kp, c, sa, sb)
    kpe_ref[...] = kr[:, 0:QK_ROPE]
    ckv_b = ckv.astype(BF16)
    kn = _dot(ckv_b, wuk_ref[...])
    kn = kn * lax.rsqrt(_seg_sum(kn * kn, ek_ref[...]) + EPS) * gk_ref[...]
    k_ref[...] = (kn + jnp.concatenate([kr] * 4, axis=1)).astype(BF16)
    v_ref[...] = _dot(ckv_b, wuv_ref[...]).astype(BF16)


def mla_prep(pa, tabs, wts, n_tab_blocks):
    n = pa.shape[0]
    c, sa, sb = tabs
    tm = c.shape[0] // n_tab_blocks
    assert n % tm == 0
    row = lambda wd: pl.BlockSpec((tm, wd), lambda i: (i, 0))
    tab = pl.BlockSpec((tm, LANES), lambda i: (i % n_tab_blocks, 0))
    return pl.pallas_call(
        _mla_prep_kernel,
        grid=(n // tm,),
        in_specs=[row(PA_W), tab, tab, tab] + [_full(w.shape) for w in wts],
        out_specs=[row(512), row(512), row(256), row(KV_LORA), row(QK_ROPE)],
        out_shape=[jax.ShapeDtypeStruct((n, 512), BF16), jax.ShapeDtypeStruct((n, 512), BF16),
                   jax.ShapeDtypeStruct((n, 256), BF16), jax.ShapeDtypeStruct((n, KV_LORA), F32),
                   jax.ShapeDtypeStruct((n, QK_ROPE), F32)],
        compiler_params=_cparams("parallel"),
        name="mla_prep",
    )(pa, c, sa, sb, *wts)


def _mla_attn_prompt_kernel(q_ref, k_ref, v_ref, o_ref, *, tq):
    i = pl.program_id(1)
    outs = []
    for h0 in range(0, N_HEADS, 2):
        heads = (h0, h0 + 1)

        def blk(j, carry, masked, heads=heads):
            off = pl.multiple_of(j * tq, tq)
            new = []
            for n, h in enumerate(heads):
                m, l, acc = carry[3 * n:3 * n + 3]
                kb = k_ref[pl.ds(off, tq), h * LANES:(h + 1) * LANES]
                vb = v_ref[pl.ds(off, tq), h * HEAD_DIM:(h + 1) * HEAD_DIM]
                s = _dot_nt(q_ref[:, h * LANES:(h + 1) * LANES], kb)
                if masked:
                    r = lax.broadcasted_iota(jnp.int32, (tq, tq), 0)
                    cc = lax.broadcasted_iota(jnp.int32, (tq, tq), 1)
                    s = jnp.where(cc <= r, s, NEG)
                m_new = jnp.maximum(m, jnp.max(s, axis=-1, keepdims=True))
                corr = jnp.exp(m - m_new)
                p = jnp.exp(s - m_new)
                l = l * corr + jnp.sum(p, axis=-1, keepdims=True)
                acc = acc * corr + _dot(p.astype(BF16), vb)
                new += [m_new, l, acc]
            return tuple(new)

        init = (jnp.full((tq, 1), NEG, F32), jnp.zeros((tq, 1), F32), jnp.zeros((tq, HEAD_DIM), F32)) * 2
        carry = lax.fori_loop(0, i, functools.partial(blk, masked=False), init)
        carry = blk(i, carry, True)
        outs += [carry[2] / carry[1], carry[5] / carry[4]]
    o_ref[...] = jnp.concatenate(outs, axis=1)


def mla_attn_prompt(q, k, v, b, t):
    tq = _tile(t, 512)
    nq = t // tq
    return pl.pallas_call(
        functools.partial(_mla_attn_prompt_kernel, tq=tq),
        grid=(b, nq),
        in_specs=[pl.BlockSpec((tq, 512), lambda bi, i: (bi * nq + i, 0)),
                  pl.BlockSpec((t, 512), lambda bi, i: (bi, 0)),
                  pl.BlockSpec((t, 256), lambda bi, i: (bi, 0))],
        out_specs=pl.BlockSpec((tq, 256), lambda bi, i: (bi * nq + i, 0)),
        out_shape=jax.ShapeDtypeStruct((b * t, 256), F32),
        compiler_params=_cparams("parallel", "arbitrary"),
        name="mla_attn_prompt",
    )(q, k, v)


N_SUB = 8
PAGE_SLOTS = 3


def _mla_attn_sample_kernel(pt_ref, qbd_ref, qpe_ref, ckvn_ref, kpen_ref, wuk_ref, wukt_ref, wuv_ref, ckv_hbm,
                            kpe_hbm, o_ref, ckv_buf, kpe_buf, sem, m_scr, l_scr, acc_scr, lhs_scr, *, n_pg, tq,
                            layer):
    b_id = pl.program_id(0)
    p_id = pl.program_id(1)
    n_steps = pl.num_programs(1)
    step = b_id * n_steps + p_id
    n_total = pl.num_programs(0) * n_steps
    slot = step % PAGE_SLOTS

    def page_copies_of(st):
        return page_copies(st // n_steps, st % n_steps, st % PAGE_SLOTS)

    def page_copies(b, p, sl):
        copies = []
        for k in range(n_pg):
            phys = pt_ref[b, p * n_pg + k]
            copies.append(pltpu.make_async_copy(ckv_hbm.at[phys, layer], ckv_buf.at[sl, pl.ds(k * PAGE, PAGE), :],
                                                sem.at[sl, 0]))
            copies.append(pltpu.make_async_copy(kpe_hbm.at[phys, layer], kpe_buf.at[sl, :, pl.ds(k * PAGE, PAGE)],
                                                sem.at[sl, 1]))
        return copies

    for ahead in range(PAGE_SLOTS - 1):
        @pl.when((step == 0) & (ahead < n_total))
        def _(ahead=ahead):
            for c in page_copies_of(ahead):
                c.start()

    @pl.when(step + (PAGE_SLOTS - 1) < n_total)
    def _():
        for c in page_copies_of(step + (PAGE_SLOTS - 1)):
            c.start()

    qpe = qpe_ref[0]
    nr = N_HEADS * tq

    @pl.when(p_id == 0)
    def _():
        lhs_scr[0:GROUP_W, :] = wukt_ref[...]
        lhs_scr[GROUP_W:GROUP_W + nr, :] = _dot_nt(qbd_ref[0], wuk_ref[...]).astype(BF16)

    def scores(ckv, kpe_t):
        ckv_b = ckv.astype(BF16)
        out = _dot_nt(lhs_scr[...], ckv_b)
        rinv = []
        for h in range(N_HEADS):
            kn = out[h * HEAD_DIM:(h + 1) * HEAD_DIM]
            ss = jnp.sum(kn * kn, axis=0, keepdims=True) * (1.0 / HEAD_DIM)
            rinv.append(jnp.broadcast_to(lax.rsqrt(ss + EPS), (tq, ss.shape[1])))
        s = out[GROUP_W:GROUP_W + nr] * jnp.concatenate(rinv, axis=0) + _dot(qpe, kpe_t.astype(BF16))
        return s, ckv_b

    @pl.when(p_id == 0)
    def _():
        s, ckv_b = scores(ckvn_ref[0], kpen_ref[0])
        r = lax.broadcasted_iota(jnp.int32, s.shape, 0) % tq
        cc = lax.broadcasted_iota(jnp.int32, s.shape, 1)
        s = jnp.where(cc <= r, s, NEG)
        m = jnp.max(s, axis=-1, keepdims=True)
        p = jnp.exp(s - m)
        m_scr[...] = m
        l_scr[...] = jnp.sum(p, axis=-1, keepdims=True)
        acc_scr[...] = _dot(p.astype(BF16), ckv_b)

    for c in page_copies(b_id, p_id, slot):
        c.wait()

    n_sub = min(N_SUB, n_pg)
    keys = (n_pg // n_sub) * PAGE
    s_parts, ckv_parts = [], []
    for g in range(n_sub):
        s, ckv_b = scores(ckv_buf[slot, g * keys:(g + 1) * keys, :], kpe_buf[slot, :, g * keys:(g + 1) * keys])
        s_parts.append(s)
        ckv_parts.append(ckv_b)
    m = m_scr[...]
    m_new = m
    for s in s_parts:
        m_new = jnp.maximum(m_new, jnp.max(s, axis=-1, keepdims=True))
    corr = jnp.exp(m - m_new)
    l = l_scr[...] * corr
    acc = acc_scr[...] * corr
    for s, ckv_b in zip(s_parts, ckv_parts):
        p = jnp.exp(s - m_new)
        l = l + jnp.sum(p, axis=-1, keepdims=True)
        acc = acc + _dot(p.astype(BF16), ckv_b)
    m_scr[...] = m_new
    l_scr[...] = l
    acc_scr[...] = acc

    @pl.when(p_id == n_steps - 1)
    def _():
        lat = (acc / l).astype(BF16)
        full = _dot(lat, wuv_ref[...])
        lane_head = lax.broadcasted_iota(jnp.int32, (tq, 256), 1) // HEAD_DIM
        out = jnp.zeros((tq, 256), F32)
        for h in range(N_HEADS):
            out = jnp.where(lane_head == h, full[h * tq:(h + 1) * tq, :], out)
        o_ref[0] = out


def mla_attn_sample(page_table, qbd, qpe, ckv_new, kpe_new_t, wuk, wukt, wuv, cache_ckv, cache_kpe_t, layer):
    bs, n_pages = page_table.shape
    tq = qbd.shape[1] // N_HEADS
    n_pg = min(PAGES_PER_STEP, n_pages)
    assert n_pages % n_pg == 0
    nr = N_HEADS * tq
    per_b = lambda shp: pl.BlockSpec((1,) + shp, lambda b, p, pt: (b, 0, 0))
    cst = lambda shp: pl.BlockSpec(shp, lambda b, p, pt: (0,) * len(shp))
    hbm = pl.BlockSpec(memory_space=pl.ANY)
    grid_spec = pltpu.PrefetchScalarGridSpec(
        num_scalar_prefetch=1,
        grid=(bs, n_pages // n_pg),
        in_specs=[per_b((nr, 256)), per_b((nr, QK_ROPE)), per_b((PAGE, KV_LORA)), per_b((QK_ROPE, PAGE)),
                  cst(wuk.shape), cst(wukt.shape), cst(wuv.shape), hbm, hbm],
        out_specs=pl.BlockSpec((1, tq, 256), lambda b, p, pt: (b, 0, 0)),
        scratch_shapes=[pltpu.VMEM((PAGE_SLOTS, n_pg * PAGE, KV_LORA), F32),
                        pltpu.VMEM((PAGE_SLOTS, QK_ROPE, n_pg * PAGE), F32),
                        pltpu.SemaphoreType.DMA((PAGE_SLOTS, 2)),
                        pltpu.VMEM((nr, 1), F32), pltpu.VMEM((nr, 1), F32), pltpu.VMEM((nr, KV_LORA), F32),
                        pltpu.VMEM((GROUP_W + nr, KV_LORA), BF16)],
    )
    return pl.pallas_call(
        functools.partial(_mla_attn_sample_kernel, n_pg=n_pg, tq=tq, layer=layer),
        grid_spec=grid_spec,
        out_shape=jax.ShapeDtypeStruct((bs, tq, 256), F32),
        compiler_params=_cparams("arbitrary", "arbitrary"),
        name="mla_attn_sample",
    )(page_table, qbd, qpe, ckv_new, kpe_new_t, wuk, wukt, wuv, cache_ckv, cache_kpe_t)


CONV_HALO = 32


def _conv_kernel(pb_ref, st_ref, cw_ref, cb_ref, e_ref, g_ref, b_ref, pw_ref, o_ref, st_out_ref, xbuf, *, tt, n_seq):
    j = pl.program_id(1)
    lo = CONV_HALO - (CONV_W - 1)

    @pl.when(j == 0)
    def _():
        xbuf[:, pl.ds(lo, CONV_W - 1), :] = st_ref[...]

    pb = pb_ref[...]
    glu = pb[:, 0:256] * _sigmoid(pb[:, 256:512])
    ys = []
    for s in range(n_seq):
        xbuf[s, pl.ds(CONV_HALO, tt), :] = glu[s * tt:(s + 1) * tt]
        y = jnp.zeros((tt, 256), F32) + cb_ref[...]
        for k in range(CONV_W):
            y = y + xbuf[s, pl.ds(lo + k, tt), :] * cw_ref[pl.ds(k, 1), :]
        new_state = xbuf[s, pl.ds(lo + tt, CONV_W - 1), :]
        xbuf[s, pl.ds(lo, CONV_W - 1), :] = new_state
        st_out_ref[s] = new_state
        ys.append(y)
    y = jnp.concatenate(ys, axis=0) if n_seq > 1 else ys[0]
    e = e_ref[...]
    yc = y - _seg_sum(y, e)
    yn = yc * lax.rsqrt(_seg_sum(yc * yc, e) + LN_EPS) * g_ref[...] + b_ref[...]
    act = yn * _sigmoid(yn)
    o_ref[...] = _dot(act.astype(BF16), pw_ref[...])


SHORT_SEQ_ROWS = 128


def _seq_tiling(b, t):
    if t >= SHORT_SEQ_ROWS:
        return 1, _tile(t, 256)
    return _tile(b, SHORT_SEQ_ROWS // t), t


def conv_module(pb, state, wts, b, t):
    n_seq, tt = _seq_tiling(b, t)
    nt = t // tt
    return pl.pallas_call(
        functools.partial(_conv_kernel, tt=tt, n_seq=n_seq),
        grid=(b // n_seq, nt),
        in_specs=[pl.BlockSpec((n_seq * tt, 512), lambda bi, j: (bi * nt + j, 0)),
                  pl.BlockSpec((n_seq, CONV_W - 1, 256), lambda bi, j: (bi, 0, 0))]
                 + [_full(w.shape) for w in wts],
        out_specs=[pl.BlockSpec((n_seq * tt, 256), lambda bi, j: (bi * nt + j, 0)),
                   pl.BlockSpec((n_seq, CONV_W - 1, 256), lambda bi, j: (bi, 0, 0))],
        out_shape=[jax.ShapeDtypeStruct((b * t, 256), F32), jax.ShapeDtypeStruct((b, CONV_W - 1, 256), F32)],
        scratch_shapes=[pltpu.VMEM((n_seq, CONV_HALO + tt, 256), F32)],
        compiler_params=_cparams("parallel", "arbitrary"),
        name="conv_module",
    )(pb, state, *wts)


def _sgu_kernel(pc_ref, g_ref, b_ref, w_ref, bias_ref, o_ref, v_ref, *, n_chunks):
    x = pc_ref[...]
    z = 0.5 * x * (1.0 + jnp.tanh(0.7978845608028654 * (x + 0.044715 * (x * x * x))))
    u = z[:, 0:256]
    v = z[:, 256:512]
    vc = v - jnp.mean(v, axis=-1, keepdims=True)
    v = vc * lax.rsqrt(jnp.mean(vc * vc, axis=-1, keepdims=True) + LN_EPS) * g_ref[...] + b_ref[...]
    v_ref[...] = v
    lane_head = lax.broadcasted_iota(jnp.int32, (CHUNK, 256), 1) // HEAD_DIM
    for c in range(n_chunks):
        vcb = v[c * CHUNK:(c + 1) * CHUNK, :]
        sv = bias_ref[...]
        for h in range(N_HEADS):
            sv = sv + _dot(w_ref[h], jnp.where(lane_head == h, vcb, 0.0).astype(BF16))
        o_ref[pl.ds(c * CHUNK, CHUNK), :] = u[c * CHUNK:(c + 1) * CHUNK, :] * sv


def sgu(pc, wts):
    n = pc.shape[0]
    tm = _tile(n, 512)
    row = lambda wd: pl.BlockSpec((tm, wd), lambda i: (i, 0))
    return pl.pallas_call(
        functools.partial(_sgu_kernel, n_chunks=tm // CHUNK),
        grid=(n // tm,),
        in_specs=[row(512)] + [_full(w.shape) for w in wts],
        out_specs=[row(256), row(256)],
        out_shape=[jax.ShapeDtypeStruct((n, 256), F32), jax.ShapeDtypeStruct((n, 256), F32)],
        compiler_params=_cparams("parallel"),
        name="sgu",
    )(pc, *wts)


RW_HALO = 8


def _rwkv_prep_kernel(pd_ref, sh_ref, mu_ref, w0_ref, w2_ref, a0_ref, a2_ref, g2_ref, kkp_ref, ka_ref, rk_ref,
                      e_ref, r_ref, w_ref, k_ref, v_ref, kk_ref, kka_ref, g_ref, bv_ref, xbuf, *, tt, n_seq):
    j = pl.program_id(1)

    @pl.when(j == 0)
    def _():
        xbuf[:, pl.ds(RW_HALO - 1, 1), :] = sh_ref[...]

    pd = pd_ref[...]
    prevs = []
    for s in range(n_seq):
        rows = pd[s * tt:(s + 1) * tt]
        xbuf[s, pl.ds(RW_HALO, tt), :] = rows
        prevs.append(xbuf[s, pl.ds(RW_HALO - 1, tt), :])
        xbuf[s, pl.ds(RW_HALO - 1, 1), :] = rows[tt - 1:tt, :]
    prev = jnp.concatenate(prevs, axis=0) if n_seq > 1 else prevs[0]
    xs = pd + (prev - pd) * mu_ref[...]
    r = xs[:, 0:256]
    k = xs[:, 256:512]
    v = xs[:, 512:768]
    xwa = xs[:, 768:896]
    xg = xs[:, 896:1024]
    z = -(w0_ref[...] + _dot(jnp.tanh(xwa).astype(BF16), w2_ref[...]))
    softplus = jnp.maximum(z, 0.0) + jnp.log(1.0 + jnp.exp(-jnp.abs(z)))
    w_ref[...] = jnp.exp(-jnp.exp(-softplus - 0.5))
    a = _sigmoid(a0_ref[...] + _dot(xwa.astype(BF16), a2_ref[...]))
    g_ref[...] = _dot(_sigmoid(xg).astype(BF16), g2_ref[...])
    e = e_ref[...]
    kk = k * kkp_ref[...]
    kk = kk * lax.rsqrt(_seg_sum(kk * kk, e) + 1e-12)
    k2 = k * (1.0 + (a - 1.0) * ka_ref[...])
    r_ref[...] = r
    k_ref[...] = k2
    v_ref[...] = v
    kk_ref[...] = kk
    kka_ref[...] = kk * a
    bv_ref[...] = _seg_sum(r * k2 * rk_ref[...], e) * v


def rwkv_prep(pd, shift, wts, b, t):
    n_seq, tt = _seq_tiling(b, t)
    nt = t // tt
    row = lambda wd: pl.BlockSpec((n_seq * tt, wd), lambda bi, j: (bi * nt + j, 0))
    return pl.pallas_call(
        functools.partial(_rwkv_prep_kernel, tt=tt, n_seq=n_seq),
        grid=(b // n_seq, nt),
        in_specs=[row(PD_W), pl.BlockSpec((n_seq, 1, PD_W), lambda bi, j: (bi, 0, 0))]
                 + [_full(w.shape) for w in wts],
        out_specs=[row(256)] * 8,
        out_shape=[jax.ShapeDtypeStruct((b * t, 256), F32)] * 8,
        scratch_shapes=[pltpu.VMEM((n_seq, RW_HALO + tt, PD_W), F32)],
        compiler_params=_cparams("parallel", "arbitrary"),
        name="rwkv_prep",
    )(pd, shift, *wts)


RW_BLOCK = 128


def _pack_bf16_pair(a, b):
    ua = lax.bitcast_convert_type(a.astype(BF16).astype(F32), jnp.uint32)
    ub = lax.bitcast_convert_type(b.astype(BF16).astype(F32), jnp.uint32)
    return lax.bitcast_convert_type(ua | (ub >> 16), jnp.int32)


def _unpack_bf16_pair(word):
    u = lax.bitcast_convert_type(word, jnp.uint32)
    return (lax.bitcast_convert_type(u & jnp.uint32(0xFFFF0000), F32),
            lax.bitcast_convert_type(u << 16, F32))


def _rwkv_scan_kernel(w_ref, kk_ref, kka_ref, k_ref, r_ref, v_ref, s0_ref, e_ref, y_ref, sf_ref, s_scr, col_scr, *,
                      nb, sblk):
    c = pl.program_id(1)

    @pl.when(c == 0)
    def _():
        s_scr[...] = s0_ref[...]

    lane = lax.broadcasted_iota(jnp.int32, (HEAD_DIM, LANES), 1)
    low = lane < HEAD_DIM
    n_half = 2 if sblk > HEAD_DIM else 1
    for b in range(nb):
        for p in range(2):
            blk = lambda ref: ref[b, :, p * LANES:(p + 1) * LANES]
            x = _pack_bf16_pair(blk(kk_ref), blk(kka_ref))
            if sblk < LANES:
                x = jnp.concatenate([x, jnp.zeros((LANES - sblk, LANES), jnp.int32)], axis=0)
            xt = x.T
            h0, h1 = xt[0:HEAD_DIM], xt[HEAD_DIM:LANES]
            col_scr[b, p, 0] = jnp.where(low, h0, pltpu.roll(h1, HEAD_DIM, 1))
            if n_half == 2:
                col_scr[b, p, 1] = jnp.where(low, pltpu.roll(h0, HEAD_DIM, 1), h1)

    base = jnp.where(low, 0, HEAD_DIM)
    diag = (lax.broadcasted_iota(jnp.int32, (HEAD_DIM, 256), 1) % HEAD_DIM
            == lax.broadcasted_iota(jnp.int32, (HEAD_DIM, 256), 0))
    for half in range(n_half):
        n_groups = min(sblk - half * HEAD_DIM, HEAD_DIM) // SUBLANES

        def group(gi, carry, half=half):
            row0 = pl.multiple_of(half * HEAD_DIM + gi * SUBLANES, SUBLANES)
            for b in range(nb):
                v8 = v_ref[b, pl.ds(row0, SUBLANES), :]
                w8 = w_ref[b, pl.ds(row0, SUBLANES), :]
                w8_hi = w8.astype(BF16).astype(F32)
                rows8 = (k_ref[b, pl.ds(row0, SUBLANES), :], r_ref[b, pl.ds(row0, SUBLANES), :], w8_hi, w8 - w8_hi)
                on_diag = [jnp.where(diag, jnp.broadcast_to(x8[i:i + 1, :], (HEAD_DIM, 256)), 0.0).astype(BF16)
                           for i in range(SUBLANES) for x8 in rows8]
                spread = _dot(jnp.concatenate(on_diag, axis=0), e_ref[...])
                ys = []
                for p in range(2):
                    st = s_scr[b, :, p * LANES:(p + 1) * LANES]
                    yp = []
                    for i in range(SUBLANES):
                        idx = base + (gi * SUBLANES + i)
                        kk, kka = _unpack_bf16_pair(jnp.take_along_axis(col_scr[b, p, half], idx, axis=1))
                        part = lambda n: spread[(4 * i + n) * HEAD_DIM:(4 * i + n + 1) * HEAD_DIM,
                                                p * LANES:(p + 1) * LANES]
                        k, r, w = part(0), part(1), part(2) + part(3)
                        vrow = v8[i:i + 1, p * LANES:(p + 1) * LANES]
                        sa = -jnp.sum(st * kk, axis=0, keepdims=True)
                        st = st * w + kka * sa + k * vrow
                        yp.append(jnp.sum(st * r, axis=0, keepdims=True))
                    s_scr[b, :, p * LANES:(p + 1) * LANES] = st
                    ys.append(jnp.concatenate(yp, axis=0))
                y_ref[b, pl.ds(row0, SUBLANES), :] = jnp.concatenate(ys, axis=1)
            return carry

        lax.fori_loop(0, n_groups, group, 0)

    @pl.when(c == pl.num_programs(1) - 1)
    def _():
        sf_ref[...] = s_scr[...]


def rwkv_scan(seqs, v, s0, nb):
    b, t, _ = v.shape
    e_heads = _head_matrix(1.0)
    sblk = min(t, RW_BLOCK)
    assert b % nb == 0 and t % sblk == 0 and sblk % SUBLANES == 0 and (sblk <= HEAD_DIM or sblk == RW_BLOCK)
    st_spec = pl.BlockSpec((nb, HEAD_DIM, 256), lambda bi, c: (bi, 0, 0))
    seq_spec = pl.BlockSpec((nb, sblk, 256), lambda bi, c: (bi, c, 0))
    return pl.pallas_call(
        functools.partial(_rwkv_scan_kernel, nb=nb, sblk=sblk),
        grid=(b // nb, t // sblk),
        in_specs=[seq_spec] * 6 + [st_spec, _full(e_heads.shape)],
        out_specs=[seq_spec, st_spec],
        out_shape=[jax.ShapeDtypeStruct((b, t, 256), F32), jax.ShapeDtypeStruct((b, HEAD_DIM, 256), F32)],
        scratch_shapes=[pltpu.VMEM((nb, HEAD_DIM, 256), F32),
                        pltpu.VMEM((nb, 2, 2, HEAD_DIM, LANES), jnp.int32)],
        compiler_params=_cparams("parallel", "arbitrary"),
        name="rwkv_scan",
    )(*seqs, v, s0, e_heads)


def _mix_out_kernel(x_ref, oa_ref, ob_ref, oc_ref, y_ref, bv_ref, g_ref, e_ref, lg_ref, lb_ref, on_ref, w_ref,
                    nx_ref, wq_ref, qg_ref, o_ref, q_ref):
    e = e_ref[...]
    y = y_ref[...]
    yc = y - _seg_sum(y, e)
    yn = yc * lax.rsqrt(_seg_sum(yc * yc, e) + RW_LN_EPS) * lg_ref[...] + lb_ref[...]
    od = (yn + bv_ref[...]) * g_ref[...]
    acc = x_ref[...]
    for gi, o in enumerate((oa_ref[...], ob_ref[...], oc_ref[...], od)):
        on = (_rms(o) * on_ref[:, gi * 256:(gi + 1) * 256]).astype(BF16)
        acc = acc + _dot(on, w_ref[pl.ds(gi * 256, 256), :])
    o_ref[...] = acc
    q = _dot((_rms(acc) * nx_ref[...]).astype(BF16), wq_ref[...])
    heads = [_rms(q[:, h * X_HEAD_DIM:(h + 1) * X_HEAD_DIM]) * qg_ref[...] * X_HEAD_DIM ** -0.5
             for h in range(X_HEADS)]
    q_ref[...] = jnp.concatenate(heads, axis=1).astype(q_ref.dtype)


def mix_out(x, oa, ob, oc, y, bv, g, wts, q_dtype):
    n, d = x.shape
    tm = _tile(n, 512)
    row = lambda wd: pl.BlockSpec((tm, wd), lambda i: (i, 0))
    return pl.pallas_call(
        _mix_out_kernel,
        grid=(n // tm,),
        in_specs=[row(d)] + [row(256)] * 6 + [_full(w.shape) for w in wts],
        out_specs=[row(d), row(512)],
        out_shape=[jax.ShapeDtypeStruct((n, d), F32), jax.ShapeDtypeStruct((n, 512), q_dtype)],
        compiler_params=_cparams("parallel"),
        name="mix_out",
    )(x, oa, ob, oc, y, bv, g, *wts)


XATTN_SHORT_ROWS = 64


def _xattn_kernel(x_ref, q_ref, k_ref, v_ref, wo_ref, o_ref, *, n_seq, rows, n_mem):
    seq_outs = []
    for s in range(n_seq):
        qs = [q_ref[s * rows:(s + 1) * rows, h * X_HEAD_DIM:(h + 1) * X_HEAD_DIM].astype(BF16)
              for h in range(X_HEADS)]
        if n_seq == 1:
            outs = []
            for h in range(X_HEADS):
                head_rows = pl.ds(h, n_mem, stride=X_HEADS)
                sc = _dot_nt(qs[h], k_ref[s, head_rows, :].astype(BF16))
                p = jnp.exp(sc - jnp.max(sc, axis=-1, keepdims=True))
                p = p / jnp.sum(p, axis=-1, keepdims=True)
                outs.append(_dot(p.astype(BF16), v_ref[s, head_rows, :].astype(BF16)))
        else:
            sc = _dot_nt(jnp.concatenate(qs, axis=0), k_ref[s].astype(BF16))
            row_head = lax.broadcasted_iota(jnp.int32, sc.shape, 0) // rows
            col_head = lax.broadcasted_iota(jnp.int32, sc.shape, 1) % X_HEADS
            sc = jnp.where(row_head == col_head, sc, NEG)
            p = jnp.exp(sc - jnp.max(sc, axis=-1, keepdims=True))
            p = p / jnp.sum(p, axis=-1, keepdims=True)
            o = _dot(p.astype(BF16), v_ref[s].astype(BF16))
            outs = [o[h * rows:(h + 1) * rows] for h in range(X_HEADS)]
        seq_outs.append(jnp.concatenate(outs, axis=1))
    xo = jnp.concatenate(seq_outs, axis=0).astype(BF16)
    o_ref[...] = x_ref[...] + _dot(xo, wo_ref[...])


def xattn(x, q, mem_k, mem_v, wo, b, t, kv_seq0):
    n, d = x.shape
    n_mem = mem_k.shape[1] // X_HEADS
    if t >= XATTN_SHORT_ROWS:
        n_seq, rows = 1, _tile(t, 512)
    else:
        n_seq, rows = _tile(b, XATTN_SHORT_ROWS // t), t
    nq = t // rows
    assert kv_seq0 % n_seq == 0
    row = lambda wd: pl.BlockSpec((n_seq * rows, wd), lambda bi, i: (bi * nq + i, 0))
    kv = pl.BlockSpec((n_seq, n_mem * X_HEADS, X_HEAD_DIM), lambda bi, i: (kv_seq0 // n_seq + bi, 0, 0))
    return pl.pallas_call(
        functools.partial(_xattn_kernel, n_seq=n_seq, rows=rows, n_mem=n_mem),
        grid=(b // n_seq, nq),
        in_specs=[row(d), row(512), kv, kv, _full(wo.shape)],
        out_specs=row(d),
        out_shape=jax.ShapeDtypeStruct((n, d), F32),
        compiler_params=_cparams("parallel", "arbitrary"),
        name="xattn",
    )(x, q, mem_k, mem_v, wo)


def _mem_kv_kernel(x_ref, g_ref, w_ref, hg_ref, k_ref, v_ref, *, tm):
    hb = (_rms(x_ref[...]) * g_ref[...]).astype(BF16)
    y = _dot(hb, w_ref[...])
    for h in range(X_HEADS):
        head_rows = pl.ds(h, tm, stride=X_HEADS)
        k_ref[head_rows, :] = _rms(y[:, h * X_HEAD_DIM:(h + 1) * X_HEAD_DIM]) * hg_ref[...]
        v_ref[head_rows, :] = y[:, (X_HEADS + h) * X_HEAD_DIM:(X_HEADS + h + 1) * X_HEAD_DIM]


def mem_kv(x, g, w, hg):
    n, d = x.shape
    tm = _tile(n, 512)
    out = pl.BlockSpec((tm * X_HEADS, X_HEAD_DIM), lambda i: (i, 0))
    return pl.pallas_call(
        functools.partial(_mem_kv_kernel, tm=tm),
        grid=(n // tm,),
        in_specs=[pl.BlockSpec((tm, d), lambda i: (i, 0)), _full(g.shape), _full(w.shape), _full(hg.shape)],
        out_specs=[out, out],
        out_shape=[jax.ShapeDtypeStruct((n * X_HEADS, X_HEAD_DIM), F32)] * 2,
        compiler_params=_cparams("parallel"),
        name="mem_kv",
    )(x, g, w, hg)


FFN_CHUNK = 256


def _ffn_kernel(x_ref, g_ref, wg_ref, wu_ref, wo_ref, o_ref, *, n_chunks):
    x = x_ref[...]
    hb = (_rms(x) * g_ref[...]).astype(BF16)
    acc = x
    for c in range(n_chunks):
        a = _dot(hb, wg_ref[c])
        u = _dot(hb, wu_ref[c])
        acc = acc + _dot((a * _sigmoid(a) * u).astype(BF16), wo_ref[c])
    o_ref[...] = acc


def ffn(x, g, wg, wu, wo):
    n, d = x.shape
    tm = _tile(n, 512)
    row = pl.BlockSpec((tm, d), lambda i: (i, 0))
    return pl.pallas_call(
        functools.partial(_ffn_kernel, n_chunks=wg.shape[0]),
        grid=(n // tm,),
        in_specs=[row, _full(g.shape), _full(wg.shape), _full(wu.shape), _full(wo.shape)],
        out_specs=row,
        out_shape=jax.ShapeDtypeStruct((n, d), F32),
        compiler_params=_cparams("parallel"),
        name="ffn",
    )(x, g, wg, wu, wo)


def _seg_matrix(seg_ids, seg_len):
    s = np.asarray(seg_ids)
    m = (s[:, None] == s[None, :]) & (s[:, None] >= 0)
    return jnp.asarray(m.astype(np.float32) / np.asarray(seg_len, np.float32)[None, :], BF16)


def _mla_segments():
    lane = np.arange(512)
    blk, off = lane // 128, lane % 128
    q_ids = np.where(off < 32, 2 * blk, np.where(off < 64, -1, 2 * blk + 1))
    q_len = np.where(off < 32, 32.0, 64.0)
    k_ids = np.where(off < 64, -1, blk)
    k_len = np.full(512, 64.0)
    return _seg_matrix(q_ids, q_len), _seg_matrix(k_ids, k_len)


def _head_matrix(scale_len):
    lane = np.arange(256)
    return _seg_matrix(lane // HEAD_DIM, np.full(256, scale_len))


def _rope_tables(pos):
    half = QK_ROPE // 2
    inv = jnp.power(ROPE_BASE, -jnp.arange(half, dtype=F32) / half)
    ang = pos.astype(F32)[:, None] * inv[None, :]
    cos, sin = jnp.cos(ang), jnp.sin(ang)
    n = pos.shape[0]
    z = lambda w: jnp.zeros((n, w), F32)
    c = jnp.concatenate([cos, cos, jnp.ones((n, LANES - QK_ROPE), F32)], axis=1)
    sa = jnp.concatenate([-sin, z(LANES - half)], axis=1)
    sb = jnp.concatenate([z(half), sin, z(LANES - QK_ROPE)], axis=1)
    return c, sa, sb


def _layer_weights(p):
    row = lambda v: v.reshape(1, -1).astype(F32)
    zc = lambda a, w: jnp.zeros((a.shape[0], w), a.dtype)
    w_in = p['w_in']
    c1, c2, c3 = 352, 352 + 512, 352 + 1024
    pa = w_in[:, :c1]
    w_in_p = jnp.concatenate(
        [pa[:, :Q_LORA], zc(pa, 64), pa[:, Q_LORA:Q_LORA + KV_LORA], pa[:, Q_LORA + KV_LORA:], zc(pa, 96),
         w_in[:, c1:c2], w_in[:, c2:c3], w_in[:, c3:]], axis=1).astype(BF16)

    wuq = p['mla_w_uq'].reshape(Q_LORA, N_HEADS, QK_NOPE + QK_ROPE)
    wuq = jnp.concatenate([wuq[:, :, QK_NOPE:], jnp.zeros((Q_LORA, N_HEADS, 32), F32), wuq[:, :, :QK_NOPE]], axis=2)
    wuq = jnp.concatenate([wuq.reshape(Q_LORA, 512), jnp.zeros((256 - Q_LORA, 512), F32)], axis=0).astype(BF16)
    wuk = p['mla_w_uk'].reshape(KV_LORA, N_HEADS, QK_NOPE)
    wuk_p = jnp.concatenate([jnp.zeros((KV_LORA, N_HEADS, 64), F32), wuk], axis=2).reshape(KV_LORA, 512).astype(BF16)
    blk = lambda a, b_, c_: jnp.tile(jnp.concatenate([a, b_, c_]), N_HEADS).reshape(1, 512)
    z32, z64 = jnp.zeros((32,), F32), jnp.zeros((64,), F32)
    gq = blk(p['mla_gq_rope'], z32, p['mla_gq_nope'])
    gk = blk(z32, z32, p['mla_gk_nope'])
    gqn = jnp.concatenate([p['mla_q_norm'], z64]).reshape(1, 256)
    gkr = jnp.concatenate([p['mla_gk_rope'], jnp.zeros((96,), F32)]).reshape(1, LANES)
    eq, ek = _mla_segments()
    mla = (gqn, row(p['mla_kv_norm']), wuq, eq, gq, gkr, wuk_p, ek, gk, p['mla_w_uv'].astype(BF16))

    e64 = _head_matrix(64.0)
    conv = (p['conv_w'], row(p['conv_b']), e64, row(p['conv_norm_g']), row(p['conv_norm_b']),
            p['conv_pw'].astype(BF16))

    z64r = jnp.zeros((64, GROUP_W), F32)
    rw = (row(p['rw_mu']),
          row(p['rw_w0']), jnp.concatenate([p['rw_w2'], z64r], axis=0).astype(BF16),
          row(p['rw_a0']), jnp.concatenate([z64r, p['rw_a2']], axis=0).astype(BF16),
          p['rw_g2'].astype(BF16), row(p['rw_kk']), row(p['rw_ka']), row(p['rw_rk']), _head_matrix(1.0))

    mix = (e64, row(p['rw_ln_g']), row(p['rw_ln_b']), row(p['out_norm']), p['w_out'].astype(BF16))

    d_ff = p['w_ffn_out'].shape[0]
    nck = d_ff // FFN_CHUNK
    d = w_in.shape[0]
    wg = p['w_ffn_in'][:, :d_ff].reshape(d, nck, FFN_CHUNK).transpose(1, 0, 2).astype(BF16)
    wu = p['w_ffn_in'][:, d_ff:].reshape(d, nck, FFN_CHUNK).transpose(1, 0, 2).astype(BF16)
    wo = p['w_ffn_out'].reshape(nck, FFN_CHUNK, d).astype(BF16)

    return dict(
        norm_mix=row(p['norm_mix']), w_in=w_in_p, mla=mla, conv=conv, rw=rw, mix=mix,
        wuk=p['mla_w_uk'].astype(BF16), wuv=p['mla_w_uv'].astype(BF16), gk_nope=p['mla_gk_nope'],
        sgu_ln=(row(p['sgu_norm_g']), row(p['sgu_norm_b'])), sgu_w=p['sgu_w'], sgu_b=p['sgu_b'],
        norm_x=row(p['norm_x']), wq=p['wq_x'].astype(BF16), xq_norm=row(p['xq_norm']),
        mem_norm=row(p['mem_norm']), wkv=jnp.concatenate([p['wk_x'], p['wv_x']], axis=1).astype(BF16),
        xk_norm=row(p['xk_norm']), wo_x=p['wo_x'].astype(BF16),
        norm_ffn=row(p['norm_ffn']), wg=wg, wu=wu, wo=wo)


def _sgu_weights(lw, t):
    l = min(t, CHUNK)
    w = lw['sgu_w'][:, :l, :l] * jnp.tril(jnp.ones((l, l), F32))
    reps = CHUNK // l
    if reps > 1:
        w = jnp.einsum('ab,hij->haibj', jnp.eye(reps, dtype=F32), w).reshape(N_HEADS, CHUNK, CHUNK)
    bias = jnp.tile(lw['sgu_b'][:, :l].T, (reps, 1))
    bias = jnp.repeat(bias, HEAD_DIM, axis=1)
    return lw['sgu_ln'] + (w.astype(BF16), bias)


def _trunk_layer(x, b, t, lw, tabs, n_tab_blocks, attend, mem_k, mem_v, kv_seq0, conv_state, shift_state, wkv_state,
                 scan_nb):
    pa, pb, pc, pd = proj_in(x, lw['norm_mix'], lw['w_in'])
    q, k, v, ckv, kpe = mla_prep(pa, tabs, lw['mla'], n_tab_blocks)
    oa = attend(q, k, v, ckv, kpe)
    ob, conv_new = conv_module(pb, conv_state, lw['conv'], b, t)
    oc, v_sgu = sgu(pc, _sgu_weights(lw, t))
    r, w, k2, vv, kk, kka, g, bv = rwkv_prep(pd, shift_state.reshape(b, 1, PD_W), lw['rw'], b, t)
    seqs = [a.reshape(b, t, 256) for a in (w, kk, kka, k2, r)]
    s0 = wkv_state.transpose(0, 3, 1, 2).reshape(b, HEAD_DIM, 256)
    y, s_fin = rwkv_scan(seqs, vv.reshape(b, t, 256), s0, scan_nb)
    wkv_new = s_fin.reshape(b, HEAD_DIM, N_HEADS, HEAD_DIM).transpose(0, 2, 3, 1)
    shift_new = pd.reshape(b, t, PD_W)[:, -1]
    q_dtype = BF16 if t % 16 == 0 else F32
    x, qx = mix_out(x, oa, ob, oc, y.reshape(b * t, 256), bv, g,
                    lw['mix'] + (lw['norm_x'], lw['wq'], lw['xq_norm']), q_dtype)
    x = xattn(x, qx, mem_k, mem_v, lw['wo_x'], b, t, kv_seq0)
    x = ffn(x, lw['norm_ffn'], lw['wg'], lw['wu'], lw['wo'])
    return x, ckv, kpe, conv_new, shift_new, wkv_new, v_sgu


def kernel(x_prompt, x_sample, mem_prompt, cache_ckv, cache_kpe, cache_mem_k, cache_mem_v, state_conv, state_shift, state_wkv, page_table, norm_mix, w_in, mla_q_norm, mla_kv_norm, mla_w_uq, mla_w_uk, mla_w_uv, mla_gq_nope, mla_gq_rope, mla_gk_nope, mla_gk_rope, conv_w, conv_b, conv_norm_g, conv_norm_b, conv_pw, sgu_norm_g, sgu_norm_b, sgu_w, sgu_b, rw_mu, rw_w0, rw_w2, rw_a0, rw_a2, rw_g2, rw_kk, rw_ka, rw_rk, rw_ln_g, rw_ln_b, out_norm, w_out, norm_x, mem_norm, wq_x, wk_x, wv_x, xq_norm, xk_norm, wo_x, norm_ffn, w_ffn_in, w_ffn_out):
    params = dict(
        norm_mix=norm_mix, w_in=w_in, mla_q_norm=mla_q_norm, mla_kv_norm=mla_kv_norm, mla_w_uq=mla_w_uq,
        mla_w_uk=mla_w_uk, mla_w_uv=mla_w_uv, mla_gq_nope=mla_gq_nope, mla_gq_rope=mla_gq_rope,
        mla_gk_nope=mla_gk_nope, mla_gk_rope=mla_gk_rope, conv_w=conv_w, conv_b=conv_b, conv_norm_g=conv_norm_g,
        conv_norm_b=conv_norm_b, conv_pw=conv_pw, sgu_norm_g=sgu_norm_g, sgu_norm_b=sgu_norm_b, sgu_w=sgu_w,
        sgu_b=sgu_b, rw_mu=rw_mu, rw_w0=rw_w0, rw_w2=rw_w2, rw_a0=rw_a0, rw_a2=rw_a2, rw_g2=rw_g2, rw_kk=rw_kk,
        rw_ka=rw_ka, rw_rk=rw_rk, rw_ln_g=rw_ln_g, rw_ln_b=rw_ln_b, out_norm=out_norm, w_out=w_out, norm_x=norm_x,
        mem_norm=mem_norm, wq_x=wq_x, wk_x=wk_x, wv_x=wv_x, xq_norm=xq_norm, xk_norm=xk_norm, wo_x=wo_x,
        norm_ffn=norm_ffn, w_ffn_in=w_ffn_in, w_ffn_out=w_ffn_out)
    depth = w_in.shape[0]
    bp, tp, d = x_prompt.shape
    bs, ts, _ = x_sample.shape
    n_mem = mem_prompt.shape[1]
    n_pages = page_table.shape[1]
    past_len = n_pages * PAGE

    tm_p = _tile(tp, 512)
    tabs_p = _rope_tables(jnp.arange(tp, dtype=jnp.int32))
    tm_s = _tile(bs * ts, 512)
    tabs_s = tuple(jnp.tile(a, (tm_s // ts, 1)) for a in _rope_tables(past_len + jnp.arange(ts, dtype=jnp.int32)))
    cache_kpe_t = jnp.swapaxes(cache_kpe, 2, 3)
    cache_k_rows = cache_mem_k.reshape(depth * bs, n_mem * X_HEADS, X_HEAD_DIM)
    cache_v_rows = cache_mem_v.reshape(depth * bs, n_mem * X_HEADS, X_HEAD_DIM)
    y_p = x_prompt.reshape(bp * tp, d)
    y_s = x_sample.reshape(bs * ts, d)
    mem_flat = mem_prompt.reshape(bp * n_mem, d)
    zeros_conv = jnp.zeros((bp, CONV_W - 1, GROUP_W), F32)
    zeros_shift = jnp.zeros((bp, PD_W), F32)
    zeros_wkv = jnp.zeros((bp, N_HEADS, HEAD_DIM, HEAD_DIM), F32)
    outs_p, outs_s, memk_l, memv_l = [], [], [], []
    for l in range(depth):
        lw = _layer_weights({k_: v_[l] for k_, v_ in params.items()})

        mk, mv = mem_kv(mem_flat, lw['mem_norm'], lw['wkv'], lw['xk_norm'])
        memk_l.append(mk.reshape(bp, n_mem, X_HEADS, X_HEAD_DIM))
        memv_l.append(mv.reshape(bp, n_mem, X_HEADS, X_HEAD_DIM))

        def attend_p(q, k, v, ckv, kpe):
            return mla_attn_prompt(q, k, v, bp, tp)

        kv_rows = n_mem * X_HEADS
        res = _trunk_layer(y_p, bp, tp, lw, tabs_p, tp // tm_p, attend_p, mk.reshape(bp, kv_rows, X_HEAD_DIM),
                           mv.reshape(bp, kv_rows, X_HEAD_DIM), 0, zeros_conv, zeros_shift, zeros_wkv,
                           _tile(bp, SCAN_NB_PROMPT if l == 0 else 2))
        y_p = res[0]
        outs_p.append(res[1:])

        def attend_s(q, k, v, ckv, kpe, l=l, lw=lw):
            qf = q.astype(F32).reshape(bs, ts, N_HEADS, LANES)
            qn = (qf[..., 64:] * lw['gk_nope']).transpose(0, 2, 1, 3)
            qbd = jnp.einsum('bhtj,hg->bhtgj', qn, jnp.eye(N_HEADS, dtype=F32)).reshape(bs, N_HEADS * ts, 256)
            qpe = qf[..., :QK_ROPE].transpose(0, 2, 1, 3).reshape(bs, N_HEADS * ts, QK_ROPE)
            pad = lambda a: jnp.pad(a.reshape(bs, ts, -1), ((0, 0), (0, PAGE - ts), (0, 0)))
            o = mla_attn_sample(page_table, qbd.astype(BF16), qpe.astype(BF16), pad(ckv),
                                pad(kpe).transpose(0, 2, 1), lw['wuk'], lw['wuk'].T, lw['wuv'], cache_ckv,
                                cache_kpe_t, l)
            return o.reshape(bs * ts, 256)

        res = _trunk_layer(y_s, bs, ts, lw, tabs_s, 1, attend_s, cache_k_rows, cache_v_rows, l * bs,
                           state_conv[l], state_shift[l], state_wkv[l], _tile(bs, SCAN_NB_SAMPLE))
        y_s = res[0]
        outs_s.append(res[1:])

    n_pp = tp // PAGE
    stack = lambda outs, i, ax: jnp.stack([o[i] for o in outs], axis=ax)
    ckv_prompt = stack(outs_p, 0, 0).reshape(depth, bp, n_pp, PAGE, KV_LORA).transpose(1, 2, 0, 3, 4)
    kpe_prompt = stack(outs_p, 1, 0).reshape(depth, bp, n_pp, PAGE, QK_ROPE).transpose(1, 2, 0, 3, 4)
    ckv_sample = stack(outs_s, 0, 0).reshape(depth, bs, ts, KV_LORA).transpose(1, 0, 2, 3)
    kpe_sample = stack(outs_s, 1, 0).reshape(depth, bs, ts, QK_ROPE).transpose(1, 0, 2, 3)
    return (y_p.reshape(bp, tp, d), y_s.reshape(bs, ts, d), ckv_prompt, kpe_prompt, ckv_sample, kpe_sample,
            jnp.stack(memk_l, 0), jnp.stack(memv_l, 0),
            stack(outs_p, 2, 0), stack(outs_s, 2, 0), stack(outs_p, 3, 0), stack(outs_s, 3, 0),
            stack(outs_p, 4, 0), stack(outs_s, 4, 0),
            stack(outs_s, 5, 0).reshape(depth, bs, ts, GROUP_W))
```

```python
import functools

import numpy as np
import jax
import jax.numpy as jnp
from jax import lax
from jax.experimental import pallas as pl
from jax.experimental.pallas import tpu as pltpu

F32 = jnp.float32
BF16 = jnp.bfloat16

EPS = 1e-6
LN_EPS = 1e-5
RW_LN_EPS = 64e-5
NEG = -1e30
ROPE_BASE = 10000.0

LANES = 128
SUBLANES = 8
VMEM_LIMIT_BYTES = 56 * 1024 * 1024

GROUP_W = 256
HEAD_DIM = 64
N_HEADS = 4
Q_LORA = 192
KV_LORA = 128
QK_ROPE = 32
QK_NOPE = 64
CONV_W = 31
CHUNK = 128
PAGE = 128
X_HEADS = 4
X_HEAD_DIM = 128
MLA_SCALE = (QK_NOPE + QK_ROPE) ** -0.5
PA_W = 512
PD_W = 1024
PAGES_PER_STEP = 32
SCAN_NB_PROMPT = 4
SCAN_NB_SAMPLE = 8


def _cparams(*sem):
    return pltpu.CompilerParams(dimension_semantics=sem, vmem_limit_bytes=VMEM_LIMIT_BYTES)


def _dot(a, b):
    return jnp.dot(a, b, preferred_element_type=F32)


def _dot_nt(a, b):
    return lax.dot_general(a, b, (((1,), (1,)), ((), ())), preferred_element_type=F32)


def _seg_sum(x, e):
    hi = x.astype(BF16)
    lo = (x - hi.astype(F32)).astype(BF16)
    return _dot(hi, e) + _dot(lo, e)


def _rms(x, width=None):
    w = x.shape[-1] if width is None else width
    return x * lax.rsqrt(jnp.sum(x * x, axis=-1, keepdims=True) * (1.0 / w) + EPS)


def _sigmoid(x):
    return 1.0 / (1.0 + jnp.exp(-x))


def _tile(n, pref):
    t = min(n, pref)
    while n % t:
        t //= 2
    return t


def _full(shape):
    nd = len(shape)
    return pl.BlockSpec(shape, lambda *a: (0,) * nd)


def _proj_in_kernel(x_ref, g_ref, w_ref, pa_ref, pb_ref, pc_ref, pd_ref):
    hb = (_rms(x_ref[...]) * g_ref[...]).astype(BF16)
    pa_ref[...] = _dot(hb, w_ref[:, 0:512])
    pb_ref[...] = _dot(hb, w_ref[:, 512:1024])
    pc_ref[...] = _dot(hb, w_ref[:, 1024:1536])
    pd_ref[...] = _dot(hb, w_ref[:, 1536:2560])


def proj_in(x, g, w):
    n, d = x.shape
    tm = _tile(n, 512)
    row = lambda wd: pl.BlockSpec((tm, wd), lambda i: (i, 0))
    return pl.pallas_call(
        _proj_in_kernel,
        grid=(n // tm,),
        in_specs=[row(d), _full(g.shape), _full(w.shape)],
        out_specs=[row(512), row(512), row(512), row(1024)],
        out_shape=[jax.ShapeDtypeStruct((n, wd), F32) for wd in (512, 512, 512, 1024)],
        compiler_params=_cparams("parallel"),
        name="proj_in",
    )(x, g, w)


def _rope128(x, c, sa, sb):
    w = x.shape[-1]
    return x * c + pltpu.roll(x, w - 16, 1) * sa + pltpu.roll(x, 16, 1) * sb


def _mla_prep_kernel(pa_ref, c_ref, sa_ref, sb_ref, gqn_ref, gkv_ref, wuq_ref, eq_ref, gq_ref,
                     gkr_ref, wuk_ref, ek_ref, gk_ref, wuv_ref,
                     q_ref, k_ref, v_ref, ckv_ref, kpe_ref):
    pa = pa_ref[...]
    c, sa, sb = c_ref[...], sa_ref[...], sb_ref[...]
    c4 = jnp.concatenate([c] * 4, axis=1)
    sa4 = jnp.concatenate([sa] * 4, axis=1)
    sb4 = jnp.concatenate([sb] * 4, axis=1)
    cq = _rms(pa[:, 0:256], Q_LORA) * gqn_ref[...]
    q = _dot(cq.astype(BF16), wuq_ref[...])
    qn = q * lax.rsqrt(_seg_sum(q * q, eq_ref[...]) + EPS) * gq_ref[...]
    q_ref[...] = (_rope128(qn, c4, sa4, sb4) * MLA_SCALE).astype(BF16)
    ckv = _rms(pa[:, 256:384]) * gkv_ref[...]
    ckv_ref[...] = ckv
    kp = _rms(pa[:, 384:512], QK_ROPE) * gkr_ref[...]
    kr = _rope128(kp, c, sa, sb)
    kpe_ref[...] = kr[:, 0:QK_ROPE]
    ckv_b = ckv.astype(BF16)
    kn = _dot(ckv_b, wuk_ref[...])
    kn = kn * lax.rsqrt(_seg_sum(kn * kn, ek_ref[...]) + EPS) * gk_ref[...]
    k_ref[...] = (kn + jnp.concatenate([kr] * 4, axis=1)).astype(BF16)
    v_ref[...] = _dot(ckv_b, wuv_ref[...]).astype(BF16)


def mla_prep(pa, tabs, wts, n_tab_blocks):
    n = pa.shape[0]
    c, sa, sb = tabs
    tm = c.shape[0] // n_tab_blocks
    assert n % tm == 0
    row = lambda wd: pl.BlockSpec((tm, wd), lambda i: (i, 0))
    tab = pl.BlockSpec((tm, LANES), lambda i: (i % n_tab_blocks, 0))
    return pl.pallas_call(
        _mla_prep_kernel,
        grid=(n // tm,),
        in_specs=[row(PA_W), tab, tab, tab] + [_full(w.shape) for w in wts],
        out_specs=[row(512), row(512), row(256), row(KV_LORA), row(QK_ROPE)],
        out_shape=[jax.ShapeDtypeStruct((n, 512), BF16), jax.ShapeDtypeStruct((n, 512), BF16),
                   jax.ShapeDtypeStruct((n, 256), BF16), jax.ShapeDtypeStruct((n, KV_LORA), F32),
                   jax.ShapeDtypeStruct((n, QK_ROPE), F32)],
        compiler_params=_cparams("parallel"),
        name="mla_prep",
    )(pa, c, sa, sb, *wts)


def _mla_attn_prompt_kernel(q_ref, k_ref, v_ref, o_ref, *, tq):
    i = pl.program_id(1)
    outs = []
    for h0 in range(0, N_HEADS, 2):
        heads = (h0, h0 + 1)

        def blk(j, carry, masked, heads=heads):
            off = pl.multiple_of(j * tq, tq)
            new = []
            for n, h in enumerate(heads):
                m, l, acc = carry[3 * n:3 * n + 3]
                kb = k_ref[pl.ds(off, tq), h * LANES:(h + 1) * LANES]
                vb = v_ref[pl.ds(off, tq), h * HEAD_DIM:(h + 1) * HEAD_DIM]
                s = _dot_nt(q_ref[:, h * LANES:(h + 1) * LANES], kb)
                if masked:
                    r = lax.broadcasted_iota(jnp.int32, (tq, tq), 0)
                    cc = lax.broadcasted_iota(jnp.int32, (tq, tq), 1)
                    s = jnp.where(cc <= r, s, NEG)
                m_new = jnp.maximum(m, jnp.max(s, axis=-1, keepdims=True))
                corr = jnp.exp(m - m_new)
                p = jnp.exp(s - m_new)
                l = l * corr + jnp.sum(p, axis=-1, keepdims=True)
                acc = acc * corr + _dot(p.astype(BF16), vb)
                new += [m_new, l, acc]
            return tuple(new)

        init = (jnp.full((tq, 1), NEG, F32), jnp.zeros((tq, 1), F32), jnp.zeros((tq, HEAD_DIM), F32)) * 2
        carry = lax.fori_loop(0, i, functools.partial(blk, masked=False), init)
        carry = blk(i, carry, True)
        outs += [carry[2] / carry[1], carry[5] / carry[4]]
    o_ref[...] = jnp.concatenate(outs, axis=1)


def mla_attn_prompt(q, k, v, b, t):
    tq = _tile(t, 512)
    nq = t // tq
    return pl.pallas_call(
        functools.partial(_mla_attn_prompt_kernel, tq=tq),
        grid=(b, nq),
        in_specs=[pl.BlockSpec((tq, 512), lambda bi, i: (bi * nq + i, 0)),
                  pl.BlockSpec((t, 512), lambda bi, i: (bi, 0)),
                  pl.BlockSpec((t, 256), lambda bi, i: (bi, 0))],
        out_specs=pl.BlockSpec((tq, 256), lambda bi, i: (bi * nq + i, 0)),
        out_shape=jax.ShapeDtypeStruct((b * t, 256), F32),
        compiler_params=_cparams("parallel", "arbitrary"),
        name="mla_attn_prompt",
    )(q, k, v)


N_SUB = 8
PAGE_SLOTS = 3


def _mla_attn_sample_kernel(pt_ref, qbd_ref, qpe_ref, ckvn_ref, kpen_ref, wuk_ref, wukt_ref, wuv_ref, ckv_hbm,
                            kpe_hbm, o_ref, ckv_buf, kpe_buf, sem, m_scr, l_scr, acc_scr, lhs_scr, *, n_pg, tq,
                            layer):
    b_id = pl.program_id(0)
    p_id = pl.program_id(1)
    n_steps = pl.num_programs(1)
    step = b_id * n_steps + p_id
    n_total = pl.num_programs(0) * n_steps
    slot = step % PAGE_SLOTS

    def page_copies_of(st):
        return page_copies(st // n_steps, st % n_steps, st % PAGE_SLOTS)

    def page_copies(b, p, sl):
        copies = []
        for k in range(n_pg):
            phys = pt_ref[b, p * n_pg + k]
            copies.append(pltpu.make_async_copy(ckv_hbm.at[phys, layer], ckv_buf.at[sl, pl.ds(k * PAGE, PAGE), :],
                                                sem.at[sl, 0]))
            copies.append(pltpu.make_async_copy(kpe_hbm.at[phys, layer], kpe_buf.at[sl, :, pl.ds(k * PAGE, PAGE)],
                                                sem.at[sl, 1]))
        return copies

    for ahead in range(PAGE_SLOTS - 1):
        @pl.when((step == 0) & (ahead < n_total))
        def _(ahead=ahead):
            for c in page_copies_of(ahead):
                c.start()

    @pl.when(step + (PAGE_SLOTS - 1) < n_total)
    def _():
        for c in page_copies_of(step + (PAGE_SLOTS - 1)):
            c.start()

    qpe = qpe_ref[0]
    nr = N_HEADS * tq

    @pl.when(p_id == 0)
    def _():
        lhs_scr[0:GROUP_W, :] = wukt_ref[...]
        lhs_scr[GROUP_W:GROUP_W + nr, :] = _dot_nt(qbd_ref[0], wuk_ref[...]).astype(BF16)

    def scores(ckv, kpe_t):
        ckv_b = ckv.astype(BF16)
        out = _dot_nt(lhs_scr[...], ckv_b)
        rinv = []
        for h in range(N_HEADS):
            kn = out[h * HEAD_DIM:(h + 1) * HEAD_DIM]
            ss = jnp.sum(kn * kn, axis=0, keepdims=True) * (1.0 / HEAD_DIM)
            rinv.append(jnp.broadcast_to(lax.rsqrt(ss + EPS), (tq, ss.shape[1])))
        s = out[GROUP_W:GROUP_W + nr] * jnp.concatenate(rinv, axis=0) + _dot(qpe, kpe_t.astype(BF16))
        return s, ckv_b

    @pl.when(p_id == 0)
    def _():
        s, ckv_b = scores(ckvn_ref[0], kpen_ref[0])
        r = lax.broadcasted_iota(jnp.int32, s.shape, 0) % tq
        cc = lax.broadcasted_iota(jnp.int32, s.shape, 1)
        s = jnp.where(cc <= r, s, NEG)
        m = jnp.max(s, axis=-1, keepdims=True)
        p = jnp.exp(s - m)
        m_scr[...] = m
        l_scr[...] = jnp.sum(p, axis=-1, keepdims=True)
        acc_scr[...] = _dot(p.astype(BF16), ckv_b)

    pltpu.make_async_copy(ckv_buf.at[slot], ckv_buf.at[slot], sem.at[slot, 0]).wait()
    pltpu.make_async_copy(kpe_buf.at[slot], kpe_buf.at[slot], sem.at[slot, 1]).wait()

    n_sub = min(N_SUB, n_pg)
    keys = (n_pg // n_sub) * PAGE
    s_parts, ckv_parts = [], []
    for g in range(n_sub):
        s, ckv_b = scores(ckv_buf[slot, g * keys:(g + 1) * keys, :], kpe_buf[slot, :, g * keys:(g + 1) * keys])
        s_parts.append(s)
        ckv_parts.append(ckv_b)
    m = m_scr[...]
    m_new = m
    for s in s_parts:
        m_new = jnp.maximum(m_new, jnp.max(s, axis=-1, keepdims=True))
    corr = jnp.exp(m - m_new)
    l = l_scr[...] * corr
    acc = acc_scr[...] * corr
    for s, ckv_b in zip(s_parts, ckv_parts):
        p = jnp.exp(s - m_new)
        l = l + jnp.sum(p, axis=-1, keepdims=True)
        acc = acc + _dot(p.astype(BF16), ckv_b)
    m_scr[...] = m_new
    l_scr[...] = l
    acc_scr[...] = acc

    @pl.when(p_id == n_steps - 1)
    def _():
        lat = (acc / l).astype(BF16)
        full = _dot(lat, wuv_ref[...])
        lane_head = lax.broadcasted_iota(jnp.int32, (tq, 256), 1) // HEAD_DIM
        out = jnp.zeros((tq, 256), F32)
        for h in range(N_HEADS):
            out = jnp.where(lane_head == h, full[h * tq:(h + 1) * tq, :], out)
        o_ref[0] = out


def mla_attn_sample(page_table, qbd, qpe, ckv_new, kpe_new_t, wuk, wukt, wuv, cache_ckv, cache_kpe_t, layer):
    bs, n_pages = page_table.shape
    tq = qbd.shape[1] // N_HEADS
    n_pg = min(PAGES_PER_STEP, n_pages)
    assert n_pages % n_pg == 0
    nr = N_HEADS * tq
    per_b = lambda shp: pl.BlockSpec((1,) + shp, lambda b, p, pt: (b, 0, 0))
    cst = lambda shp: pl.BlockSpec(shp, lambda b, p, pt: (0,) * len(shp))
    hbm = pl.BlockSpec(memory_space=pl.ANY)
    grid_spec = pltpu.PrefetchScalarGridSpec(
        num_scalar_prefetch=1,
        grid=(bs, n_pages // n_pg),
        in_specs=[per_b((nr, 256)), per_b((nr, QK_ROPE)), per_b((PAGE, KV_LORA)), per_b((QK_ROPE, PAGE)),
                  cst(wuk.shape), cst(wukt.shape), cst(wuv.shape), hbm, hbm],
        out_specs=pl.BlockSpec((1, tq, 256), lambda b, p, pt: (b, 0, 0)),
        scratch_shapes=[pltpu.VMEM((PAGE_SLOTS, n_pg * PAGE, KV_LORA), F32),
                        pltpu.VMEM((PAGE_SLOTS, QK_ROPE, n_pg * PAGE), F32),
                        pltpu.SemaphoreType.DMA((PAGE_SLOTS, 2)),
                        pltpu.VMEM((nr, 1), F32), pltpu.VMEM((nr, 1), F32), pltpu.VMEM((nr, KV_LORA), F32),
                        pltpu.VMEM((GROUP_W + nr, KV_LORA), BF16)],
    )
    return pl.pallas_call(
        functools.partial(_mla_attn_sample_kernel, n_pg=n_pg, tq=tq, layer=layer),
        grid_spec=grid_spec,
        out_shape=jax.ShapeDtypeStruct((bs, tq, 256), F32),
        compiler_params=_cparams("arbitrary", "arbitrary"),
        name="mla_attn_sample",
    )(page_table, qbd, qpe, ckv_new, kpe_new_t, wuk, wukt, wuv, cache_ckv, cache_kpe_t)


CONV_HALO = 32


def _conv_kernel(pb_ref, st_ref, cw_ref, cb_ref, e_ref, g_ref, b_ref, pw_ref, o_ref, st_out_ref, xbuf, *, tt, n_seq):
    j = pl.program_id(1)
    lo = CONV_HALO - (CONV_W - 1)

    @pl.when(j == 0)
    def _():
        xbuf[:, pl.ds(lo, CONV_W - 1), :] = st_ref[...]

    pb = pb_ref[...]
    glu = pb[:, 0:256] * _sigmoid(pb[:, 256:512])
    ys = []
    for s in range(n_seq):
        xbuf[s, pl.ds(CONV_HALO, tt), :] = glu[s * tt:(s + 1) * tt]
        y = jnp.zeros((tt, 256), F32) + cb_ref[...]
        for k in range(CONV_W):
            y = y + xbuf[s, pl.ds(lo + k, tt), :] * cw_ref[pl.ds(k, 1), :]
        new_state = xbuf[s, pl.ds(lo + tt, CONV_W - 1), :]
        xbuf[s, pl.ds(lo, CONV_W - 1), :] = new_state
        st_out_ref[s] = new_state
        ys.append(y)
    y = jnp.concatenate(ys, axis=0) if n_seq > 1 else ys[0]
    e = e_ref[...]
    yc = y - _seg_sum(y, e)
    yn = yc * lax.rsqrt(_seg_sum(yc * yc, e) + LN_EPS) * g_ref[...] + b_ref[...]
    act = yn * _sigmoid(yn)
    o_ref[...] = _dot(act.astype(BF16), pw_ref[...])


SHORT_SEQ_ROWS = 128


def _seq_tiling(b, t):
    if t >= SHORT_SEQ_ROWS:
        return 1, _tile(t, 256)
    return _tile(b, SHORT_SEQ_ROWS // t), t


def conv_module(pb, state, wts, b, t):
    n_seq, tt = _seq_tiling(b, t)
    nt = t // tt
    return pl.pallas_call(
        functools.partial(_conv_kernel, tt=tt, n_seq=n_seq),
        grid=(b // n_seq, nt),
        in_specs=[pl.BlockSpec((n_seq * tt, 512), lambda bi, j: (bi * nt + j, 0)),
                  pl.BlockSpec((n_seq, CONV_W - 1, 256), lambda bi, j: (bi, 0, 0))]
                 + [_full(w.shape) for w in wts],
        out_specs=[pl.BlockSpec((n_seq * tt, 256), lambda bi, j: (bi * nt + j, 0)),
                   pl.BlockSpec((n_seq, CONV_W - 1, 256), lambda bi, j: (bi, 0, 0))],
        out_shape=[jax.ShapeDtypeStruct((b * t, 256), F32), jax.ShapeDtypeStruct((b, CONV_W - 1, 256), F32)],
        scratch_shapes=[pltpu.VMEM((n_seq, CONV_HALO + tt, 256), F32)],
        compiler_params=_cparams("parallel", "arbitrary"),
        name="conv_module",
    )(pb, state, *wts)


def _sgu_kernel(pc_ref, g_ref, b_ref, w_ref, bias_ref, o_ref, v_ref, *, n_chunks):
    x = pc_ref[...]
    z = 0.5 * x * (1.0 + jnp.tanh(0.7978845608028654 * (x + 0.044715 * (x * x * x))))
    u = z[:, 0:256]
    v = z[:, 256:512]
    vc = v - jnp.mean(v, axis=-1, keepdims=True)
    v = vc * lax.rsqrt(jnp.mean(vc * vc, axis=-1, keepdims=True) + LN_EPS) * g_ref[...] + b_ref[...]
    v_ref[...] = v
    lane_head = lax.broadcasted_iota(jnp.int32, (CHUNK, 256), 1) // HEAD_DIM
    for c in range(n_chunks):
        vcb = v[c * CHUNK:(c + 1) * CHUNK, :]
        sv = bias_ref[...]
        for h in range(N_HEADS):
            sv = sv + _dot(w_ref[h], jnp.where(lane_head == h, vcb, 0.0).astype(BF16))
        o_ref[pl.ds(c * CHUNK, CHUNK), :] = u[c * CHUNK:(c + 1) * CHUNK, :] * sv


def sgu(pc, wts):
    n = pc.shape[0]
    tm = _tile(n, 512)
    row = lambda wd: pl.BlockSpec((tm, wd), lambda i: (i, 0))
    return pl.pallas_call(
        functools.partial(_sgu_kernel, n_chunks=tm // CHUNK),
        grid=(n // tm,),
        in_specs=[row(512)] + [_full(w.shape) for w in wts],
        out_specs=[row(256), row(256)],
        out_shape=[jax.ShapeDtypeStruct((n, 256), F32), jax.ShapeDtypeStruct((n, 256), F32)],
        compiler_params=_cparams("parallel"),
        name="sgu",
    )(pc, *wts)


RW_HALO = 8


def _rwkv_prep_kernel(pd_ref, sh_ref, mu_ref, w0_ref, w2_ref, a0_ref, a2_ref, g2_ref, kkp_ref, ka_ref, rk_ref,
                      e_ref, r_ref, w_ref, k_ref, v_ref, kk_ref, kka_ref, g_ref, bv_ref, xbuf, *, tt, n_seq):
    j = pl.program_id(1)

    @pl.when(j == 0)
    def _():
        xbuf[:, pl.ds(RW_HALO - 1, 1), :] = sh_ref[...]

    pd = pd_ref[...]
    prevs = []
    for s in range(n_seq):
        rows = pd[s * tt:(s + 1) * tt]
        xbuf[s, pl.ds(RW_HALO, tt), :] = rows
        prevs.append(xbuf[s, pl.ds(RW_HALO - 1, tt), :])
        xbuf[s, pl.ds(RW_HALO - 1, 1), :] = rows[tt - 1:tt, :]
    prev = jnp.concatenate(prevs, axis=0) if n_seq > 1 else prevs[0]
    xs = pd + (prev - pd) * mu_ref[...]
    r = xs[:, 0:256]
    k = xs[:, 256:512]
    v = xs[:, 512:768]
    xwa = xs[:, 768:896]
    xg = xs[:, 896:1024]
    z = -(w0_ref[...] + _dot(jnp.tanh(xwa).astype(BF16), w2_ref[...]))
    softplus = jnp.maximum(z, 0.0) + jnp.log(1.0 + jnp.exp(-jnp.abs(z)))
    w_ref[...] = jnp.exp(-jnp.exp(-softplus - 0.5))
    a = _sigmoid(a0_ref[...] + _dot(xwa.astype(BF16), a2_ref[...]))
    g_ref[...] = _dot(_sigmoid(xg).astype(BF16), g2_ref[...])
    e = e_ref[...]
    kk = k * kkp_ref[...]
    kk = kk * lax.rsqrt(_seg_sum(kk * kk, e) + 1e-12)
    k2 = k * (1.0 + (a - 1.0) * ka_ref[...])
    r_ref[...] = r
    k_ref[...] = k2
    v_ref[...] = v
    kk_ref[...] = kk
    kka_ref[...] = kk * a
    bv_ref[...] = _seg_sum(r * k2 * rk_ref[...], e) * v


def rwkv_prep(pd, shift, wts, b, t):
    n_seq, tt = _seq_tiling(b, t)
    nt = t // tt
    row = lambda wd: pl.BlockSpec((n_seq * tt, wd), lambda bi, j: (bi * nt + j, 0))
    return pl.pallas_call(
        functools.partial(_rwkv_prep_kernel, tt=tt, n_seq=n_seq),
        grid=(b // n_seq, nt),
        in_specs=[row(PD_W), pl.BlockSpec((n_seq, 1, PD_W), lambda bi, j: (bi, 0, 0))]
                 + [_full(w.shape) for w in wts],
        out_specs=[row(256)] * 8,
        out_shape=[jax.ShapeDtypeStruct((b * t, 256), F32)] * 8,
        scratch_shapes=[pltpu.VMEM((n_seq, RW_HALO + tt, PD_W), F32)],
        compiler_params=_cparams("parallel", "arbitrary"),
        name="rwkv_prep",
    )(pd, shift, *wts)


RW_BLOCK = 128


def _pack_bf16_pair(a, b):
    ua = lax.bitcast_convert_type(a.astype(BF16).astype(F32), jnp.uint32)
    ub = lax.bitcast_convert_type(b.astype(BF16).astype(F32), jnp.uint32)
    return lax.bitcast_convert_type(ua | (ub >> 16), jnp.int32)


def _unpack_bf16_pair(word):
    u = lax.bitcast_convert_type(word, jnp.uint32)
    return (lax.bitcast_convert_type(u & jnp.uint32(0xFFFF0000), F32),
            lax.bitcast_convert_type(u << 16, F32))


def _rwkv_scan_kernel(w_ref, kk_ref, kka_ref, k_ref, r_ref, v_ref, s0_ref, e_ref, y_ref, sf_ref, s_scr, col_scr, *,
                      nb, sblk):
    c = pl.program_id(1)

    @pl.when(c == 0)
    def _():
        s_scr[...] = s0_ref[...]

    lane = lax.broadcasted_iota(jnp.int32, (HEAD_DIM, LANES), 1)
    low = lane < HEAD_DIM
    n_half = 2 if sblk > HEAD_DIM else 1
    for b in range(nb):
        for p in range(2):
            blk = lambda ref: ref[b, :, p * LANES:(p + 1) * LANES]
            x = _pack_bf16_pair(blk(kk_ref), blk(kka_ref))
            if sblk < LANES:
                x = jnp.concatenate([x, jnp.zeros((LANES - sblk, LANES), jnp.int32)], axis=0)
            xt = x.T
            h0, h1 = xt[0:HEAD_DIM], xt[HEAD_DIM:LANES]
            col_scr[b, p, 0] = jnp.where(low, h0, pltpu.roll(h1, HEAD_DIM, 1))
            if n_half == 2:
                col_scr[b, p, 1] = jnp.where(low, pltpu.roll(h0, HEAD_DIM, 1), h1)

    base = jnp.where(low, 0, HEAD_DIM)
    diag = (lax.broadcasted_iota(jnp.int32, (HEAD_DIM, 256), 1) % HEAD_DIM
            == lax.broadcasted_iota(jnp.int32, (HEAD_DIM, 256), 0))
    for half in range(n_half):
        n_groups = min(sblk - half * HEAD_DIM, HEAD_DIM) // SUBLANES

        def group(gi, carry, half=half):
            row0 = pl.multiple_of(half * HEAD_DIM + gi * SUBLANES, SUBLANES)
            for b in range(nb):
                v8 = v_ref[b, pl.ds(row0, SUBLANES), :]
                w8 = w_ref[b, pl.ds(row0, SUBLANES), :]
                w8_hi = w8.astype(BF16).astype(F32)
                rows8 = (k_ref[b, pl.ds(row0, SUBLANES), :], r_ref[b, pl.ds(row0, SUBLANES), :], w8_hi, w8 - w8_hi)
                on_diag = [jnp.where(diag, jnp.broadcast_to(x8[i:i + 1, :], (HEAD_DIM, 256)), 0.0).astype(BF16)
                           for i in range(SUBLANES) for x8 in rows8]
                spread = _dot(jnp.concatenate(on_diag, axis=0), e_ref[...])
                ys = []
                for p in range(2):
                    st = s_scr[b, :, p * LANES:(p + 1) * LANES]
                    yp = []
                    for i in range(SUBLANES):
                        idx = base + (gi * SUBLANES + i)
                        kk, kka = _unpack_bf16_pair(jnp.take_along_axis(col_scr[b, p, half], idx, axis=1))
                        part = lambda n: spread[(4 * i + n) * HEAD_DIM:(4 * i + n + 1) * HEAD_DIM,
                                                p * LANES:(p + 1) * LANES]
                        k, r, w = part(0), part(1), part(2) + part(3)
                        vrow = v8[i:i + 1, p * LANES:(p + 1) * LANES]
                        sa = -jnp.sum(st * kk, axis=0, keepdims=True)
                        st = st * w + kka * sa + k * vrow
                        yp.append(jnp.sum(st * r, axis=0, keepdims=True))
                    s_scr[b, :, p * LANES:(p + 1) * LANES] = st
                    ys.append(jnp.concatenate(yp, axis=0))
                y_ref[b, pl.ds(row0, SUBLANES), :] = jnp.concatenate(ys, axis=1)
            return carry

        lax.fori_loop(0, n_groups, group, 0)

    @pl.when(c == pl.num_programs(1) - 1)
    def _():
        sf_ref[...] = s_scr[...]


def rwkv_scan(seqs, v, s0, nb):
    b, t, _ = v.shape
    e_heads = _head_matrix(1.0)
    sblk = min(t, RW_BLOCK)
    assert b % nb == 0 and t % sblk == 0 and sblk % SUBLANES == 0 and (sblk <= HEAD_DIM or sblk == RW_BLOCK)
    st_spec = pl.BlockSpec((nb, HEAD_DIM, 256), lambda bi, c: (bi, 0, 0))
    seq_spec = pl.BlockSpec((nb, sblk, 256), lambda bi, c: (bi, c, 0))
    return pl.pallas_call(
        functools.partial(_rwkv_scan_kernel, nb=nb, sblk=sblk),
        grid=(b // nb, t // sblk),
        in_specs=[seq_spec] * 6 + [st_spec, _full(e_heads.shape)],
        out_specs=[seq_spec, st_spec],
        out_shape=[jax.ShapeDtypeStruct((b, t, 256), F32), jax.ShapeDtypeStruct((b, HEAD_DIM, 256), F32)],
        scratch_shapes=[pltpu.VMEM((nb, HEAD_DIM, 256), F32),
                        pltpu.VMEM((nb, 2, 2, HEAD_DIM, LANES), jnp.int32)],
        compiler_params=_cparams("parallel", "arbitrary"),
        name="rwkv_scan",
    )(*seqs, v, s0, e_heads)


def _mix_out_kernel(x_ref, oa_ref, ob_ref, oc_ref, y_ref, bv_ref, g_ref, e_ref, lg_ref, lb_ref, on_ref, w_ref,
                    nx_ref, wq_ref, qg_ref, o_ref, q_ref):
    e = e_ref[...]
    y = y_ref[...]
    yc = y - _seg_sum(y, e)
    yn = yc * lax.rsqrt(_seg_sum(yc * yc, e) + RW_LN_EPS) * lg_ref[...] + lb_ref[...]
    od = (yn + bv_ref[...]) * g_ref[...]
    acc = x_ref[...]
    for gi, o in enumerate((oa_ref[...], ob_ref[...], oc_ref[...], od)):
        on = (_rms(o) * on_ref[:, gi * 256:(gi + 1) * 256]).astype(BF16)
        acc = acc + _dot(on, w_ref[pl.ds(gi * 256, 256), :])
    o_ref[...] = acc
    q = _dot((_rms(acc) * nx_ref[...]).astype(BF16), wq_ref[...])
    heads = [_rms(q[:, h * X_HEAD_DIM:(h + 1) * X_HEAD_DIM]) * qg_ref[...] * X_HEAD_DIM ** -0.5
             for h in range(X_HEADS)]
    q_ref[...] = jnp.concatenate(heads, axis=1).astype(q_ref.dtype)


def mix_out(x, oa, ob, oc, y, bv, g, wts, q_dtype):
    n, d = x.shape
    tm = _tile(n, 512)
    row = lambda wd: pl.BlockSpec((tm, wd), lambda i: (i, 0))
    return pl.pallas_call(
        _mix_out_kernel,
        grid=(n // tm,),
        in_specs=[row(d)] + [row(256)] * 6 + [_full(w.shape) for w in wts],
        out_specs=[row(d), row(512)],
        out_shape=[jax.ShapeDtypeStruct((n, d), F32), jax.ShapeDtypeStruct((n, 512), q_dtype)],
        compiler_params=_cparams("parallel"),
        name="mix_out",
    )(x, oa, ob, oc, y, bv, g, *wts)


XATTN_SHORT_ROWS = 64


def _xattn_kernel(x_ref, q_ref, k_ref, v_ref, wo_ref, o_ref, *, n_seq, rows, n_mem):
    seq_outs = []
    for s in range(n_seq):
        qs = [q_ref[s * rows:(s + 1) * rows, h * X_HEAD_DIM:(h + 1) * X_HEAD_DIM].astype(BF16)
              for h in range(X_HEADS)]
        if n_seq == 1:
            outs = []
            for h in range(X_HEADS):
                head_rows = pl.ds(h, n_mem, stride=X_HEADS)
                sc = _dot_nt(qs[h], k_ref[s, head_rows, :].astype(BF16))
                p = jnp.exp(sc - jnp.max(sc, axis=-1, keepdims=True))
                p = p / jnp.sum(p, axis=-1, keepdims=True)
                outs.append(_dot(p.astype(BF16), v_ref[s, head_rows, :].astype(BF16)))
        else:
            sc = _dot_nt(jnp.concatenate(qs, axis=0), k_ref[s].astype(BF16))
            row_head = lax.broadcasted_iota(jnp.int32, sc.shape, 0) // rows
            col_head = lax.broadcasted_iota(jnp.int32, sc.shape, 1) % X_HEADS
            sc = jnp.where(row_head == col_head, sc, NEG)
            p = jnp.exp(sc - jnp.max(sc, axis=-1, keepdims=True))
            p = p / jnp.sum(p, axis=-1, keepdims=True)
            o = _dot(p.astype(BF16), v_ref[s].astype(BF16))
            outs = [o[h * rows:(h + 1) * rows] for h in range(X_HEADS)]
        seq_outs.append(jnp.concatenate(outs, axis=1))
    xo = jnp.concatenate(seq_outs, axis=0).astype(BF16)
    o_ref[...] = x_ref[...] + _dot(xo, wo_ref[...])


def xattn(x, q, mem_k, mem_v, wo, b, t, kv_seq0):
    n, d = x.shape
    n_mem = mem_k.shape[1] // X_HEADS
    if t >= XATTN_SHORT_ROWS:
        n_seq, rows = 1, _tile(t, 512)
    else:
        n_seq, rows = _tile(b, XATTN_SHORT_ROWS // t), t
    nq = t // rows
    assert kv_seq0 % n_seq == 0
    row = lambda wd: pl.BlockSpec((n_seq * rows, wd), lambda bi, i: (bi * nq + i, 0))
    kv = pl.BlockSpec((n_seq, n_mem * X_HEADS, X_HEAD_DIM), lambda bi, i: (kv_seq0 // n_seq + bi, 0, 0))
    return pl.pallas_call(
        functools.partial(_xattn_kernel, n_seq=n_seq, rows=rows, n_mem=n_mem),
        grid=(b // n_seq, nq),
        in_specs=[row(d), row(512), kv, kv, _full(wo.shape)],
        out_specs=row(d),
        out_shape=jax.ShapeDtypeStruct((n, d), F32),
        compiler_params=_cparams("parallel", "arbitrary"),
        name="xattn",
    )(x, q, mem_k, mem_v, wo)


def _mem_kv_kernel(x_ref, g_ref, w_ref, hg_ref, k_ref, v_ref, *, tm):
    hb = (_rms(x_ref[...]) * g_ref[...]).astype(BF16)
    y = _dot(hb, w_ref[...])
    for h in range(X_HEADS):
        head_rows = pl.ds(h, tm, stride=X_HEADS)
        k_ref[head_rows, :] = _rms(y[:, h * X_HEAD_DIM:(h + 1) * X_HEAD_DIM]) * hg_ref[...]
        v_ref[head_rows, :] = y[:, (X_HEADS + h) * X_HEAD_DIM:(X_HEADS + h + 1) * X_HEAD_DIM]


def mem_kv(x, g, w, hg):
    n, d = x.shape
    tm = _tile(n, 512)
    out = pl.BlockSpec((tm * X_HEADS, X_HEAD_DIM), lambda i: (i, 0))
    return pl.pallas_call(
        functools.partial(_mem_kv_kernel, tm=tm),
        grid=(n // tm,),
        in_specs=[pl.BlockSpec((tm, d), lambda i: (i, 0)), _full(g.shape), _full(w.shape), _full(hg.shape)],
        out_specs=[out, out],
        out_shape=[jax.ShapeDtypeStruct((n * X_HEADS, X_HEAD_DIM), F32)] * 2,
        compiler_params=_cparams("parallel"),
        name="mem_kv",
    )(x, g, w, hg)


FFN_CHUNK = 256


def _ffn_kernel(x_ref, g_ref, wg_ref, wu_ref, wo_ref, o_ref, *, n_chunks):
    x = x_ref[...]
    hb = (_rms(x) * g_ref[...]).astype(BF16)
    acc = x
    for c in range(n_chunks):
        a = _dot(hb, wg_ref[c])
        u = _dot(hb, wu_ref[c])
        acc = acc + _dot((a * _sigmoid(a) * u).astype(BF16), wo_ref[c])
    o_ref[...] = acc


def ffn(x, g, wg, wu, wo):
    n, d = x.shape
    tm = _tile(n, 512)
    row = pl.BlockSpec((tm, d), lambda i: (i, 0))
    return pl.pallas_call(
        functools.partial(_ffn_kernel, n_chunks=wg.shape[0]),
        grid=(n // tm,),
        in_specs=[row, _full(g.shape), _full(wg.shape), _full(wu.shape), _full(wo.shape)],
        out_specs=row,
        out_shape=jax.ShapeDtypeStruct((n, d), F32),
        compiler_params=_cparams("parallel"),
        name="ffn",
    )(x, g, wg, wu, wo)


def _seg_matrix(seg_ids, seg_len):
    s = np.asarray(seg_ids)
    m = (s[:, None] == s[None, :]) & (s[:, None] >= 0)
    return jnp.asarray(m.astype(np.float32) / np.asarray(seg_len, np.float32)[None, :], BF16)


def _mla_segments():
    lane = np.arange(512)
    blk, off = lane // 128, lane % 128
    q_ids = np.where(off < 32, 2 * blk, np.where(off < 64, -1, 2 * blk + 1))
    q_len = np.where(off < 32, 32.0, 64.0)
    k_ids = np.where(off < 64, -1, blk)
    k_len = np.full(512, 64.0)
    return _seg_matrix(q_ids, q_len), _seg_matrix(k_ids, k_len)


def _head_matrix(scale_len):
    lane = np.arange(256)
    return _seg_matrix(lane // HEAD_DIM, np.full(256, scale_len))


def _rope_tables(pos):
    half = QK_ROPE // 2
    inv = jnp.power(ROPE_BASE, -jnp.arange(half, dtype=F32) / half)
    ang = pos.astype(F32)[:, None] * inv[None, :]
    cos, sin = jnp.cos(ang), jnp.sin(ang)
    n = pos.shape[0]
    z = lambda w: jnp.zeros((n, w), F32)
    c = jnp.concatenate([cos, cos, jnp.ones((n, LANES - QK_ROPE), F32)], axis=1)
    sa = jnp.concatenate([-sin, z(LANES - half)], axis=1)
    sb = jnp.concatenate([z(half), sin, z(LANES - QK_ROPE)], axis=1)
    return c, sa, sb


def _layer_weights(p):
    row = lambda v: v.reshape(1, -1).astype(F32)
    zc = lambda a, w: jnp.zeros((a.shape[0], w), a.dtype)
    w_in = p['w_in']
    c1, c2, c3 = 352, 352 + 512, 352 + 1024
    pa = w_in[:, :c1]
    w_in_p = jnp.concatenate(
        [pa[:, :Q_LORA], zc(pa, 64), pa[:, Q_LORA:Q_LORA + KV_LORA], pa[:, Q_LORA + KV_LORA:], zc(pa, 96),
         w_in[:, c1:c2], w_in[:, c2:c3], w_in[:, c3:]], axis=1).astype(BF16)

    wuq = p['mla_w_uq'].reshape(Q_LORA, N_HEADS, QK_NOPE + QK_ROPE)
    wuq = jnp.concatenate([wuq[:, :, QK_NOPE:], jnp.zeros((Q_LORA, N_HEADS, 32), F32), wuq[:, :, :QK_NOPE]], axis=2)
    wuq = jnp.concatenate([wuq.reshape(Q_LORA, 512), jnp.zeros((256 - Q_LORA, 512), F32)], axis=0).astype(BF16)
    wuk = p['mla_w_uk'].reshape(KV_LORA, N_HEADS, QK_NOPE)
    wuk_p = jnp.concatenate([jnp.zeros((KV_LORA, N_HEADS, 64), F32), wuk], axis=2).reshape(KV_LORA, 512).astype(BF16)
    blk = lambda a, b_, c_: jnp.tile(jnp.concatenate([a, b_, c_]), N_HEADS).reshape(1, 512)
    z32, z64 = jnp.zeros((32,), F32), jnp.zeros((64,), F32)
    gq = blk(p['mla_gq_rope'], z32, p['mla_gq_nope'])
    gk = blk(z32, z32, p['mla_gk_nope'])
    gqn = jnp.concatenate([p['mla_q_norm'], z64]).reshape(1, 256)
    gkr = jnp.concatenate([p['mla_gk_rope'], jnp.zeros((96,), F32)]).reshape(1, LANES)
    eq, ek = _mla_segments()
    mla = (gqn, row(p['mla_kv_norm']), wuq, eq, gq, gkr, wuk_p, ek, gk, p['mla_w_uv'].astype(BF16))

    e64 = _head_matrix(64.0)
    conv = (p['conv_w'], row(p['conv_b']), e64, row(p['conv_norm_g']), row(p['conv_norm_b']),
            p['conv_pw'].astype(BF16))

    z64r = jnp.zeros((64, GROUP_W), F32)
    rw = (row(p['rw_mu']),
          row(p['rw_w0']), jnp.concatenate([p['rw_w2'], z64r], axis=0).astype(BF16),
          row(p['rw_a0']), jnp.concatenate([z64r, p['rw_a2']], axis=0).astype(BF16),
          p['rw_g2'].astype(BF16), row(p['rw_kk']), row(p['rw_ka']), row(p['rw_rk']), _head_matrix(1.0))

    mix = (e64, row(p['rw_ln_g']), row(p['rw_ln_b']), row(p['out_norm']), p['w_out'].astype(BF16))

    d_ff = p['w_ffn_out'].shape[0]
    nck = d_ff // FFN_CHUNK
    d = w_in.shape[0]
    wg = p['w_ffn_in'][:, :d_ff].reshape(d, nck, FFN_CHUNK).transpose(1, 0, 2).astype(BF16)
    wu = p['w_ffn_in'][:, d_ff:].reshape(d, nck, FFN_CHUNK).transpose(1, 0, 2).astype(BF16)
    wo = p['w_ffn_out'].reshape(nck, FFN_CHUNK, d).astype(BF16)

    return dict(
        norm_mix=row(p['norm_mix']), w_in=w_in_p, mla=mla, conv=conv, rw=rw, mix=mix,
        wuk=p['mla_w_uk'].astype(BF16), wuv=p['mla_w_uv'].astype(BF16), gk_nope=p['mla_gk_nope'],
        sgu_ln=(row(p['sgu_norm_g']), row(p['sgu_norm_b'])), sgu_w=p['sgu_w'], sgu_b=p['sgu_b'],
        norm_x=row(p['norm_x']), wq=p['wq_x'].astype(BF16), xq_norm=row(p['xq_norm']),
        mem_norm=row(p['mem_norm']), wkv=jnp.concatenate([p['wk_x'], p['wv_x']], axis=1).astype(BF16),
        xk_norm=row(p['xk_norm']), wo_x=p['wo_x'].astype(BF16),
        norm_ffn=row(p['norm_ffn']), wg=wg, wu=wu, wo=wo)


def _sgu_weights(lw, t):
    l = min(t, CHUNK)
    w = lw['sgu_w'][:, :l, :l] * jnp.tril(jnp.ones((l, l), F32))
    reps = CHUNK // l
    if reps > 1:
        w = jnp.einsum('ab,hij->haibj', jnp.eye(reps, dtype=F32), w).reshape(N_HEADS, CHUNK, CHUNK)
    bias = jnp.tile(lw['sgu_b'][:, :l].T, (reps, 1))
    bias = jnp.repeat(bias, HEAD_DIM, axis=1)
    return lw['sgu_ln'] + (w.astype(BF16), bias)


def _trunk_layer(x, b, t, lw, tabs, n_tab_blocks, attend, mem_k, mem_v, kv_seq0, conv_state, shift_state, wkv_state,
                 scan_nb):
    pa, pb, pc, pd = proj_in(x, lw['norm_mix'], lw['w_in'])
    q, k, v, ckv, kpe = mla_prep(pa, tabs, lw['mla'], n_tab_blocks)
    oa = attend(q, k, v, ckv, kpe)
    ob, conv_new = conv_module(pb, conv_state, lw['conv'], b, t)
    oc, v_sgu = sgu(pc, _sgu_weights(lw, t))
    r, w, k2, vv, kk, kka, g, bv = rwkv_prep(pd, shift_state.reshape(b, 1, PD_W), lw['rw'], b, t)
    seqs = [a.reshape(b, t, 256) for a in (w, kk, kka, k2, r)]
    s0 = wkv_state.transpose(0, 3, 1, 2).reshape(b, HEAD_DIM, 256)
    y, s_fin = rwkv_scan(seqs, vv.reshape(b, t, 256), s0, scan_nb)
    wkv_new = s_fin.reshape(b, HEAD_DIM, N_HEADS, HEAD_DIM).transpose(0, 2, 3, 1)
    shift_new = pd.reshape(b, t, PD_W)[:, -1]
    q_dtype = BF16 if t % 16 == 0 else F32
    x, qx = mix_out(x, oa, ob, oc, y.reshape(b * t, 256), bv, g,
                    lw['mix'] + (lw['norm_x'], lw['wq'], lw['xq_norm']), q_dtype)
    x = xattn(x, qx, mem_k, mem_v, lw['wo_x'], b, t, kv_seq0)
    x = ffn(x, lw['norm_ffn'], lw['wg'], lw['wu'], lw['wo'])
    return x, ckv, kpe, conv_new, shift_new, wkv_new, v_sgu


def kernel(x_prompt, x_sample, mem_prompt, cache_ckv, cache_kpe, cache_mem_k, cache_mem_v, state_conv, state_shift, state_wkv, page_table, norm_mix, w_in, mla_q_norm, mla_kv_norm, mla_w_uq, mla_w_uk, mla_w_uv, mla_gq_nope, mla_gq_rope, mla_gk_nope, mla_gk_rope, conv_w, conv_b, conv_norm_g, conv_norm_b, conv_pw, sgu_norm_g, sgu_norm_b, sgu_w, sgu_b, rw_mu, rw_w0, rw_w2, rw_a0, rw_a2, rw_g2, rw_kk, rw_ka, rw_rk, rw_ln_g, rw_ln_b, out_norm, w_out, norm_x, mem_norm, wq_x, wk_x, wv_x, xq_norm, xk_norm, wo_x, norm_ffn, w_ffn_in, w_ffn_out):
    params = dict(
        norm_mix=norm_mix, w_in=w_in, mla_q_norm=mla_q_norm, mla_kv_norm=mla_kv_norm, mla_w_uq=mla_w_uq,
        mla_w_uk=mla_w_uk, mla_w_uv=mla_w_uv, mla_gq_nope=mla_gq_nope, mla_gq_rope=mla_gq_rope,
        mla_gk_nope=mla_gk_nope, mla_gk_rope=mla_gk_rope, conv_w=conv_w, conv_b=conv_b, conv_norm_g=conv_norm_g,
        conv_norm_b=conv_norm_b, conv_pw=conv_pw, sgu_norm_g=sgu_norm_g, sgu_norm_b=sgu_norm_b, sgu_w=sgu_w,
        sgu_b=sgu_b, rw_mu=rw_mu, rw_w0=rw_w0, rw_w2=rw_w2, rw_a0=rw_a0, rw_a2=rw_a2, rw_g2=rw_g2, rw_kk=rw_kk,
        rw_ka=rw_ka, rw_rk=rw_rk, rw_ln_g=rw_ln_g, rw_ln_b=rw_ln_b, out_norm=out_norm, w_out=w_out, norm_x=norm_x,
        mem_norm=mem_norm, wq_x=wq_x, wk_x=wk_x, wv_x=wv_x, xq_norm=xq_norm, xk_norm=xk_norm, wo_x=wo_x,
        norm_ffn=norm_ffn, w_ffn_in=w_ffn_in, w_ffn_out=w_ffn_out)
    depth = w_in.shape[0]
    bp, tp, d = x_prompt.shape
    bs, ts, _ = x_sample.shape
    n_mem = mem_prompt.shape[1]
    n_pages = page_table.shape[1]
    past_len = n_pages * PAGE

    tm_p = _tile(tp, 512)
    tabs_p = _rope_tables(jnp.arange(tp, dtype=jnp.int32))
    tm_s = _tile(bs * ts, 512)
    tabs_s = tuple(jnp.tile(a, (tm_s // ts, 1)) for a in _rope_tables(past_len + jnp.arange(ts, dtype=jnp.int32)))
    cache_kpe_t = jnp.swapaxes(cache_kpe, 2, 3)
    cache_k_rows = cache_mem_k.reshape(depth * bs, n_mem * X_HEADS, X_HEAD_DIM)
    cache_v_rows = cache_mem_v.reshape(depth * bs, n_mem * X_HEADS, X_HEAD_DIM)
    y_p = x_prompt.reshape(bp * tp, d)
    y_s = x_sample.reshape(bs * ts, d)
    mem_flat = mem_prompt.reshape(bp * n_mem, d)
    zeros_conv = jnp.zeros((bp, CONV_W - 1, GROUP_W), F32)
    zeros_shift = jnp.zeros((bp, PD_W), F32)
    zeros_wkv = jnp.zeros((bp, N_HEADS, HEAD_DIM, HEAD_DIM), F32)
    outs_p, outs_s, memk_l, memv_l = [], [], [], []
    for l in range(depth):
        lw = _layer_weights({k_: v_[l] for k_, v_ in params.items()})

        mk, mv = mem_kv(mem_flat, lw['mem_norm'], lw['wkv'], lw['xk_norm'])
        memk_l.append(mk.reshape(bp, n_mem, X_HEADS, X_HEAD_DIM))
        memv_l.append(mv.reshape(bp, n_mem, X_HEADS, X_HEAD_DIM))

        def attend_p(q, k, v, ckv, kpe):
            return mla_attn_prompt(q, k, v, bp, tp)

        kv_rows = n_mem * X_HEADS
        res = _trunk_layer(y_p, bp, tp, lw, tabs_p, tp // tm_p, attend_p, mk.reshape(bp, kv_rows, X_HEAD_DIM),
                           mv.reshape(bp, kv_rows, X_HEAD_DIM), 0, zeros_conv, zeros_shift, zeros_wkv,
                           _tile(bp, SCAN_NB_PROMPT if l == 0 else 8))
        y_p = res[0]
        outs_p.append(res[1:])

        def attend_s(q, k, v, ckv, kpe, l=l, lw=lw):
            qf = q.astype(F32).reshape(bs, ts, N_HEADS, LANES)
            qn = (qf[..., 64:] * lw['gk_nope']).transpose(0, 2, 1, 3)
            qbd = jnp.einsum('bhtj,hg->bhtgj', qn, jnp.eye(N_HEADS, dtype=F32)).reshape(bs, N_HEADS * ts, 256)
            qpe = qf[..., :QK_ROPE].transpose(0, 2, 1, 3).reshape(bs, N_HEADS * ts, QK_ROPE)
            pad = lambda a: jnp.pad(a.reshape(bs, ts, -1), ((0, 0), (0, PAGE - ts), (0, 0)))
            o = mla_attn_sample(page_table, qbd.astype(BF16), qpe.astype(BF16), pad(ckv),
                                pad(kpe).transpose(0, 2, 1), lw['wuk'], lw['wuk'].T, lw['wuv'], cache_ckv,
                                cache_kpe_t, l)
            return o.reshape(bs * ts, 256)

        res = _trunk_layer(y_s, bs, ts, lw, tabs_s, 1, attend_s, cache_k_rows, cache_v_rows, l * bs,
                           state_conv[l], state_shift[l], state_wkv[l], _tile(bs, SCAN_NB_SAMPLE))
        y_s = res[0]
        outs_s.append(res[1:])

    n_pp = tp // PAGE
    stack = lambda outs, i, ax: jnp.stack([o[i] for o in outs], axis=ax)
    ckv_prompt = stack(outs_p, 0, 0).reshape(depth, bp, n_pp, PAGE, KV_LORA).transpose(1, 2, 0, 3, 4)
    kpe_prompt = stack(outs_p, 1, 0).reshape(depth, bp, n_pp, PAGE, QK_ROPE).transpose(1, 2, 0, 3, 4)
    ckv_sample = stack(outs_s, 0, 0).reshape(depth, bs, ts, KV_LORA).transpose(1, 0, 2, 3)
    kpe_sample = stack(outs_s, 1, 0).reshape(depth, bs, ts, QK_ROPE).transpose(1, 0, 2, 3)
    return (y_p.reshape(bp, tp, d), y_s.reshape(bs, ts, d), ckv_prompt, kpe_prompt, ckv_sample, kpe_sample,
            jnp.stack(memk_l, 0), jnp.stack(memv_l, 0),
            stack(outs_p, 2, 0), stack(outs_s, 2, 0), stack(outs_p, 3, 0), stack(outs_s, 3, 0),
            stack(outs_p, 4, 0), stack(outs_s, 4, 0),
            stack(outs_s, 5, 0).reshape(depth, bs, ts, GROUP_W))
```

```python
import functools

import numpy as np
import jax
import jax.numpy as jnp
from jax import lax
from jax.experimental import pallas as pl
from jax.experimental.pallas import tpu as pltpu

F32 = jnp.float32
BF16 = jnp.bfloat16

EPS = 1e-6
LN_EPS = 1e-5
RW_LN_EPS = 64e-5
NEG = -1e30
ROPE_BASE = 10000.0

LANES = 128
SUBLANES = 8
VMEM_LIMIT_BYTES = 56 * 1024 * 1024

GROUP_W = 256
HEAD_DIM = 64
N_HEADS = 4
Q_LORA = 192
KV_LORA = 128
QK_ROPE = 32
QK_NOPE = 64
CONV_W = 31
CHUNK = 128
PAGE = 128
X_HEADS = 4
X_HEAD_DIM = 128
MLA_SCALE = (QK_NOPE + QK_ROPE) ** -0.5
PA_W = 512
PD_W = 1024
PAGES_PER_STEP = 32
SCAN_NB_PROMPT = 8
SCAN_NB_SAMPLE = 8


def _cparams(*sem):
    return pltpu.CompilerParams(dimension_semantics=sem, vmem_limit_bytes=VMEM_LIMIT_BYTES)


def _dot(a, b):
    return jnp.dot(a, b, preferred_element_type=F32)


def _dot_nt(a, b):
    return lax.dot_general(a, b, (((1,), (1,)), ((), ())), preferred_element_type=F32)


def _seg_sum(x, e):
    hi = x.astype(BF16)
    lo = (x - hi.astype(F32)).astype(BF16)
    return _dot(hi, e) + _dot(lo, e)


def _rms(x, width=None):
    w = x.shape[-1] if width is None else width
    return x * lax.rsqrt(jnp.sum(x * x, axis=-1, keepdims=True) * (1.0 / w) + EPS)


def _sigmoid(x):
    return 1.0 / (1.0 + jnp.exp(-x))


def _tile(n, pref):
    t = min(n, pref)
    while n % t:
        t //= 2
    return t


def _full(shape):
    nd = len(shape)
    return pl.BlockSpec(shape, lambda *a: (0,) * nd)


def _proj_in_kernel(x_ref, g_ref, w_ref, pa_ref, pb_ref, pc_ref, pd_ref):
    hb = (_rms(x_ref[...]) * g_ref[...]).astype(BF16)
    pa_ref[...] = _dot(hb, w_ref[:, 0:512])
    pb_ref[...] = _dot(hb, w_ref[:, 512:1024])
    pc_ref[...] = _dot(hb, w_ref[:, 1024:1536])
    pd_ref[...] = _dot(hb, w_ref[:, 1536:2560])


def proj_in(x, g, w):
    n, d = x.shape
    tm = _tile(n, 512)
    row = lambda wd: pl.BlockSpec((tm, wd), lambda i: (i, 0))
    return pl.pallas_call(
        _proj_in_kernel,
        grid=(n // tm,),
        in_specs=[row(d), _full(g.shape), _full(w.shape)],
        out_specs=[row(512), row(512), row(512), row(1024)],
        out_shape=[jax.ShapeDtypeStruct((n, wd), F32) for wd in (512, 512, 512, 1024)],
        compiler_params=_cparams("parallel"),
        name="proj_in",
    )(x, g, w)


def _rope128(x, c, sa, sb):
    w = x.shape[-1]
    return x * c + pltpu.roll(x, w - 16, 1) * sa + pltpu.roll(x, 16, 1) * sb


def _mla_prep_kernel(pa_ref, c_ref, sa_ref, sb_ref, gqn_ref, gkv_ref, wuq_ref, eq_ref, gq_ref,
                     gkr_ref, wuk_ref, ek_ref, gk_ref, wuv_ref,
                     q_ref, k_ref, v_ref, ckv_ref, kpe_ref):
    pa = pa_ref[...]
    c, sa, sb = c_ref[...], sa_ref[...], sb_ref[...]
    c4 = jnp.concatenate([c] * 4, axis=1)
    sa4 = jnp.concatenate([sa] * 4, axis=1)
    sb4 = jnp.concatenate([sb] * 4, axis=1)
    cq = _rms(pa[:, 0:256], Q_LORA) * gqn_ref[...]
    q = _dot(cq.astype(BF16), wuq_ref[...])
    qn = q * lax.rsqrt(_seg_sum(q * q, eq_ref[...]) + EPS) * gq_ref[...]
    q_ref[...] = (_rope128(qn, c4, sa4, sb4) * MLA_SCALE).astype(BF16)
    ckv = _rms(pa[:, 256:384]) * gkv_ref[...]
    ckv_ref[...] = ckv
    kp = _rms(pa[:, 384:512], QK_ROPE) * gkr_ref[...]
    kr = _rope128(kp, c, sa, sb)
    kpe_ref[...] = kr[:, 0:QK_ROPE]
    ckv_b = ckv.astype(BF16)
    kn = _dot(ckv_b, wuk_ref[...])
    kn = kn * lax.rsqrt(_seg_sum(kn * kn, ek_ref[...]) + EPS) * gk_ref[...]
    k_ref[...] = (kn + jnp.concatenate([kr] * 4, axis=1)).astype(BF16)
    v_ref[...] = _dot(ckv_b, wuv_ref[...]).astype(BF16)


def mla_prep(pa, tabs, wts, n_tab_blocks):
    n = pa.shape[0]
    c, sa, sb = tabs
    tm = c.shape[0] // n_tab_blocks
    assert n % tm == 0
    row = lambda wd: pl.BlockSpec((tm, wd), lambda i: (i, 0))
    tab = pl.BlockSpec((tm, LANES), lambda i: (i % n_tab_blocks, 0))
    return pl.pallas_call(
        _mla_prep_kernel,
        grid=(n // tm,),
        in_specs=[row(PA_W), tab, tab, tab] + [_full(w.shape) for w in wts],
        out_specs=[row(512), row(512), row(256), row(KV_LORA), row(QK_ROPE)],
        out_shape=[jax.ShapeDtypeStruct((n, 512), BF16), jax.ShapeDtypeStruct((n, 512), BF16),
                   jax.ShapeDtypeStruct((n, 256), BF16), jax.ShapeDtypeStruct((n, KV_LORA), F32),
                   jax.ShapeDtypeStruct((n, QK_ROPE), F32)],
        compiler_params=_cparams("parallel"),
        name="mla_prep",
    )(pa, c, sa, sb, *wts)


def _mla_attn_prompt_kernel(q_ref, k_ref, v_ref, o_ref, *, tq):
    i = pl.program_id(1)
    outs = []
    for h0 in range(0, N_HEADS, 2):
        heads = (h0, h0 + 1)

        def blk(j, carry, masked, heads=heads):
            off = pl.multiple_of(j * tq, tq)
            new = []
            for n, h in enumerate(heads):
                m, l, acc = carry[3 * n:3 * n + 3]
                kb = k_ref[pl.ds(off, tq), h * LANES:(h + 1) * LANES]
                vb = v_ref[pl.ds(off, tq), h * HEAD_DIM:(h + 1) * HEAD_DIM]
                s = _dot_nt(q_ref[:, h * LANES:(h + 1) * LANES], kb)
                if masked:
                    r = lax.broadcasted_iota(jnp.int32, (tq, tq), 0)
                    cc = lax.broadcasted_iota(jnp.int32, (tq, tq), 1)
                    s = jnp.where(cc <= r, s, NEG)
                m_new = jnp.maximum(m, jnp.max(s, axis=-1, keepdims=True))
                corr = jnp.exp(m - m_new)
                p = jnp.exp(s - m_new)
                l = l * corr + jnp.sum(p, axis=-1, keepdims=True)
                acc = acc * corr + _dot(p.astype(BF16), vb)
                new += [m_new, l, acc]
            return tuple(new)

        init = (jnp.full((tq, 1), NEG, F32), jnp.zeros((tq, 1), F32), jnp.zeros((tq, HEAD_DIM), F32)) * 2
        carry = lax.fori_loop(0, i, functools.partial(blk, masked=False), init)
        carry = blk(i, carry, True)
        outs += [carry[2] / carry[1], carry[5] / carry[4]]
    o_ref[...] = jnp.concatenate(outs, axis=1)


def mla_attn_prompt(q, k, v, b, t):
    tq = _tile(t, 512)
    nq = t // tq
    return pl.pallas_call(
        functools.partial(_mla_attn_prompt_kernel, tq=tq),
        grid=(b, nq),
        in_specs=[pl.BlockSpec((tq, 512), lambda bi, i: (bi * nq + i, 0)),
                  pl.BlockSpec((t, 512), lambda bi, i: (bi, 0)),
                  pl.BlockSpec((t, 256), lambda bi, i: (bi, 0))],
        out_specs=pl.BlockSpec((tq, 256), lambda bi, i: (bi * nq + i, 0)),
        out_shape=jax.ShapeDtypeStruct((b * t, 256), F32),
        compiler_params=_cparams("parallel", "arbitrary"),
        name="mla_attn_prompt",
    )(q, k, v)


SUB_PAGES = 4
PAGE_SLOTS = 3


def _mla_attn_sample_kernel(pt_ref, qbd_ref, qpe_ref, ckvn_ref, kpen_ref, wuk_ref, wukt_ref, wuv_ref, ckv_hbm,
                            kpe_hbm, o_ref, ckv_buf, kpe_buf, sem, m_scr, l_scr, acc_scr, lhs_scr, *, n_pg, tq,
                            layer):
    b_id = pl.program_id(0)
    p_id = pl.program_id(1)
    n_steps = pl.num_programs(1)
    step = b_id * n_steps + p_id
    n_total = pl.num_programs(0) * n_steps
    slot = step % PAGE_SLOTS

    def page_copies_of(st):
        return page_copies(st // n_steps, st % n_steps, st % PAGE_SLOTS)

    def page_copies(b, p, sl):
        copies = []
        for k in range(n_pg):
            phys = pt_ref[b, p * n_pg + k]
            copies.append(pltpu.make_async_copy(ckv_hbm.at[phys, layer], ckv_buf.at[sl, pl.ds(k * PAGE, PAGE), :],
                                                sem.at[sl, 0]))
            copies.append(pltpu.make_async_copy(kpe_hbm.at[phys, layer], kpe_buf.at[sl, :, pl.ds(k * PAGE, PAGE)],
                                                sem.at[sl, 1]))
        return copies

    for ahead in range(PAGE_SLOTS - 1):
        @pl.when((step == 0) & (ahead < n_total))
        def _(ahead=ahead):
            for c in page_copies_of(ahead):
                c.start()

    @pl.when(step + (PAGE_SLOTS - 1) < n_total)
    def _():
        for c in page_copies_of(step + (PAGE_SLOTS - 1)):
            c.start()

    qpe = qpe_ref[0]
    nr = N_HEADS * tq

    @pl.when(p_id == 0)
    def _():
        lhs_scr[0:GROUP_W, :] = wukt_ref[...]
        lhs_scr[GROUP_W:GROUP_W + nr, :] = _dot_nt(qbd_ref[0], wuk_ref[...]).astype(BF16)

    def scores(ckv, kpe_t):
        ckv_b = ckv.astype(BF16)
        out = _dot_nt(lhs_scr[...], ckv_b)
        rinv = []
        for h in range(N_HEADS):
            kn = out[h * HEAD_DIM:(h + 1) * HEAD_DIM]
            ss = jnp.sum(kn * kn, axis=0, keepdims=True) * (1.0 / HEAD_DIM)
            rinv.append(jnp.broadcast_to(lax.rsqrt(ss + EPS), (tq, ss.shape[1])))
        s = out[GROUP_W:GROUP_W + nr] * jnp.concatenate(rinv, axis=0) + _dot(qpe, kpe_t.astype(BF16))
        return s, ckv_b

    @pl.when(p_id == 0)
    def _():
        s, ckv_b = scores(ckvn_ref[0], kpen_ref[0])
        r = lax.broadcasted_iota(jnp.int32, s.shape, 0) % tq
        cc = lax.broadcasted_iota(jnp.int32, s.shape, 1)
        s = jnp.where(cc <= r, s, NEG)
        m = jnp.max(s, axis=-1, keepdims=True)
        p = jnp.exp(s - m)
        m_scr[...] = m
        l_scr[...] = jnp.sum(p, axis=-1, keepdims=True)
        acc_scr[...] = _dot(p.astype(BF16), ckv_b)

    pltpu.make_async_copy(ckv_buf.at[slot], ckv_buf.at[slot], sem.at[slot, 0]).wait()
    pltpu.make_async_copy(kpe_buf.at[slot], kpe_buf.at[slot], sem.at[slot, 1]).wait()

    n_sub = max(n_pg // SUB_PAGES, 1)
    keys = (n_pg // n_sub) * PAGE
    s_parts, ckv_parts = [], []
    for g in range(n_sub):
        s, ckv_b = scores(ckv_buf[slot, g * keys:(g + 1) * keys, :], kpe_buf[slot, :, g * keys:(g + 1) * keys])
        s_parts.append(s)
        ckv_parts.append(ckv_b)
    m = m_scr[...]
    m_new = m
    for s in s_parts:
        m_new = jnp.maximum(m_new, jnp.max(s, axis=-1, keepdims=True))
    corr = jnp.exp(m - m_new)
    l = l_scr[...] * corr
    acc = acc_scr[...] * corr
    for s, ckv_b in zip(s_parts, ckv_parts):
        p = jnp.exp(s - m_new)
        l = l + jnp.sum(p, axis=-1, keepdims=True)
        acc = acc + _dot(p.astype(BF16), ckv_b)
    m_scr[...] = m_new
    l_scr[...] = l
    acc_scr[...] = acc

    @pl.when(p_id == n_steps - 1)
    def _():
        lat = (acc / l).astype(BF16)
        full = _dot(lat, wuv_ref[...])
        lane_head = lax.broadcasted_iota(jnp.int32, (tq, 256), 1) // HEAD_DIM
        out = jnp.zeros((tq, 256), F32)
        for h in range(N_HEADS):
            out = jnp.where(lane_head == h, full[h * tq:(h + 1) * tq, :], out)
        o_ref[0] = out


def mla_attn_sample(page_table, qbd, qpe, ckv_new, kpe_new_t, wuk, wukt, wuv, cache_ckv, cache_kpe_t, layer,
                    pages_per_step):
    bs, n_pages = page_table.shape
    tq = qbd.shape[1] // N_HEADS
    n_pg = min(pages_per_step, n_pages)
    assert n_pages % n_pg == 0
    nr = N_HEADS * tq
    per_b = lambda shp: pl.BlockSpec((1,) + shp, lambda b, p, pt: (b, 0, 0))
    cst = lambda shp: pl.BlockSpec(shp, lambda b, p, pt: (0,) * len(shp))
    hbm = pl.BlockSpec(memory_space=pl.ANY)
    grid_spec = pltpu.PrefetchScalarGridSpec(
        num_scalar_prefetch=1,
        grid=(bs, n_pages // n_pg),
        in_specs=[per_b((nr, 256)), per_b((nr, QK_ROPE)), per_b((PAGE, KV_LORA)), per_b((QK_ROPE, PAGE)),
                  cst(wuk.shape), cst(wukt.shape), cst(wuv.shape), hbm, hbm],
        out_specs=pl.BlockSpec((1, tq, 256), lambda b, p, pt: (b, 0, 0)),
        scratch_shapes=[pltpu.VMEM((PAGE_SLOTS, n_pg * PAGE, KV_LORA), F32),
                        pltpu.VMEM((PAGE_SLOTS, QK_ROPE, n_pg * PAGE), F32),
                        pltpu.SemaphoreType.DMA((PAGE_SLOTS, 2)),
                        pltpu.VMEM((nr, 1), F32), pltpu.VMEM((nr, 1), F32), pltpu.VMEM((nr, KV_LORA), F32),
                        pltpu.VMEM((GROUP_W + nr, KV_LORA), BF16)],
    )
    return pl.pallas_call(
        functools.partial(_mla_attn_sample_kernel, n_pg=n_pg, tq=tq, layer=layer),
        grid_spec=grid_spec,
        out_shape=jax.ShapeDtypeStruct((bs, tq, 256), F32),
        compiler_params=_cparams("arbitrary", "arbitrary"),
        name="mla_attn_sample",
    )(page_table, qbd, qpe, ckv_new, kpe_new_t, wuk, wukt, wuv, cache_ckv, cache_kpe_t)


CONV_HALO = 32


def _conv_kernel(pb_ref, st_ref, cw_ref, cb_ref, e_ref, g_ref, b_ref, pw_ref, o_ref, st_out_ref, xbuf, *, tt, n_seq):
    j = pl.program_id(1)
    lo = CONV_HALO - (CONV_W - 1)

    @pl.when(j == 0)
    def _():
        xbuf[:, pl.ds(lo, CONV_W - 1), :] = st_ref[...]
        xbuf[:, pl.ds(CONV_HALO + tt, SUBLANES), :] = jnp.zeros((n_seq, SUBLANES, 256), F32)

    pb = pb_ref[...]
    glu = pb[:, 0:256] * _sigmoid(pb[:, 256:512])
    ys = []
    for s in range(n_seq):
        xbuf[s, pl.ds(CONV_HALO, tt), :] = glu[s * tt:(s + 1) * tt]
        y = jnp.zeros((tt, 256), F32) + cb_ref[...]
        for r in range(SUBLANES):
            z = None
            for k in range(CONV_W):
                if (lo + k) % SUBLANES == r:
                    term = xbuf[s, pl.ds(lo + k - r, tt + SUBLANES), :] * cw_ref[pl.ds(k, 1), :]
                    z = term if z is None else z + term
            y = y + z[r:r + tt]
        new_state = xbuf[s, pl.ds(lo + tt, CONV_W - 1), :]
        xbuf[s, pl.ds(lo, CONV_W - 1), :] = new_state
        st_out_ref[s] = new_state
        ys.append(y)
    y = jnp.concatenate(ys, axis=0) if n_seq > 1 else ys[0]
    e = e_ref[...]
    yc = y - _seg_sum(y, e)
    yn = yc * lax.rsqrt(_seg_sum(yc * yc, e) + LN_EPS) * g_ref[...] + b_ref[...]
    act = yn * _sigmoid(yn)
    o_ref[...] = _dot(act.astype(BF16), pw_ref[...])


SHORT_SEQ_ROWS = 128


def _seq_tiling(b, t):
    if t >= SHORT_SEQ_ROWS:
        return 1, _tile(t, 256)
    return _tile(b, SHORT_SEQ_ROWS // t), t


def conv_module(pb, state, wts, b, t):
    n_seq, tt = _seq_tiling(b, t)
    nt = t // tt
    return pl.pallas_call(
        functools.partial(_conv_kernel, tt=tt, n_seq=n_seq),
        grid=(b // n_seq, nt),
        in_specs=[pl.BlockSpec((n_seq * tt, 512), lambda bi, j: (bi * nt + j, 0)),
                  pl.BlockSpec((n_seq, CONV_W - 1, 256), lambda bi, j: (bi, 0, 0))]
                 + [_full(w.shape) for w in wts],
        out_specs=[pl.BlockSpec((n_seq * tt, 256), lambda bi, j: (bi * nt + j, 0)),
                   pl.BlockSpec((n_seq, CONV_W - 1, 256), lambda bi, j: (bi, 0, 0))],
        out_shape=[jax.ShapeDtypeStruct((b * t, 256), F32), jax.ShapeDtypeStruct((b, CONV_W - 1, 256), F32)],
        scratch_shapes=[pltpu.VMEM((n_seq, CONV_HALO + tt + SUBLANES, 256), F32)],
        compiler_params=_cparams("parallel", "arbitrary"),
        name="conv_module",
    )(pb, state, *wts)


def _sgu_kernel(pc_ref, g_ref, b_ref, w_ref, bias_ref, o_ref, v_ref, *, n_chunks):
    x = pc_ref[...]
    z = 0.5 * x * (1.0 + jnp.tanh(0.7978845608028654 * (x + 0.044715 * (x * x * x))))
    u = z[:, 0:256]
    v = z[:, 256:512]
    vc = v - jnp.mean(v, axis=-1, keepdims=True)
    v = vc * lax.rsqrt(jnp.mean(vc * vc, axis=-1, keepdims=True) + LN_EPS) * g_ref[...] + b_ref[...]
    v_ref[...] = v
    lane_head = lax.broadcasted_iota(jnp.int32, (CHUNK, 256), 1) // HEAD_DIM
    for c in range(n_chunks):
        vcb = v[c * CHUNK:(c + 1) * CHUNK, :]
        sv = bias_ref[...]
        for h in range(N_HEADS):
            sv = sv + _dot(w_ref[h], jnp.where(lane_head == h, vcb, 0.0).astype(BF16))
        o_ref[pl.ds(c * CHUNK, CHUNK), :] = u[c * CHUNK:(c + 1) * CHUNK, :] * sv


def sgu(pc, wts):
    n = pc.shape[0]
    tm = _tile(n, 512)
    row = lambda wd: pl.BlockSpec((tm, wd), lambda i: (i, 0))
    return pl.pallas_call(
        functools.partial(_sgu_kernel, n_chunks=tm // CHUNK),
        grid=(n // tm,),
        in_specs=[row(512)] + [_full(w.shape) for w in wts],
        out_specs=[row(256), row(256)],
        out_shape=[jax.ShapeDtypeStruct((n, 256), F32), jax.ShapeDtypeStruct((n, 256), F32)],
        compiler_params=_cparams("parallel"),
        name="sgu",
    )(pc, *wts)


RW_HALO = 8


def _rwkv_prep_kernel(pd_ref, sh_ref, mu_ref, w0_ref, w2_ref, a0_ref, a2_ref, g2_ref, kkp_ref, ka_ref, rk_ref,
                      e_ref, r_ref, w_ref, k_ref, v_ref, kk_ref, kka_ref, g_ref, bv_ref, xbuf, *, tt, n_seq):
    j = pl.program_id(1)

    @pl.when(j == 0)
    def _():
        xbuf[:, pl.ds(RW_HALO - 1, 1), :] = sh_ref[...]

    pd = pd_ref[...]
    prevs = []
    for s in range(n_seq):
        rows = pd[s * tt:(s + 1) * tt]
        xbuf[s, pl.ds(RW_HALO, tt), :] = rows
        prevs.append(xbuf[s, pl.ds(RW_HALO - 1, tt), :])
        xbuf[s, pl.ds(RW_HALO - 1, 1), :] = rows[tt - 1:tt, :]
    prev = jnp.concatenate(prevs, axis=0) if n_seq > 1 else prevs[0]
    xs = pd + (prev - pd) * mu_ref[...]
    r = xs[:, 0:256]
    k = xs[:, 256:512]
    v = xs[:, 512:768]
    xwa = xs[:, 768:896]
    xg = xs[:, 896:1024]
    z = -(w0_ref[...] + _dot(jnp.tanh(xwa).astype(BF16), w2_ref[...]))
    softplus = jnp.maximum(z, 0.0) + jnp.log(1.0 + jnp.exp(-jnp.abs(z)))
    w_ref[...] = jnp.exp(-jnp.exp(-softplus - 0.5))
    a = _sigmoid(a0_ref[...] + _dot(xwa.astype(BF16), a2_ref[...]))
    g_ref[...] = _dot(_sigmoid(xg).astype(BF16), g2_ref[...])
    e = e_ref[...]
    kk = k * kkp_ref[...]
    kk = kk * lax.rsqrt(_seg_sum(kk * kk, e) + 1e-12)
    k2 = k * (1.0 + (a - 1.0) * ka_ref[...])
    r_ref[...] = r
    k_ref[...] = k2
    v_ref[...] = v
    kk_ref[...] = kk
    kka_ref[...] = kk * a
    bv_ref[...] = _seg_sum(r * k2 * rk_ref[...], e) * v


def rwkv_prep(pd, shift, wts, b, t):
    n_seq, tt = _seq_tiling(b, t)
    nt = t // tt
    row = lambda wd: pl.BlockSpec((n_seq * tt, wd), lambda bi, j: (bi * nt + j, 0))
    return pl.pallas_call(
        functools.partial(_rwkv_prep_kernel, tt=tt, n_seq=n_seq),
        grid=(b // n_seq, nt),
        in_specs=[row(PD_W), pl.BlockSpec((n_seq, 1, PD_W), lambda bi, j: (bi, 0, 0))]
                 + [_full(w.shape) for w in wts],
        out_specs=[row(256)] * 8,
        out_shape=[jax.ShapeDtypeStruct((b * t, 256), F32)] * 8,
        scratch_shapes=[pltpu.VMEM((n_seq, RW_HALO + tt, PD_W), F32)],
        compiler_params=_cparams("parallel", "arbitrary"),
        name="rwkv_prep",
    )(pd, shift, *wts)


RW_BLOCK = 128


def _pack_bf16_pair(a, b):
    ua = lax.bitcast_convert_type(a.astype(BF16).astype(F32), jnp.uint32)
    ub = lax.bitcast_convert_type(b.astype(BF16).astype(F32), jnp.uint32)
    return lax.bitcast_convert_type(ua | (ub >> 16), jnp.int32)


def _unpack_bf16_pair(word):
    u = lax.bitcast_convert_type(word, jnp.uint32)
    return (lax.bitcast_convert_type(u & jnp.uint32(0xFFFF0000), F32),
            lax.bitcast_convert_type(u << 16, F32))


def _rwkv_scan_kernel(w_ref, kk_ref, kka_ref, k_ref, r_ref, v_ref, s0_ref, e_ref, y_ref, sf_ref, s_scr, col_scr, *,
                      nb, sblk):
    c = pl.program_id(1)

    @pl.when(c == 0)
    def _():
        s_scr[...] = s0_ref[...]

    lane = lax.broadcasted_iota(jnp.int32, (HEAD_DIM, LANES), 1)
    low = lane < HEAD_DIM
    n_half = 2 if sblk > HEAD_DIM else 1
    for b in range(nb):
        for p in range(2):
            blk = lambda ref: ref[b, :, p * LANES:(p + 1) * LANES]
            x = _pack_bf16_pair(blk(kk_ref), blk(kka_ref))
            if sblk < LANES:
                x = jnp.concatenate([x, jnp.zeros((LANES - sblk, LANES), jnp.int32)], axis=0)
            xt = x.T
            h0, h1 = xt[0:HEAD_DIM], xt[HEAD_DIM:LANES]
            col_scr[b, p, 0] = jnp.where(low, h0, pltpu.roll(h1, HEAD_DIM, 1))
            if n_half == 2:
                col_scr[b, p, 1] = jnp.where(low, pltpu.roll(h0, HEAD_DIM, 1), h1)

    base = jnp.where(low, 0, HEAD_DIM)
    diag = (lax.broadcasted_iota(jnp.int32, (HEAD_DIM, 256), 1) % HEAD_DIM
            == lax.broadcasted_iota(jnp.int32, (HEAD_DIM, 256), 0))
    for half in range(n_half):
        n_groups = min(sblk - half * HEAD_DIM, HEAD_DIM) // SUBLANES

        def group(gi, carry, half=half):
            row0 = pl.multiple_of(half * HEAD_DIM + gi * SUBLANES, SUBLANES)
            for b in range(nb):
                v8 = v_ref[b, pl.ds(row0, SUBLANES), :]
                w8 = w_ref[b, pl.ds(row0, SUBLANES), :]
                w8_hi = w8.astype(BF16).astype(F32)
                rows8 = (k_ref[b, pl.ds(row0, SUBLANES), :], r_ref[b, pl.ds(row0, SUBLANES), :], w8_hi, w8 - w8_hi)
                on_diag = [jnp.where(diag, jnp.broadcast_to(x8[i:i + 1, :], (HEAD_DIM, 256)), 0.0).astype(BF16)
                           for i in range(SUBLANES) for x8 in rows8]
                spread = _dot(jnp.concatenate(on_diag, axis=0), e_ref[...])
                ys = []
                for p in range(2):
                    st = s_scr[b, :, p * LANES:(p + 1) * LANES]
                    yp = []
                    for i in range(SUBLANES):
                        idx = base + (gi * SUBLANES + i)
                        kk, kka = _unpack_bf16_pair(jnp.take_along_axis(col_scr[b, p, half], idx, axis=1))
                        part = lambda n: spread[(4 * i + n) * HEAD_DIM:(4 * i + n + 1) * HEAD_DIM,
                                                p * LANES:(p + 1) * LANES]
                        k, r, w = part(0), part(1), part(2) + part(3)
                        vrow = v8[i:i + 1, p * LANES:(p + 1) * LANES]
                        sa = -jnp.sum(st * kk, axis=0, keepdims=True)
                        st = st * w + kka * sa + k * vrow
                        yp.append(jnp.sum(st * r, axis=0, keepdims=True))
                    s_scr[b, :, p * LANES:(p + 1) * LANES] = st
                    ys.append(jnp.concatenate(yp, axis=0))
                y_ref[b, pl.ds(row0, SUBLANES), :] = jnp.concatenate(ys, axis=1)
            return carry

        lax.fori_loop(0, n_groups, group, 0)

    @pl.when(c == pl.num_programs(1) - 1)
    def _():
        sf_ref[...] = s_scr[...]


def rwkv_scan(seqs, v, s0, nb):
    b, t, _ = v.shape
    e_heads = _head_matrix(1.0)
    sblk = min(t, RW_BLOCK)
    assert b % nb == 0 and t % sblk == 0 and sblk % SUBLANES == 0 and (sblk <= HEAD_DIM or sblk == RW_BLOCK)
    st_spec = pl.BlockSpec((nb, HEAD_DIM, 256), lambda bi, c: (bi, 0, 0))
    seq_spec = pl.BlockSpec((nb, sblk, 256), lambda bi, c: (bi, c, 0))
    return pl.pallas_call(
        functools.partial(_rwkv_scan_kernel, nb=nb, sblk=sblk),
        grid=(b // nb, t // sblk),
        in_specs=[seq_spec] * 6 + [st_spec, _full(e_heads.shape)],
        out_specs=[seq_spec, st_spec],
        out_shape=[jax.ShapeDtypeStruct((b, t, 256), F32), jax.ShapeDtypeStruct((b, HEAD_DIM, 256), F32)],
        scratch_shapes=[pltpu.VMEM((nb, HEAD_DIM, 256), F32),
                        pltpu.VMEM((nb, 2, 2, HEAD_DIM, LANES), jnp.int32)],
        compiler_params=_cparams("parallel", "arbitrary"),
        name="rwkv_scan",
    )(*seqs, v, s0, e_heads)


def _mix_out_kernel(x_ref, oa_ref, ob_ref, oc_ref, y_ref, bv_ref, g_ref, e_ref, lg_ref, lb_ref, on_ref, w_ref,
                    nx_ref, wq_ref, qg_ref, o_ref, q_ref):
    e = e_ref[...]
    y = y_ref[...]
    yc = y - _seg_sum(y, e)
    yn = yc * lax.rsqrt(_seg_sum(yc * yc, e) + RW_LN_EPS) * lg_ref[...] + lb_ref[...]
    od = (yn + bv_ref[...]) * g_ref[...]
    acc = x_ref[...]
    for gi, o in enumerate((oa_ref[...], ob_ref[...], oc_ref[...], od)):
        on = (_rms(o) * on_ref[:, gi * 256:(gi + 1) * 256]).astype(BF16)
        acc = acc + _dot(on, w_ref[pl.ds(gi * 256, 256), :])
    o_ref[...] = acc
    q = _dot((_rms(acc) * nx_ref[...]).astype(BF16), wq_ref[...])
    heads = [_rms(q[:, h * X_HEAD_DIM:(h + 1) * X_HEAD_DIM]) * qg_ref[...] * X_HEAD_DIM ** -0.5
             for h in range(X_HEADS)]
    q_ref[...] = jnp.concatenate(heads, axis=1).astype(q_ref.dtype)


def mix_out(x, oa, ob, oc, y, bv, g, wts, q_dtype):
    n, d = x.shape
    tm = _tile(n, 512)
    row = lambda wd: pl.BlockSpec((tm, wd), lambda i: (i, 0))
    return pl.pallas_call(
        _mix_out_kernel,
        grid=(n // tm,),
        in_specs=[row(d)] + [row(256)] * 6 + [_full(w.shape) for w in wts],
        out_specs=[row(d), row(512)],
        out_shape=[jax.ShapeDtypeStruct((n, d), F32), jax.ShapeDtypeStruct((n, 512), q_dtype)],
        compiler_params=_cparams("parallel"),
        name="mix_out",
    )(x, oa, ob, oc, y, bv, g, *wts)


XATTN_SHORT_ROWS = 64


def _xattn_kernel(x_ref, q_ref, k_ref, v_ref, wo_ref, o_ref, *, n_seq, rows, n_mem):
    seq_outs = []
    for s in range(n_seq):
        qs = [q_ref[s * rows:(s + 1) * rows, h * X_HEAD_DIM:(h + 1) * X_HEAD_DIM].astype(BF16)
              for h in range(X_HEADS)]
        if n_seq == 1:
            outs = []
            for h in range(X_HEADS):
                head_rows = pl.ds(h, n_mem, stride=X_HEADS)
                sc = _dot_nt(qs[h], k_ref[s, head_rows, :].astype(BF16))
                p = jnp.exp(sc - jnp.max(sc, axis=-1, keepdims=True))
                p = p / jnp.sum(p, axis=-1, keepdims=True)
                outs.append(_dot(p.astype(BF16), v_ref[s, head_rows, :].astype(BF16)))
        else:
            sc = _dot_nt(jnp.concatenate(qs, axis=0), k_ref[s].astype(BF16))
            row_head = lax.broadcasted_iota(jnp.int32, sc.shape, 0) // rows
            col_head = lax.broadcasted_iota(jnp.int32, sc.shape, 1) % X_HEADS
            sc = jnp.where(row_head == col_head, sc, NEG)
            p = jnp.exp(sc - jnp.max(sc, axis=-1, keepdims=True))
            p = p / jnp.sum(p, axis=-1, keepdims=True)
            o = _dot(p.astype(BF16), v_ref[s].astype(BF16))
            outs = [o[h * rows:(h + 1) * rows] for h in range(X_HEADS)]
        seq_outs.append(jnp.concatenate(outs, axis=1))
    xo = jnp.concatenate(seq_outs, axis=0).astype(BF16)
    o_ref[...] = x_ref[...] + _dot(xo, wo_ref[...])


def xattn(x, q, mem_k, mem_v, wo, b, t, kv_seq0):
    n, d = x.shape
    n_mem = mem_k.shape[1] // X_HEADS
    if t >= XATTN_SHORT_ROWS:
        n_seq, rows = 1, _tile(t, 512)
    else:
        n_seq, rows = _tile(b, XATTN_SHORT_ROWS // t), t
    nq = t // rows
    assert kv_seq0 % n_seq == 0
    row = lambda wd: pl.BlockSpec((n_seq * rows, wd), lambda bi, i: (bi * nq + i, 0))
    kv = pl.BlockSpec((n_seq, n_mem * X_HEADS, X_HEAD_DIM), lambda bi, i: (kv_seq0 // n_seq + bi, 0, 0))
    return pl.pallas_call(
        functools.partial(_xattn_kernel, n_seq=n_seq, rows=rows, n_mem=n_mem),
        grid=(b // n_seq, nq),
        in_specs=[row(d), row(512), kv, kv, _full(wo.shape)],
        out_specs=row(d),
        out_shape=jax.ShapeDtypeStruct((n, d), F32),
        compiler_params=_cparams("parallel", "arbitrary"),
        name="xattn",
    )(x, q, mem_k, mem_v, wo)


def _mem_kv_kernel(x_ref, g_ref, w_ref, hg_ref, k_ref, v_ref, *, tm):
    hb = (_rms(x_ref[...]) * g_ref[...]).astype(BF16)
    y = _dot(hb, w_ref[...])
    for h in range(X_HEADS):
        head_rows = pl.ds(h, tm, stride=X_HEADS)
        k_ref[head_rows, :] = _rms(y[:, h * X_HEAD_DIM:(h + 1) * X_HEAD_DIM]) * hg_ref[...]
        v_ref[head_rows, :] = y[:, (X_HEADS + h) * X_HEAD_DIM:(X_HEADS + h + 1) * X_HEAD_DIM]


def mem_kv(x, g, w, hg):
    n, d = x.shape
    tm = _tile(n, 512)
    out = pl.BlockSpec((tm * X_HEADS, X_HEAD_DIM), lambda i: (i, 0))
    return pl.pallas_call(
        functools.partial(_mem_kv_kernel, tm=tm),
        grid=(n // tm,),
        in_specs=[pl.BlockSpec((tm, d), lambda i: (i, 0)), _full(g.shape), _full(w.shape), _full(hg.shape)],
        out_specs=[out, out],
        out_shape=[jax.ShapeDtypeStruct((n * X_HEADS, X_HEAD_DIM), F32)] * 2,
        compiler_params=_cparams("parallel"),
        name="mem_kv",
    )(x, g, w, hg)


FFN_CHUNK = 256


def _ffn_kernel(x_ref, g_ref, wg_ref, wu_ref, wo_ref, o_ref, *, n_chunks):
    x = x_ref[...]
    hb = (_rms(x) * g_ref[...]).astype(BF16)
    acc = x
    for c in range(n_chunks):
        a = _dot(hb, wg_ref[c])
        u = _dot(hb, wu_ref[c])
        acc = acc + _dot((a * _sigmoid(a) * u).astype(BF16), wo_ref[c])
    o_ref[...] = acc


def ffn(x, g, wg, wu, wo):
    n, d = x.shape
    tm = _tile(n, 512)
    row = pl.BlockSpec((tm, d), lambda i: (i, 0))
    return pl.pallas_call(
        functools.partial(_ffn_kernel, n_chunks=wg.shape[0]),
        grid=(n // tm,),
        in_specs=[row, _full(g.shape), _full(wg.shape), _full(wu.shape), _full(wo.shape)],
        out_specs=row,
        out_shape=jax.ShapeDtypeStruct((n, d), F32),
        compiler_params=_cparams("parallel"),
        name="ffn",
    )(x, g, wg, wu, wo)


def _seg_matrix(seg_ids, seg_len):
    s = np.asarray(seg_ids)
    m = (s[:, None] == s[None, :]) & (s[:, None] >= 0)
    return jnp.asarray(m.astype(np.float32) / np.asarray(seg_len, np.float32)[None, :], BF16)


def _mla_segments():
    lane = np.arange(512)
    blk, off = lane // 128, lane % 128
    q_ids = np.where(off < 32, 2 * blk, np.where(off < 64, -1, 2 * blk + 1))
    q_len = np.where(off < 32, 32.0, 64.0)
    k_ids = np.where(off < 64, -1, blk)
    k_len = np.full(512, 64.0)
    return _seg_matrix(q_ids, q_len), _seg_matrix(k_ids, k_len)


def _head_matrix(scale_len):
    lane = np.arange(256)
    return _seg_matrix(lane // HEAD_DIM, np.full(256, scale_len))


def _rope_tables(pos):
    half = QK_ROPE // 2
    inv = jnp.power(ROPE_BASE, -jnp.arange(half, dtype=F32) / half)
    ang = pos.astype(F32)[:, None] * inv[None, :]
    cos, sin = jnp.cos(ang), jnp.sin(ang)
    n = pos.shape[0]
    z = lambda w: jnp.zeros((n, w), F32)
    c = jnp.concatenate([cos, cos, jnp.ones((n, LANES - QK_ROPE), F32)], axis=1)
    sa = jnp.concatenate([-sin, z(LANES - half)], axis=1)
    sb = jnp.concatenate([z(half), sin, z(LANES - QK_ROPE)], axis=1)
    return c, sa, sb


def _layer_weights(p):
    row = lambda v: v.reshape(1, -1).astype(F32)
    zc = lambda a, w: jnp.zeros((a.shape[0], w), a.dtype)
    w_in = p['w_in']
    c1, c2, c3 = 352, 352 + 512, 352 + 1024
    pa = w_in[:, :c1]
    w_in_p = jnp.concatenate(
        [pa[:, :Q_LORA], zc(pa, 64), pa[:, Q_LORA:Q_LORA + KV_LORA], pa[:, Q_LORA + KV_LORA:], zc(pa, 96),
         w_in[:, c1:c2], w_in[:, c2:c3], w_in[:, c3:]], axis=1).astype(BF16)

    wuq = p['mla_w_uq'].reshape(Q_LORA, N_HEADS, QK_NOPE + QK_ROPE)
    wuq = jnp.concatenate([wuq[:, :, QK_NOPE:], jnp.zeros((Q_LORA, N_HEADS, 32), F32), wuq[:, :, :QK_NOPE]], axis=2)
    wuq = jnp.concatenate([wuq.reshape(Q_LORA, 512), jnp.zeros((256 - Q_LORA, 512), F32)], axis=0).astype(BF16)
    wuk = p['mla_w_uk'].reshape(KV_LORA, N_HEADS, QK_NOPE)
    wuk_p = jnp.concatenate([jnp.zeros((KV_LORA, N_HEADS, 64), F32), wuk], axis=2).reshape(KV_LORA, 512).astype(BF16)
    blk = lambda a, b_, c_: jnp.tile(jnp.concatenate([a, b_, c_]), N_HEADS).reshape(1, 512)
    z32, z64 = jnp.zeros((32,), F32), jnp.zeros((64,), F32)
    gq = blk(p['mla_gq_rope'], z32, p['mla_gq_nope'])
    gk = blk(z32, z32, p['mla_gk_nope'])
    gqn = jnp.concatenate([p['mla_q_norm'], z64]).reshape(1, 256)
    gkr = jnp.concatenate([p['mla_gk_rope'], jnp.zeros((96,), F32)]).reshape(1, LANES)
    eq, ek = _mla_segments()
    mla = (gqn, row(p['mla_kv_norm']), wuq, eq, gq, gkr, wuk_p, ek, gk, p['mla_w_uv'].astype(BF16))

    e64 = _head_matrix(64.0)
    conv = (p['conv_w'], row(p['conv_b']), e64, row(p['conv_norm_g']), row(p['conv_norm_b']),
            p['conv_pw'].astype(BF16))

    z64r = jnp.zeros((64, GROUP_W), F32)
    rw = (row(p['rw_mu']),
          row(p['rw_w0']), jnp.concatenate([p['rw_w2'], z64r], axis=0).astype(BF16),
          row(p['rw_a0']), jnp.concatenate([z64r, p['rw_a2']], axis=0).astype(BF16),
          p['rw_g2'].astype(BF16), row(p['rw_kk']), row(p['rw_ka']), row(p['rw_rk']), _head_matrix(1.0))

    mix = (e64, row(p['rw_ln_g']), row(p['rw_ln_b']), row(p['out_norm']), p['w_out'].astype(BF16))

    d_ff = p['w_ffn_out'].shape[0]
    nck = d_ff // FFN_CHUNK
    d = w_in.shape[0]
    wg = p['w_ffn_in'][:, :d_ff].reshape(d, nck, FFN_CHUNK).transpose(1, 0, 2).astype(BF16)
    wu = p['w_ffn_in'][:, d_ff:].reshape(d, nck, FFN_CHUNK).transpose(1, 0, 2).astype(BF16)
    wo = p['w_ffn_out'].reshape(nck, FFN_CHUNK, d).astype(BF16)

    return dict(
        norm_mix=row(p['norm_mix']), w_in=w_in_p, mla=mla, conv=conv, rw=rw, mix=mix,
        wuk=p['mla_w_uk'].astype(BF16), wuv=p['mla_w_uv'].astype(BF16), gk_nope=p['mla_gk_nope'],
        sgu_ln=(row(p['sgu_norm_g']), row(p['sgu_norm_b'])), sgu_w=p['sgu_w'], sgu_b=p['sgu_b'],
        norm_x=row(p['norm_x']), wq=p['wq_x'].astype(BF16), xq_norm=row(p['xq_norm']),
        mem_norm=row(p['mem_norm']), wkv=jnp.concatenate([p['wk_x'], p['wv_x']], axis=1).astype(BF16),
        xk_norm=row(p['xk_norm']), wo_x=p['wo_x'].astype(BF16),
        norm_ffn=row(p['norm_ffn']), wg=wg, wu=wu, wo=wo)


def _sgu_weights(lw, t):
    l = min(t, CHUNK)
    w = lw['sgu_w'][:, :l, :l] * jnp.tril(jnp.ones((l, l), F32))
    reps = CHUNK // l
    if reps > 1:
        w = jnp.einsum('ab,hij->haibj', jnp.eye(reps, dtype=F32), w).reshape(N_HEADS, CHUNK, CHUNK)
    bias = jnp.tile(lw['sgu_b'][:, :l].T, (reps, 1))
    bias = jnp.repeat(bias, HEAD_DIM, axis=1)
    return lw['sgu_ln'] + (w.astype(BF16), bias)


def _trunk_layer(x, b, t, lw, tabs, n_tab_blocks, attend, mem_k, mem_v, kv_seq0, conv_state, shift_state, wkv_state,
                 scan_nb):
    pa, pb, pc, pd = proj_in(x, lw['norm_mix'], lw['w_in'])
    q, k, v, ckv, kpe = mla_prep(pa, tabs, lw['mla'], n_tab_blocks)
    oa = attend(q, k, v, ckv, kpe)
    ob, conv_new = conv_module(pb, conv_state, lw['conv'], b, t)
    oc, v_sgu = sgu(pc, _sgu_weights(lw, t))
    r, w, k2, vv, kk, kka, g, bv = rwkv_prep(pd, shift_state.reshape(b, 1, PD_W), lw['rw'], b, t)
    seqs = [a.reshape(b, t, 256) for a in (w, kk, kka, k2, r)]
    s0 = wkv_state.transpose(0, 3, 1, 2).reshape(b, HEAD_DIM, 256)
    y, s_fin = rwkv_scan(seqs, vv.reshape(b, t, 256), s0, scan_nb)
    wkv_new = s_fin.reshape(b, HEAD_DIM, N_HEADS, HEAD_DIM).transpose(0, 2, 3, 1)
    shift_new = pd.reshape(b, t, PD_W)[:, -1]
    q_dtype = BF16 if t % 16 == 0 else F32
    x, qx = mix_out(x, oa, ob, oc, y.reshape(b * t, 256), bv, g,
                    lw['mix'] + (lw['norm_x'], lw['wq'], lw['xq_norm']), q_dtype)
    x = xattn(x, qx, mem_k, mem_v, lw['wo_x'], b, t, kv_seq0)
    x = ffn(x, lw['norm_ffn'], lw['wg'], lw['wu'], lw['wo'])
    return x, ckv, kpe, conv_new, shift_new, wkv_new, v_sgu


def kernel(x_prompt, x_sample, mem_prompt, cache_ckv, cache_kpe, cache_mem_k, cache_mem_v, state_conv, state_shift, state_wkv, page_table, norm_mix, w_in, mla_q_norm, mla_kv_norm, mla_w_uq, mla_w_uk, mla_w_uv, mla_gq_nope, mla_gq_rope, mla_gk_nope, mla_gk_rope, conv_w, conv_b, conv_norm_g, conv_norm_b, conv_pw, sgu_norm_g, sgu_norm_b, sgu_w, sgu_b, rw_mu, rw_w0, rw_w2, rw_a0, rw_a2, rw_g2, rw_kk, rw_ka, rw_rk, rw_ln_g, rw_ln_b, out_norm, w_out, norm_x, mem_norm, wq_x, wk_x, wv_x, xq_norm, xk_norm, wo_x, norm_ffn, w_ffn_in, w_ffn_out):
    params = dict(
        norm_mix=norm_mix, w_in=w_in, mla_q_norm=mla_q_norm, mla_kv_norm=mla_kv_norm, mla_w_uq=mla_w_uq,
        mla_w_uk=mla_w_uk, mla_w_uv=mla_w_uv, mla_gq_nope=mla_gq_nope, mla_gq_rope=mla_gq_rope,
        mla_gk_nope=mla_gk_nope, mla_gk_rope=mla_gk_rope, conv_w=conv_w, conv_b=conv_b, conv_norm_g=conv_norm_g,
        conv_norm_b=conv_norm_b, conv_pw=conv_pw, sgu_norm_g=sgu_norm_g, sgu_norm_b=sgu_norm_b, sgu_w=sgu_w,
        sgu_b=sgu_b, rw_mu=rw_mu, rw_w0=rw_w0, rw_w2=rw_w2, rw_a0=rw_a0, rw_a2=rw_a2, rw_g2=rw_g2, rw_kk=rw_kk,
        rw_ka=rw_ka, rw_rk=rw_rk, rw_ln_g=rw_ln_g, rw_ln_b=rw_ln_b, out_norm=out_norm, w_out=w_out, norm_x=norm_x,
        mem_norm=mem_norm, wq_x=wq_x, wk_x=wk_x, wv_x=wv_x, xq_norm=xq_norm, xk_norm=xk_norm, wo_x=wo_x,
        norm_ffn=norm_ffn, w_ffn_in=w_ffn_in, w_ffn_out=w_ffn_out)
    depth = w_in.shape[0]
    bp, tp, d = x_prompt.shape
    bs, ts, _ = x_sample.shape
    n_mem = mem_prompt.shape[1]
    n_pages = page_table.shape[1]
    past_len = n_pages * PAGE

    tm_p = _tile(tp, 512)
    tabs_p = _rope_tables(jnp.arange(tp, dtype=jnp.int32))
    tm_s = _tile(bs * ts, 512)
    tabs_s = tuple(jnp.tile(a, (tm_s // ts, 1)) for a in _rope_tables(past_len + jnp.arange(ts, dtype=jnp.int32)))
    cache_kpe_t = jnp.swapaxes(cache_kpe, 2, 3)
    cache_k_rows = cache_mem_k.reshape(depth * bs, n_mem * X_HEADS, X_HEAD_DIM)
    cache_v_rows = cache_mem_v.reshape(depth * bs, n_mem * X_HEADS, X_HEAD_DIM)
    y_p = x_prompt.reshape(bp * tp, d)
    y_s = x_sample.reshape(bs * ts, d)
    mem_flat = mem_prompt.reshape(bp * n_mem, d)
    zeros_conv = jnp.zeros((bp, CONV_W - 1, GROUP_W), F32)
    zeros_shift = jnp.zeros((bp, PD_W), F32)
    zeros_wkv = jnp.zeros((bp, N_HEADS, HEAD_DIM, HEAD_DIM), F32)
    outs_p, outs_s, memk_l, memv_l = [], [], [], []
    for l in range(depth):
        lw = _layer_weights({k_: v_[l] for k_, v_ in params.items()})

        mk, mv = mem_kv(mem_flat, lw['mem_norm'], lw['wkv'], lw['xk_norm'])
        memk_l.append(mk.reshape(bp, n_mem, X_HEADS, X_HEAD_DIM))
        memv_l.append(mv.reshape(bp, n_mem, X_HEADS, X_HEAD_DIM))

        def attend_p(q, k, v, ckv, kpe):
            return mla_attn_prompt(q, k, v, bp, tp)

        kv_rows = n_mem * X_HEADS
        res = _trunk_layer(y_p, bp, tp, lw, tabs_p, tp // tm_p, attend_p, mk.reshape(bp, kv_rows, X_HEAD_DIM),
                           mv.reshape(bp, kv_rows, X_HEAD_DIM), 0, zeros_conv, zeros_shift, zeros_wkv,
                           _tile(bp, SCAN_NB_PROMPT if l == 0 else 16))
        y_p = res[0]
        outs_p.append(res[1:])

        def attend_s(q, k, v, ckv, kpe, l=l, lw=lw):
            qf = q.astype(F32).reshape(bs, ts, N_HEADS, LANES)
            qn = (qf[..., 64:] * lw['gk_nope']).transpose(0, 2, 1, 3)
            qbd = jnp.einsum('bhtj,hg->bhtgj', qn, jnp.eye(N_HEADS, dtype=F32)).reshape(bs, N_HEADS * ts, 256)
            qpe = qf[..., :QK_ROPE].transpose(0, 2, 1, 3).reshape(bs, N_HEADS * ts, QK_ROPE)
            pad = lambda a: jnp.pad(a.reshape(bs, ts, -1), ((0, 0), (0, PAGE - ts), (0, 0)))
            o = mla_attn_sample(page_table, qbd.astype(BF16), qpe.astype(BF16), pad(ckv),
                                pad(kpe).transpose(0, 2, 1), lw['wuk'], lw['wuk'].T, lw['wuv'], cache_ckv,
                                cache_kpe_t, l, PAGES_PER_STEP if l == 0 else 2 * PAGES_PER_STEP)
            return o.reshape(bs * ts, 256)

        res = _trunk_layer(y_s, bs, ts, lw, tabs_s, 1, attend_s, cache_k_rows, cache_v_rows, l * bs,
                           state_conv[l], state_shift[l], state_wkv[l], _tile(bs, SCAN_NB_SAMPLE))
        y_s = res[0]
        outs_s.append(res[1:])

    n_pp = tp // PAGE
    stack = lambda outs, i, ax: jnp.stack([o[i] for o in outs], axis=ax)
    ckv_prompt = stack(outs_p, 0, 0).reshape(depth, bp, n_pp, PAGE, KV_LORA).transpose(1, 2, 0, 3, 4)
    kpe_prompt = stack(outs_p, 1, 0).reshape(depth, bp, n_pp, PAGE, QK_ROPE).transpose(1, 2, 0, 3, 4)
    ckv_sample = stack(outs_s, 0, 0).reshape(depth, bs, ts, KV_LORA).transpose(1, 0, 2, 3)
    kpe_sample = stack(outs_s, 1, 0).reshape(depth, bs, ts, QK_ROPE).transpose(1, 0, 2, 3)
    return (y_p.reshape(bp, tp, d), y_s.reshape(bs, ts, d), ckv_prompt, kpe_prompt, ckv_sample, kpe_sample,
            jnp.stack(memk_l, 0), jnp.stack(memv_l, 0),
            stack(outs_p, 2, 0), stack(outs_s, 2, 0), stack(outs_p, 3, 0), stack(outs_s, 3, 0),
            stack(outs_p, 4, 0), stack(outs_s, 4, 0),
            stack(outs_s, 5, 0).reshape(depth, bs, ts, GROUP_W))
```

```python
import functools

import numpy as np
import jax
import jax.numpy as jnp
from jax import lax
from jax.experimental import pallas as pl
from jax.experimental.pallas import tpu as pltpu

F32 = jnp.float32
BF16 = jnp.bfloat16

EPS = 1e-6
LN_EPS = 1e-5
RW_LN_EPS = 64e-5
NEG = -1e30
ROPE_BASE = 10000.0

LANES = 128
SUBLANES = 8
VMEM_LIMIT_BYTES = 56 * 1024 * 1024

GROUP_W = 256
HEAD_DIM = 64
N_HEADS = 4
Q_LORA = 192
KV_LORA = 128
QK_ROPE = 32
QK_NOPE = 64
CONV_W = 31
CHUNK = 128
PAGE = 128
X_HEADS = 4
X_HEAD_DIM = 128
MLA_SCALE = (QK_NOPE + QK_ROPE) ** -0.5
PA_W = 512
PD_W = 1024
PAGES_PER_STEP = 64
SCAN_NB_PROMPT = 8
SCAN_NB_SAMPLE = 8


def _cparams(*sem):
    return pltpu.CompilerParams(dimension_semantics=sem, vmem_limit_bytes=VMEM_LIMIT_BYTES)


def _dot(a, b):
    return jnp.dot(a, b, preferred_element_type=F32)


def _dot_nt(a, b):
    return lax.dot_general(a, b, (((1,), (1,)), ((), ())), preferred_element_type=F32)


def _seg_sum(x, e):
    hi = x.astype(BF16)
    lo = (x - hi.astype(F32)).astype(BF16)
    return _dot(hi, e) + _dot(lo, e)


def _rms(x, width=None):
    w = x.shape[-1] if width is None else width
    return x * lax.rsqrt(jnp.sum(x * x, axis=-1, keepdims=True) * (1.0 / w) + EPS)


def _sigmoid(x):
    return 1.0 / (1.0 + jnp.exp(-x))


def _tile(n, pref):
    t = min(n, pref)
    while n % t:
        t //= 2
    return t


def _full(shape):
    nd = len(shape)
    return pl.BlockSpec(shape, lambda *a: (0,) * nd)


def _proj_in_kernel(x_ref, g_ref, w_ref, pa_ref, pb_ref, pc_ref, pd_ref):
    hb = (_rms(x_ref[...]) * g_ref[...]).astype(BF16)
    pa_ref[...] = _dot(hb, w_ref[:, 0:512])
    pb_ref[...] = _dot(hb, w_ref[:, 512:1024])
    pc_ref[...] = _dot(hb, w_ref[:, 1024:1536])
    pd_ref[...] = _dot(hb, w_ref[:, 1536:2560])


def proj_in(x, g, w):
    n, d = x.shape
    tm = _tile(n, 512)
    row = lambda wd: pl.BlockSpec((tm, wd), lambda i: (i, 0))
    return pl.pallas_call(
        _proj_in_kernel,
        grid=(n // tm,),
        in_specs=[row(d), _full(g.shape), _full(w.shape)],
        out_specs=[row(512), row(512), row(512), row(1024)],
        out_shape=[jax.ShapeDtypeStruct((n, wd), F32) for wd in (512, 512, 512, 1024)],
        compiler_params=_cparams("parallel"),
        name="proj_in",
    )(x, g, w)


def _rope128(x, c, sa, sb):
    w = x.shape[-1]
    return x * c + pltpu.roll(x, w - 16, 1) * sa + pltpu.roll(x, 16, 1) * sb


def _mla_prep_kernel(pa_ref, c_ref, sa_ref, sb_ref, gqn_ref, gkv_ref, wuq_ref, eq_ref, gq_ref,
                     gkr_ref, wuk_ref, ek_ref, gk_ref, wuv_ref,
                     q_ref, k_ref, v_ref, ckv_ref, kpe_ref):
    pa = pa_ref[...]
    c, sa, sb = c_ref[...], sa_ref[...], sb_ref[...]
    c4 = jnp.concatenate([c] * 4, axis=1)
    sa4 = jnp.concatenate([sa] * 4, axis=1)
    sb4 = jnp.concatenate([sb] * 4, axis=1)
    cq = _rms(pa[:, 0:256], Q_LORA) * gqn_ref[...]
    q = _dot(cq.astype(BF16), wuq_ref[...])
    qn = q * lax.rsqrt(_seg_sum(q * q, eq_ref[...]) + EPS) * gq_ref[...]
    q_ref[...] = (_rope128(qn, c4, sa4, sb4) * MLA_SCALE).astype(BF16)
    ckv = _rms(pa[:, 256:384]) * gkv_ref[...]
    ckv_ref[...] = ckv
    kp = _rms(pa[:, 384:512], QK_ROPE) * gkr_ref[...]
    kr = _rope128(kp, c, sa, sb)
    kpe_ref[...] = kr[:, 0:QK_ROPE]
    ckv_b = ckv.astype(BF16)
    kn = _dot(ckv_b, wuk_ref[...])
    kn = kn * lax.rsqrt(_seg_sum(kn * kn, ek_ref[...]) + EPS) * gk_ref[...]
    k_ref[...] = (kn + jnp.concatenate([kr] * 4, axis=1)).astype(BF16)
    v_ref[...] = _dot(ckv_b, wuv_ref[...]).astype(BF16)


def mla_prep(pa, tabs, wts, n_tab_blocks):
    n = pa.shape[0]
    c, sa, sb = tabs
    tm = c.shape[0] // n_tab_blocks
    assert n % tm == 0
    row = lambda wd: pl.BlockSpec((tm, wd), lambda i: (i, 0))
    tab = pl.BlockSpec((tm, LANES), lambda i: (i % n_tab_blocks, 0))
    return pl.pallas_call(
        _mla_prep_kernel,
        grid=(n // tm,),
        in_specs=[row(PA_W), tab, tab, tab] + [_full(w.shape) for w in wts],
        out_specs=[row(512), row(512), row(256), row(KV_LORA), row(QK_ROPE)],
        out_shape=[jax.ShapeDtypeStruct((n, 512), BF16), jax.ShapeDtypeStruct((n, 512), BF16),
                   jax.ShapeDtypeStruct((n, 256), BF16), jax.ShapeDtypeStruct((n, KV_LORA), F32),
                   jax.ShapeDtypeStruct((n, QK_ROPE), F32)],
        compiler_params=_cparams("parallel"),
        name="mla_prep",
    )(pa, c, sa, sb, *wts)


def _mla_attn_prompt_kernel(q_ref, k_ref, v_ref, o_ref, *, tq):
    i = pl.program_id(1)
    outs = []
    for h0 in range(0, N_HEADS, 2):
        heads = (h0, h0 + 1)

        def blk(j, carry, masked, heads=heads):
            off = pl.multiple_of(j * tq, tq)
            new = []
            for n, h in enumerate(heads):
                m, l, acc = carry[3 * n:3 * n + 3]
                kb = k_ref[pl.ds(off, tq), h * LANES:(h + 1) * LANES]
                vb = v_ref[pl.ds(off, tq), h * HEAD_DIM:(h + 1) * HEAD_DIM]
                s = _dot_nt(q_ref[:, h * LANES:(h + 1) * LANES], kb)
                if masked:
                    r = lax.broadcasted_iota(jnp.int32, (tq, tq), 0)
                    cc = lax.broadcasted_iota(jnp.int32, (tq, tq), 1)
                    s = jnp.where(cc <= r, s, NEG)
                m_new = jnp.maximum(m, jnp.max(s, axis=-1, keepdims=True))
                corr = jnp.exp(m - m_new)
                p = jnp.exp(s - m_new)
                l = l * corr + jnp.sum(p, axis=-1, keepdims=True)
                acc = acc * corr + _dot(p.astype(BF16), vb)
                new += [m_new, l, acc]
            return tuple(new)

        init = (jnp.full((tq, 1), NEG, F32), jnp.zeros((tq, 1), F32), jnp.zeros((tq, HEAD_DIM), F32)) * 2
        carry = lax.fori_loop(0, i, functools.partial(blk, masked=False), init)
        carry = blk(i, carry, True)
        outs += [carry[2] / carry[1], carry[5] / carry[4]]
    o_ref[...] = jnp.concatenate(outs, axis=1)


def mla_attn_prompt(q, k, v, b, t):
    tq = _tile(t, 512)
    nq = t // tq
    return pl.pallas_call(
        functools.partial(_mla_attn_prompt_kernel, tq=tq),
        grid=(b, nq),
        in_specs=[pl.BlockSpec((tq, 512), lambda bi, i: (bi * nq + i, 0)),
                  pl.BlockSpec((t, 512), lambda bi, i: (bi, 0)),
                  pl.BlockSpec((t, 256), lambda bi, i: (bi, 0))],
        out_specs=pl.BlockSpec((tq, 256), lambda bi, i: (bi * nq + i, 0)),
        out_shape=jax.ShapeDtypeStruct((b * t, 256), F32),
        compiler_params=_cparams("parallel", "arbitrary"),
        name="mla_attn_prompt",
    )(q, k, v)


SUB_PAGES = 4
PAGE_SLOTS = 3


def _mla_attn_sample_kernel(pt_ref, qbd_ref, qpe_ref, ckvn_ref, kpen_ref, wuk_ref, wukt_ref, wuv_ref, ckv_hbm,
                            kpe_hbm, o_ref, ckv_buf, kpe_buf, sem, m_scr, l_scr, acc_scr, lhs_scr, *, n_pg, tq,
                            layer):
    b_id = pl.program_id(0)
    p_id = pl.program_id(1)
    n_steps = pl.num_programs(1)
    step = b_id * n_steps + p_id
    n_total = pl.num_programs(0) * n_steps
    slot = step % PAGE_SLOTS

    def page_copies_of(st):
        return page_copies(st // n_steps, st % n_steps, st % PAGE_SLOTS)

    def page_copies(b, p, sl):
        copies = []
        for k in range(n_pg):
            phys = pt_ref[b, p * n_pg + k]
            copies.append(pltpu.make_async_copy(ckv_hbm.at[phys, layer], ckv_buf.at[sl, pl.ds(k * PAGE, PAGE), :],
                                                sem.at[sl, 0]))
            copies.append(pltpu.make_async_copy(kpe_hbm.at[phys, layer], kpe_buf.at[sl, :, pl.ds(k * PAGE, PAGE)],
                                                sem.at[sl, 1]))
        return copies

    for ahead in range(PAGE_SLOTS - 1):
        @pl.when((step == 0) & (ahead < n_total))
        def _(ahead=ahead):
            for c in page_copies_of(ahead):
                c.start()

    @pl.when(step + (PAGE_SLOTS - 1) < n_total)
    def _():
        for c in page_copies_of(step + (PAGE_SLOTS - 1)):
            c.start()

    qpe = qpe_ref[0]
    nr = N_HEADS * tq

    @pl.when(p_id == 0)
    def _():
        lhs_scr[0:GROUP_W, :] = wukt_ref[...]
        lhs_scr[GROUP_W:GROUP_W + nr, :] = _dot_nt(qbd_ref[0], wuk_ref[...]).astype(BF16)

    def scores(ckv, kpe_t):
        ckv_b = ckv.astype(BF16)
        out = _dot_nt(lhs_scr[...], ckv_b)
        rinv = []
        for h in range(N_HEADS):
            kn = out[h * HEAD_DIM:(h + 1) * HEAD_DIM]
            ss = jnp.sum(kn * kn, axis=0, keepdims=True) * (1.0 / HEAD_DIM)
            rinv.append(jnp.broadcast_to(lax.rsqrt(ss + EPS), (tq, ss.shape[1])))
        s = out[GROUP_W:GROUP_W + nr] * jnp.concatenate(rinv, axis=0) + _dot(qpe, kpe_t.astype(BF16))
        return s, ckv_b

    @pl.when(p_id == 0)
    def _():
        s, ckv_b = scores(ckvn_ref[0], kpen_ref[0])
        r = lax.broadcasted_iota(jnp.int32, s.shape, 0) % tq
        cc = lax.broadcasted_iota(jnp.int32, s.shape, 1)
        s = jnp.where(cc <= r, s, NEG)
        m = jnp.max(s, axis=-1, keepdims=True)
        p = jnp.exp(s - m)
        m_scr[...] = m
        l_scr[...] = jnp.sum(p, axis=-1, keepdims=True)
        acc_scr[...] = _dot(p.astype(BF16), ckv_b)

    pltpu.make_async_copy(ckv_buf.at[slot], ckv_buf.at[slot], sem.at[slot, 0]).wait()
    pltpu.make_async_copy(kpe_buf.at[slot], kpe_buf.at[slot], sem.at[slot, 1]).wait()

    n_sub = max(n_pg // SUB_PAGES, 1)
    keys = (n_pg // n_sub) * PAGE
    s_parts, ckv_parts = [], []
    for g in range(n_sub):
        s, ckv_b = scores(ckv_buf[slot, g * keys:(g + 1) * keys, :], kpe_buf[slot, :, g * keys:(g + 1) * keys])
        s_parts.append(s)
        ckv_parts.append(ckv_b)
    m = m_scr[...]
    m_new = m
    for s in s_parts:
        m_new = jnp.maximum(m_new, jnp.max(s, axis=-1, keepdims=True))
    corr = jnp.exp(m - m_new)
    l = l_scr[...] * corr
    acc = acc_scr[...] * corr
    for s, ckv_b in zip(s_parts, ckv_parts):
        p = jnp.exp(s - m_new)
        l = l + jnp.sum(p, axis=-1, keepdims=True)
        acc = acc + _dot(p.astype(BF16), ckv_b)
    m_scr[...] = m_new
    l_scr[...] = l
    acc_scr[...] = acc

    @pl.when(p_id == n_steps - 1)
    def _():
        lat = (acc / l).astype(BF16)
        full = _dot(lat, wuv_ref[...])
        lane_head = lax.broadcasted_iota(jnp.int32, (tq, 256), 1) // HEAD_DIM
        out = jnp.zeros((tq, 256), F32)
        for h in range(N_HEADS):
            out = jnp.where(lane_head == h, full[h * tq:(h + 1) * tq, :], out)
        o_ref[0] = out


def mla_attn_sample(page_table, qbd, qpe, ckv_new, kpe_new_t, wuk, wukt, wuv, cache_ckv, cache_kpe_t, layer,
                    pages_per_step):
    bs, n_pages = page_table.shape
    tq = qbd.shape[1] // N_HEADS
    n_pg = min(pages_per_step, n_pages)
    assert n_pages % n_pg == 0
    nr = N_HEADS * tq
    per_b = lambda shp: pl.BlockSpec((1,) + shp, lambda b, p, pt: (b, 0, 0))
    cst = lambda shp: pl.BlockSpec(shp, lambda b, p, pt: (0,) * len(shp))
    hbm = pl.BlockSpec(memory_space=pl.ANY)
    grid_spec = pltpu.PrefetchScalarGridSpec(
        num_scalar_prefetch=1,
        grid=(bs, n_pages // n_pg),
        in_specs=[per_b((nr, 256)), per_b((nr, QK_ROPE)), per_b((PAGE, KV_LORA)), per_b((QK_ROPE, PAGE)),
                  cst(wuk.shape), cst(wukt.shape), cst(wuv.shape), hbm, hbm],
        out_specs=pl.BlockSpec((1, tq, 256), lambda b, p, pt: (b, 0, 0)),
        scratch_shapes=[pltpu.VMEM((PAGE_SLOTS, n_pg * PAGE, KV_LORA), F32),
                        pltpu.VMEM((PAGE_SLOTS, QK_ROPE, n_pg * PAGE), F32),
                        pltpu.SemaphoreType.DMA((PAGE_SLOTS, 2)),
                        pltpu.VMEM((nr, 1), F32), pltpu.VMEM((nr, 1), F32), pltpu.VMEM((nr, KV_LORA), F32),
                        pltpu.VMEM((GROUP_W + nr, KV_LORA), BF16)],
    )
    return pl.pallas_call(
        functools.partial(_mla_attn_sample_kernel, n_pg=n_pg, tq=tq, layer=layer),
        grid_spec=grid_spec,
        out_shape=jax.ShapeDtypeStruct((bs, tq, 256), F32),
        compiler_params=_cparams("arbitrary", "arbitrary"),
        name="mla_attn_sample",
    )(page_table, qbd, qpe, ckv_new, kpe_new_t, wuk, wukt, wuv, cache_ckv, cache_kpe_t)


CONV_HALO = 32


def _conv_kernel(pb_ref, st_ref, cw_ref, cb_ref, e_ref, g_ref, b_ref, pw_ref, o_ref, st_out_ref, xbuf, *, tt, n_seq):
    j = pl.program_id(1)
    lo = CONV_HALO - (CONV_W - 1)

    @pl.when(j == 0)
    def _():
        xbuf[:, pl.ds(lo, CONV_W - 1), :] = st_ref[...]
        xbuf[:, pl.ds(CONV_HALO + tt, SUBLANES), :] = jnp.zeros((n_seq, SUBLANES, 256), F32)

    pb = pb_ref[...]
    glu = pb[:, 0:256] * _sigmoid(pb[:, 256:512])
    ys = []
    for s in range(n_seq):
        xbuf[s, pl.ds(CONV_HALO, tt), :] = glu[s * tt:(s + 1) * tt]
        y = jnp.zeros((tt, 256), F32) + cb_ref[...]
        for r in range(SUBLANES):
            z = None
            for k in range(CONV_W):
                if (lo + k) % SUBLANES == r:
                    term = xbuf[s, pl.ds(lo + k - r, tt + SUBLANES), :] * cw_ref[pl.ds(k, 1), :]
                    z = term if z is None else z + term
            y = y + z[r:r + tt]
        new_state = xbuf[s, pl.ds(lo + tt, CONV_W - 1), :]
        xbuf[s, pl.ds(lo, CONV_W - 1), :] = new_state
        st_out_ref[s] = new_state
        ys.append(y)
    y = jnp.concatenate(ys, axis=0) if n_seq > 1 else ys[0]
    e = e_ref[...]
    yc = y - _seg_sum(y, e)
    yn = yc * lax.rsqrt(_seg_sum(yc * yc, e) + LN_EPS) * g_ref[...] + b_ref[...]
    act = yn * _sigmoid(yn)
    o_ref[...] = _dot(act.astype(BF16), pw_ref[...])


SHORT_SEQ_ROWS = 128


def _seq_tiling(b, t):
    if t >= SHORT_SEQ_ROWS:
        return 1, _tile(t, 256)
    return _tile(b, SHORT_SEQ_ROWS // t), t


def conv_module(pb, state, wts, b, t):
    n_seq, tt = _seq_tiling(b, t)
    nt = t // tt
    return pl.pallas_call(
        functools.partial(_conv_kernel, tt=tt, n_seq=n_seq),
        grid=(b // n_seq, nt),
        in_specs=[pl.BlockSpec((n_seq * tt, 512), lambda bi, j: (bi * nt + j, 0)),
                  pl.BlockSpec((n_seq, CONV_W - 1, 256), lambda bi, j: (bi, 0, 0))]
                 + [_full(w.shape) for w in wts],
        out_specs=[pl.BlockSpec((n_seq * tt, 256), lambda bi, j: (bi * nt + j, 0)),
                   pl.BlockSpec((n_seq, CONV_W - 1, 256), lambda bi, j: (bi, 0, 0))],
        out_shape=[jax.ShapeDtypeStruct((b * t, 256), F32), jax.ShapeDtypeStruct((b, CONV_W - 1, 256), F32)],
        scratch_shapes=[pltpu.VMEM((n_seq, CONV_HALO + tt + SUBLANES, 256), F32)],
        compiler_params=_cparams("parallel", "arbitrary"),
        name="conv_module",
    )(pb, state, *wts)


def _sgu_kernel(pc_ref, g_ref, b_ref, w_ref, bias_ref, o_ref, v_ref, *, n_chunks):
    x = pc_ref[...]
    z = 0.5 * x * (1.0 + jnp.tanh(0.7978845608028654 * (x + 0.044715 * (x * x * x))))
    u = z[:, 0:256]
    v = z[:, 256:512]
    vc = v - jnp.mean(v, axis=-1, keepdims=True)
    v = vc * lax.rsqrt(jnp.mean(vc * vc, axis=-1, keepdims=True) + LN_EPS) * g_ref[...] + b_ref[...]
    v_ref[...] = v
    lane_head = lax.broadcasted_iota(jnp.int32, (CHUNK, 256), 1) // HEAD_DIM
    for c in range(n_chunks):
        vcb = v[c * CHUNK:(c + 1) * CHUNK, :]
        sv = bias_ref[...]
        for h in range(N_HEADS):
            sv = sv + _dot(w_ref[h], jnp.where(lane_head == h, vcb, 0.0).astype(BF16))
        o_ref[pl.ds(c * CHUNK, CHUNK), :] = u[c * CHUNK:(c + 1) * CHUNK, :] * sv


def sgu(pc, wts):
    n = pc.shape[0]
    tm = _tile(n, 512)
    row = lambda wd: pl.BlockSpec((tm, wd), lambda i: (i, 0))
    return pl.pallas_call(
        functools.partial(_sgu_kernel, n_chunks=tm // CHUNK),
        grid=(n // tm,),
        in_specs=[row(512)] + [_full(w.shape) for w in wts],
        out_specs=[row(256), row(256)],
        out_shape=[jax.ShapeDtypeStruct((n, 256), F32), jax.ShapeDtypeStruct((n, 256), F32)],
        compiler_params=_cparams("parallel"),
        name="sgu",
    )(pc, *wts)


RW_HALO = 8


def _rwkv_prep_kernel(pd_ref, sh_ref, mu_ref, w0_ref, w2_ref, a0_ref, a2_ref, g2_ref, kkp_ref, ka_ref, rk_ref,
                      e_ref, r_ref, w_ref, k_ref, v_ref, kk_ref, kka_ref, g_ref, bv_ref, xbuf, *, tt, n_seq):
    j = pl.program_id(1)

    @pl.when(j == 0)
    def _():
        xbuf[:, pl.ds(RW_HALO - 1, 1), :] = sh_ref[...]

    pd = pd_ref[...]
    prevs = []
    for s in range(n_seq):
        rows = pd[s * tt:(s + 1) * tt]
        xbuf[s, pl.ds(RW_HALO, tt), :] = rows
        prevs.append(xbuf[s, pl.ds(RW_HALO - 1, tt), :])
        xbuf[s, pl.ds(RW_HALO - 1, 1), :] = rows[tt - 1:tt, :]
    prev = jnp.concatenate(prevs, axis=0) if n_seq > 1 else prevs[0]
    xs = pd + (prev - pd) * mu_ref[...]
    r = xs[:, 0:256]
    k = xs[:, 256:512]
    v = xs[:, 512:768]
    xwa = xs[:, 768:896]
    xg = xs[:, 896:1024]
    z = -(w0_ref[...] + _dot(jnp.tanh(xwa).astype(BF16), w2_ref[...]))
    softplus = jnp.maximum(z, 0.0) + jnp.log(1.0 + jnp.exp(-jnp.abs(z)))
    w_ref[...] = jnp.exp(-jnp.exp(-softplus - 0.5))
    a = _sigmoid(a0_ref[...] + _dot(xwa.astype(BF16), a2_ref[...]))
    g_ref[...] = _dot(_sigmoid(xg).astype(BF16), g2_ref[...])
    e = e_ref[...]
    kk = k * kkp_ref[...]
    kk = kk * lax.rsqrt(_seg_sum(kk * kk, e) + 1e-12)
    k2 = k * (1.0 + (a - 1.0) * ka_ref[...])
    r_ref[...] = r
    k_ref[...] = k2
    v_ref[...] = v
    kk_ref[...] = kk
    kka_ref[...] = kk * a
    bv_ref[...] = _seg_sum(r * k2 * rk_ref[...], e) * v


def rwkv_prep(pd, shift, wts, b, t):
    n_seq, tt = _seq_tiling(b, t)
    nt = t // tt
    row = lambda wd: pl.BlockSpec((n_seq * tt, wd), lambda bi, j: (bi * nt + j, 0))
    return pl.pallas_call(
        functools.partial(_rwkv_prep_kernel, tt=tt, n_seq=n_seq),
        grid=(b // n_seq, nt),
        in_specs=[row(PD_W), pl.BlockSpec((n_seq, 1, PD_W), lambda bi, j: (bi, 0, 0))]
                 + [_full(w.shape) for w in wts],
        out_specs=[row(256)] * 8,
        out_shape=[jax.ShapeDtypeStruct((b * t, 256), F32)] * 8,
        scratch_shapes=[pltpu.VMEM((n_seq, RW_HALO + tt, PD_W), F32)],
        compiler_params=_cparams("parallel", "arbitrary"),
        name="rwkv_prep",
    )(pd, shift, *wts)


RW_BLOCK = 128


def _pack_bf16_pair(a, b):
    ua = lax.bitcast_convert_type(a.astype(BF16).astype(F32), jnp.uint32)
    ub = lax.bitcast_convert_type(b.astype(BF16).astype(F32), jnp.uint32)
    return lax.bitcast_convert_type(ua | (ub >> 16), jnp.int32)


def _unpack_bf16_pair(word):
    u = lax.bitcast_convert_type(word, jnp.uint32)
    return (lax.bitcast_convert_type(u & jnp.uint32(0xFFFF0000), F32),
            lax.bitcast_convert_type(u << 16, F32))


def _rwkv_scan_kernel(w_ref, kk_ref, kka_ref, k_ref, r_ref, v_ref, s0_ref, e_ref, y_ref, sf_ref, s_scr, col_scr, *,
                      nb, sblk):
    c = pl.program_id(1)

    @pl.when(c == 0)
    def _():
        s_scr[...] = s0_ref[...]

    lane = lax.broadcasted_iota(jnp.int32, (HEAD_DIM, LANES), 1)
    low = lane < HEAD_DIM
    n_half = 2 if sblk > HEAD_DIM else 1
    for b in range(nb):
        for p in range(2):
            blk = lambda ref: ref[b, :, p * LANES:(p + 1) * LANES]
            x = _pack_bf16_pair(blk(kk_ref), blk(kka_ref))
            if sblk < LANES:
                x = jnp.concatenate([x, jnp.zeros((LANES - sblk, LANES), jnp.int32)], axis=0)
            xt = x.T
            h0, h1 = xt[0:HEAD_DIM], xt[HEAD_DIM:LANES]
            col_scr[b, p, 0] = jnp.where(low, h0, pltpu.roll(h1, HEAD_DIM, 1))
            if n_half == 2:
                col_scr[b, p, 1] = jnp.where(low, pltpu.roll(h0, HEAD_DIM, 1), h1)

    base = jnp.where(low, 0, HEAD_DIM)
    diag = (lax.broadcasted_iota(jnp.int32, (HEAD_DIM, 256), 1) % HEAD_DIM
            == lax.broadcasted_iota(jnp.int32, (HEAD_DIM, 256), 0))
    for half in range(n_half):
        n_groups = min(sblk - half * HEAD_DIM, HEAD_DIM) // SUBLANES

        def group(gi, carry, half=half):
            row0 = pl.multiple_of(half * HEAD_DIM + gi * SUBLANES, SUBLANES)
            for b in range(nb):
                v8 = v_ref[b, pl.ds(row0, SUBLANES), :]
                w8 = w_ref[b, pl.ds(row0, SUBLANES), :]
                w8_hi = w8.astype(BF16).astype(F32)
                rows8 = (k_ref[b, pl.ds(row0, SUBLANES), :], r_ref[b, pl.ds(row0, SUBLANES), :], w8_hi, w8 - w8_hi)
                on_diag = [jnp.where(diag, jnp.broadcast_to(x8[i:i + 1, :], (HEAD_DIM, 256)), 0.0).astype(BF16)
                           for i in range(SUBLANES) for x8 in rows8]
                spread = _dot(jnp.concatenate(on_diag, axis=0), e_ref[...])
                ys = []
                for p in range(2):
                    st = s_scr[b, :, p * LANES:(p + 1) * LANES]
                    yp = []
                    for i in range(SUBLANES):
                        idx = base + (gi * SUBLANES + i)
                        kk, kka = _unpack_bf16_pair(jnp.take_along_axis(col_scr[b, p, half], idx, axis=1))
                        part = lambda n: spread[(4 * i + n) * HEAD_DIM:(4 * i + n + 1) * HEAD_DIM,
                                                p * LANES:(p + 1) * LANES]
                        k, r, w = part(0), part(1), part(2) + part(3)
                        vrow = v8[i:i + 1, p * LANES:(p + 1) * LANES]
                        sa = -jnp.sum(st * kk, axis=0, keepdims=True)
                        st = st * w + kka * sa + k * vrow
                        yp.append(jnp.sum(st * r, axis=0, keepdims=True))
                    s_scr[b, :, p * LANES:(p + 1) * LANES] = st
                    ys.append(jnp.concatenate(yp, axis=0))
                y_ref[b, pl.ds(row0, SUBLANES), :] = jnp.concatenate(ys, axis=1)
            return carry

        lax.fori_loop(0, n_groups, group, 0)

    @pl.when(c == pl.num_programs(1) - 1)
    def _():
        sf_ref[...] = s_scr[...]


def rwkv_scan(seqs, v, s0, nb):
    b, t, _ = v.shape
    e_heads = _head_matrix(1.0)
    sblk = min(t, RW_BLOCK)
    assert b % nb == 0 and t % sblk == 0 and sblk % SUBLANES == 0 and (sblk <= HEAD_DIM or sblk == RW_BLOCK)
    st_spec = pl.BlockSpec((nb, HEAD_DIM, 256), lambda bi, c: (bi, 0, 0))
    seq_spec = pl.BlockSpec((nb, sblk, 256), lambda bi, c: (bi, c, 0))
    return pl.pallas_call(
        functools.partial(_rwkv_scan_kernel, nb=nb, sblk=sblk),
        grid=(b // nb, t // sblk),
        in_specs=[seq_spec] * 6 + [st_spec, _full(e_heads.shape)],
        out_specs=[seq_spec, st_spec],
        out_shape=[jax.ShapeDtypeStruct((b, t, 256), F32), jax.ShapeDtypeStruct((b, HEAD_DIM, 256), F32)],
        scratch_shapes=[pltpu.VMEM((nb, HEAD_DIM, 256), F32),
                        pltpu.VMEM((nb, 2, 2, HEAD_DIM, LANES), jnp.int32)],
        compiler_params=_cparams("parallel", "arbitrary"),
        name="rwkv_scan",
    )(*seqs, v, s0, e_heads)


def _mix_out_kernel(x_ref, oa_ref, ob_ref, oc_ref, y_ref, bv_ref, g_ref, e_ref, lg_ref, lb_ref, on_ref, w_ref,
                    nx_ref, wq_ref, qg_ref, o_ref, q_ref):
    e = e_ref[...]
    y = y_ref[...]
    yc = y - _seg_sum(y, e)
    yn = yc * lax.rsqrt(_seg_sum(yc * yc, e) + RW_LN_EPS) * lg_ref[...] + lb_ref[...]
    od = (yn + bv_ref[...]) * g_ref[...]
    acc = x_ref[...]
    for gi, o in enumerate((oa_ref[...], ob_ref[...], oc_ref[...], od)):
        on = (_rms(o) * on_ref[:, gi * 256:(gi + 1) * 256]).astype(BF16)
        acc = acc + _dot(on, w_ref[pl.ds(gi * 256, 256), :])
    o_ref[...] = acc
    q = _dot((_rms(acc) * nx_ref[...]).astype(BF16), wq_ref[...])
    heads = [_rms(q[:, h * X_HEAD_DIM:(h + 1) * X_HEAD_DIM]) * qg_ref[...] * X_HEAD_DIM ** -0.5
             for h in range(X_HEADS)]
    q_ref[...] = jnp.concatenate(heads, axis=1).astype(q_ref.dtype)


def mix_out(x, oa, ob, oc, y, bv, g, wts, q_dtype):
    n, d = x.shape
    tm = _tile(n, 512)
    row = lambda wd: pl.BlockSpec((tm, wd), lambda i: (i, 0))
    return pl.pallas_call(
        _mix_out_kernel,
        grid=(n // tm,),
        in_specs=[row(d)] + [row(256)] * 6 + [_full(w.shape) for w in wts],
        out_specs=[row(d), row(512)],
        out_shape=[jax.ShapeDtypeStruct((n, d), F32), jax.ShapeDtypeStruct((n, 512), q_dtype)],
        compiler_params=_cparams("parallel"),
        name="mix_out",
    )(x, oa, ob, oc, y, bv, g, *wts)


XATTN_SHORT_ROWS = 64


def _xattn_kernel(x_ref, q_ref, k_ref, v_ref, wo_ref, o_ref, *, n_seq, rows, n_mem):
    seq_outs = []
    for s in range(n_seq):
        qs = [q_ref[s * rows:(s + 1) * rows, h * X_HEAD_DIM:(h + 1) * X_HEAD_DIM].astype(BF16)
              for h in range(X_HEADS)]
        if n_seq == 1:
            outs = []
            for h in range(X_HEADS):
                head_rows = pl.ds(h, n_mem, stride=X_HEADS)
                sc = _dot_nt(qs[h], k_ref[s, head_rows, :].astype(BF16))
                p = jnp.exp(sc - jnp.max(sc, axis=-1, keepdims=True))
                p = p / jnp.sum(p, axis=-1, keepdims=True)
                outs.append(_dot(p.astype(BF16), v_ref[s, head_rows, :].astype(BF16)))
        else:
            sc = _dot_nt(jnp.concatenate(qs, axis=0), k_ref[s].astype(BF16))
            row_head = lax.broadcasted_iota(jnp.int32, sc.shape, 0) // rows
            col_head = lax.broadcasted_iota(jnp.int32, sc.shape, 1) % X_HEADS
            sc = jnp.where(row_head == col_head, sc, NEG)
            p = jnp.exp(sc - jnp.max(sc, axis=-1, keepdims=True))
            p = p / jnp.sum(p, axis=-1, keepdims=True)
            o = _dot(p.astype(BF16), v_ref[s].astype(BF16))
            outs = [o[h * rows:(h + 1) * rows] for h in range(X_HEADS)]
        seq_outs.append(jnp.concatenate(outs, axis=1))
    xo = jnp.concatenate(seq_outs, axis=0).astype(BF16)
    o_ref[...] = x_ref[...] + _dot(xo, wo_ref[...])


def xattn(x, q, mem_k, mem_v, wo, b, t, kv_seq0):
    n, d = x.shape
    n_mem = mem_k.shape[1] // X_HEADS
    if t >= XATTN_SHORT_ROWS:
        n_seq, rows = 1, _tile(t, 512)
    else:
        n_seq, rows = _tile(b, XATTN_SHORT_ROWS // t), t
    nq = t // rows
    assert kv_seq0 % n_seq == 0
    row = lambda wd: pl.BlockSpec((n_seq * rows, wd), lambda bi, i: (bi * nq + i, 0))
    kv = pl.BlockSpec((n_seq, n_mem * X_HEADS, X_HEAD_DIM), lambda bi, i: (kv_seq0 // n_seq + bi, 0, 0))
    return pl.pallas_call(
        functools.partial(_xattn_kernel, n_seq=n_seq, rows=rows, n_mem=n_mem),
        grid=(b // n_seq, nq),
        in_specs=[row(d), row(512), kv, kv, _full(wo.shape)],
        out_specs=row(d),
        out_shape=jax.ShapeDtypeStruct((n, d), F32),
        compiler_params=_cparams("parallel", "arbitrary"),
        name="xattn",
    )(x, q, mem_k, mem_v, wo)


def _mem_kv_kernel(x_ref, g_ref, w_ref, hg_ref, k_ref, v_ref, *, tm):
    hb = (_rms(x_ref[...]) * g_ref[...]).astype(BF16)
    y = _dot(hb, w_ref[...])
    for h in range(X_HEADS):
        head_rows = pl.ds(h, tm, stride=X_HEADS)
        k_ref[head_rows, :] = _rms(y[:, h * X_HEAD_DIM:(h + 1) * X_HEAD_DIM]) * hg_ref[...]
        v_ref[head_rows, :] = y[:, (X_HEADS + h) * X_HEAD_DIM:(X_HEADS + h + 1) * X_HEAD_DIM]


def mem_kv(x, g, w, hg):
    n, d = x.shape
    tm = _tile(n, 512)
    out = pl.BlockSpec((tm * X_HEADS, X_HEAD_DIM), lambda i: (i, 0))
    return pl.pallas_call(
        functools.partial(_mem_kv_kernel, tm=tm),
        grid=(n // tm,),
        in_specs=[pl.BlockSpec((tm, d), lambda i: (i, 0)), _full(g.shape), _full(w.shape), _full(hg.shape)],
        out_specs=[out, out],
        out_shape=[jax.ShapeDtypeStruct((n * X_HEADS, X_HEAD_DIM), F32)] * 2,
        compiler_params=_cparams("parallel"),
        name="mem_kv",
    )(x, g, w, hg)


FFN_CHUNK = 256


def _ffn_kernel(x_ref, g_ref, wg_ref, wu_ref, wo_ref, o_ref, *, n_chunks):
    x = x_ref[...]
    hb = (_rms(x) * g_ref[...]).astype(BF16)
    acc = x
    for c in range(n_chunks):
        a = _dot(hb, wg_ref[c])
        u = _dot(hb, wu_ref[c])
        acc = acc + _dot((a * _sigmoid(a) * u).astype(BF16), wo_ref[c])
    o_ref[...] = acc


def ffn(x, g, wg, wu, wo):
    n, d = x.shape
    tm = _tile(n, 512)
    row = pl.BlockSpec((tm, d), lambda i: (i, 0))
    return pl.pallas_call(
        functools.partial(_ffn_kernel, n_chunks=wg.shape[0]),
        grid=(n // tm,),
        in_specs=[row, _full(g.shape), _full(wg.shape), _full(wu.shape), _full(wo.shape)],
        out_specs=row,
        out_shape=jax.ShapeDtypeStruct((n, d), F32),
        compiler_params=_cparams("parallel"),
        name="ffn",
    )(x, g, wg, wu, wo)


def _seg_matrix(seg_ids, seg_len):
    s = np.asarray(seg_ids)
    m = (s[:, None] == s[None, :]) & (s[:, None] >= 0)
    return jnp.asarray(m.astype(np.float32) / np.asarray(seg_len, np.float32)[None, :], BF16)


def _mla_segments():
    lane = np.arange(512)
    blk, off = lane // 128, lane % 128
    q_ids = np.where(off < 32, 2 * blk, np.where(off < 64, -1, 2 * blk + 1))
    q_len = np.where(off < 32, 32.0, 64.0)
    k_ids = np.where(off < 64, -1, blk)
    k_len = np.full(512, 64.0)
    return _seg_matrix(q_ids, q_len), _seg_matrix(k_ids, k_len)


def _head_matrix(scale_len):
    lane = np.arange(256)
    return _seg_matrix(lane // HEAD_DIM, np.full(256, scale_len))


def _rope_tables(pos):
    half = QK_ROPE // 2
    inv = jnp.power(ROPE_BASE, -jnp.arange(half, dtype=F32) / half)
    ang = pos.astype(F32)[:, None] * inv[None, :]
    cos, sin = jnp.cos(ang), jnp.sin(ang)
    n = pos.shape[0]
    z = lambda w: jnp.zeros((n, w), F32)
    c = jnp.concatenate([cos, cos, jnp.ones((n, LANES - QK_ROPE), F32)], axis=1)
    sa = jnp.concatenate([-sin, z(LANES - half)], axis=1)
    sb = jnp.concatenate([z(half), sin, z(LANES - QK_ROPE)], axis=1)
    return c, sa, sb


def _layer_weights(p):
    row = lambda v: v.reshape(1, -1).astype(F32)
    zc = lambda a, w: jnp.zeros((a.shape[0], w), a.dtype)
    w_in = p['w_in']
    c1, c2, c3 = 352, 352 + 512, 352 + 1024
    pa = w_in[:, :c1]
    w_in_p = jnp.concatenate(
        [pa[:, :Q_LORA], zc(pa, 64), pa[:, Q_LORA:Q_LORA + KV_LORA], pa[:, Q_LORA + KV_LORA:], zc(pa, 96),
         w_in[:, c1:c2], w_in[:, c2:c3], w_in[:, c3:]], axis=1).astype(BF16)

    wuq = p['mla_w_uq'].reshape(Q_LORA, N_HEADS, QK_NOPE + QK_ROPE)
    wuq = jnp.concatenate([wuq[:, :, QK_NOPE:], jnp.zeros((Q_LORA, N_HEADS, 32), F32), wuq[:, :, :QK_NOPE]], axis=2)
    wuq = jnp.concatenate([wuq.reshape(Q_LORA, 512), jnp.zeros((256 - Q_LORA, 512), F32)], axis=0).astype(BF16)
    wuk = p['mla_w_uk'].reshape(KV_LORA, N_HEADS, QK_NOPE)
    wuk_p = jnp.concatenate([jnp.zeros((KV_LORA, N_HEADS, 64), F32), wuk], axis=2).reshape(KV_LORA, 512).astype(BF16)
    blk = lambda a, b_, c_: jnp.tile(jnp.concatenate([a, b_, c_]), N_HEADS).reshape(1, 512)
    z32, z64 = jnp.zeros((32,), F32), jnp.zeros((64,), F32)
    gq = blk(p['mla_gq_rope'], z32, p['mla_gq_nope'])
    gk = blk(z32, z32, p['mla_gk_nope'])
    gqn = jnp.concatenate([p['mla_q_norm'], z64]).reshape(1, 256)
    gkr = jnp.concatenate([p['mla_gk_rope'], jnp.zeros((96,), F32)]).reshape(1, LANES)
    eq, ek = _mla_segments()
    mla = (gqn, row(p['mla_kv_norm']), wuq, eq, gq, gkr, wuk_p, ek, gk, p['mla_w_uv'].astype(BF16))

    e64 = _head_matrix(64.0)
    conv = (p['conv_w'], row(p['conv_b']), e64, row(p['conv_norm_g']), row(p['conv_norm_b']),
            p['conv_pw'].astype(BF16))

    z64r = jnp.zeros((64, GROUP_W), F32)
    rw = (row(p['rw_mu']),
          row(p['rw_w0']), jnp.concatenate([p['rw_w2'], z64r], axis=0).astype(BF16),
          row(p['rw_a0']), jnp.concatenate([z64r, p['rw_a2']], axis=0).astype(BF16),
          p['rw_g2'].astype(BF16), row(p['rw_kk']), row(p['rw_ka']), row(p['rw_rk']), _head_matrix(1.0))

    mix = (e64, row(p['rw_ln_g']), row(p['rw_ln_b']), row(p['out_norm']), p['w_out'].astype(BF16))

    d_ff = p['w_ffn_out'].shape[0]
    nck = d_ff // FFN_CHUNK
    d = w_in.shape[0]
    wg = p['w_ffn_in'][:, :d_ff].reshape(d, nck, FFN_CHUNK).transpose(1, 0, 2).astype(BF16)
    wu = p['w_ffn_in'][:, d_ff:].reshape(d, nck, FFN_CHUNK).transpose(1, 0, 2).astype(BF16)
    wo = p['w_ffn_out'].reshape(nck, FFN_CHUNK, d).astype(BF16)

    return dict(
        norm_mix=row(p['norm_mix']), w_in=w_in_p, mla=mla, conv=conv, rw=rw, mix=mix,
        wuk=p['mla_w_uk'].astype(BF16), wuv=p['mla_w_uv'].astype(BF16), gk_nope=p['mla_gk_nope'],
        sgu_ln=(row(p['sgu_norm_g']), row(p['sgu_norm_b'])), sgu_w=p['sgu_w'], sgu_b=p['sgu_b'],
        norm_x=row(p['norm_x']), wq=p['wq_x'].astype(BF16), xq_norm=row(p['xq_norm']),
        mem_norm=row(p['mem_norm']), wkv=jnp.concatenate([p['wk_x'], p['wv_x']], axis=1).astype(BF16),
        xk_norm=row(p['xk_norm']), wo_x=p['wo_x'].astype(BF16),
        norm_ffn=row(p['norm_ffn']), wg=wg, wu=wu, wo=wo)


def _sgu_weights(lw, t):
    l = min(t, CHUNK)
    w = lw['sgu_w'][:, :l, :l] * jnp.tril(jnp.ones((l, l), F32))
    reps = CHUNK // l
    if reps > 1:
        w = jnp.einsum('ab,hij->haibj', jnp.eye(reps, dtype=F32), w).reshape(N_HEADS, CHUNK, CHUNK)
    bias = jnp.tile(lw['sgu_b'][:, :l].T, (reps, 1))
    bias = jnp.repeat(bias, HEAD_DIM, axis=1)
    return lw['sgu_ln'] + (w.astype(BF16), bias)


def _trunk_layer(x, b, t, lw, tabs, n_tab_blocks, attend, mem_k, mem_v, kv_seq0, conv_state, shift_state, wkv_state,
                 scan_nb):
    pa, pb, pc, pd = proj_in(x, lw['norm_mix'], lw['w_in'])
    q, k, v, ckv, kpe = mla_prep(pa, tabs, lw['mla'], n_tab_blocks)
    oa = attend(q, k, v, ckv, kpe)
    ob, conv_new = conv_module(pb, conv_state, lw['conv'], b, t)
    oc, v_sgu = sgu(pc, _sgu_weights(lw, t))
    r, w, k2, vv, kk, kka, g, bv = rwkv_prep(pd, shift_state.reshape(b, 1, PD_W), lw['rw'], b, t)
    seqs = [a.reshape(b, t, 256) for a in (w, kk, kka, k2, r)]
    s0 = wkv_state.transpose(0, 3, 1, 2).reshape(b, HEAD_DIM, 256)
    y, s_fin = rwkv_scan(seqs, vv.reshape(b, t, 256), s0, scan_nb)
    wkv_new = s_fin.reshape(b, HEAD_DIM, N_HEADS, HEAD_DIM).transpose(0, 2, 3, 1)
    shift_new = pd.reshape(b, t, PD_W)[:, -1]
    q_dtype = BF16 if t % 16 == 0 else F32
    x, qx = mix_out(x, oa, ob, oc, y.reshape(b * t, 256), bv, g,
                    lw['mix'] + (lw['norm_x'], lw['wq'], lw['xq_norm']), q_dtype)
    x = xattn(x, qx, mem_k, mem_v, lw['wo_x'], b, t, kv_seq0)
    x = ffn(x, lw['norm_ffn'], lw['wg'], lw['wu'], lw['wo'])
    return x, ckv, kpe, conv_new, shift_new, wkv_new, v_sgu


def kernel(x_prompt, x_sample, mem_prompt, cache_ckv, cache_kpe, cache_mem_k, cache_mem_v, state_conv, state_shift, state_wkv, page_table, norm_mix, w_in, mla_q_norm, mla_kv_norm, mla_w_uq, mla_w_uk, mla_w_uv, mla_gq_nope, mla_gq_rope, mla_gk_nope, mla_gk_rope, conv_w, conv_b, conv_norm_g, conv_norm_b, conv_pw, sgu_norm_g, sgu_norm_b, sgu_w, sgu_b, rw_mu, rw_w0, rw_w2, rw_a0, rw_a2, rw_g2, rw_kk, rw_ka, rw_rk, rw_ln_g, rw_ln_b, out_norm, w_out, norm_x, mem_norm, wq_x, wk_x, wv_x, xq_norm, xk_norm, wo_x, norm_ffn, w_ffn_in, w_ffn_out):
    params = dict(
        norm_mix=norm_mix, w_in=w_in, mla_q_norm=mla_q_norm, mla_kv_norm=mla_kv_norm, mla_w_uq=mla_w_uq,
        mla_w_uk=mla_w_uk, mla_w_uv=mla_w_uv, mla_gq_nope=mla_gq_nope, mla_gq_rope=mla_gq_rope,
        mla_gk_nope=mla_gk_nope, mla_gk_rope=mla_gk_rope, conv_w=conv_w, conv_b=conv_b, conv_norm_g=conv_norm_g,
        conv_norm_b=conv_norm_b, conv_pw=conv_pw, sgu_norm_g=sgu_norm_g, sgu_norm_b=sgu_norm_b, sgu_w=sgu_w,
        sgu_b=sgu_b, rw_mu=rw_mu, rw_w0=rw_w0, rw_w2=rw_w2, rw_a0=rw_a0, rw_a2=rw_a2, rw_g2=rw_g2, rw_kk=rw_kk,
        rw_ka=rw_ka, rw_rk=rw_rk, rw_ln_g=rw_ln_g, rw_ln_b=rw_ln_b, out_norm=out_norm, w_out=w_out, norm_x=norm_x,
        mem_norm=mem_norm, wq_x=wq_x, wk_x=wk_x, wv_x=wv_x, xq_norm=xq_norm, xk_norm=xk_norm, wo_x=wo_x,
        norm_ffn=norm_ffn, w_ffn_in=w_ffn_in, w_ffn_out=w_ffn_out)
    depth = w_in.shape[0]
    bp, tp, d = x_prompt.shape
    bs, ts, _ = x_sample.shape
    n_mem = mem_prompt.shape[1]
    n_pages = page_table.shape[1]
    past_len = n_pages * PAGE

    tm_p = _tile(tp, 512)
    tabs_p = _rope_tables(jnp.arange(tp, dtype=jnp.int32))
    tm_s = _tile(bs * ts, 512)
    tabs_s = tuple(jnp.tile(a, (tm_s // ts, 1)) for a in _rope_tables(past_len + jnp.arange(ts, dtype=jnp.int32)))
    cache_kpe_t = jnp.swapaxes(cache_kpe, 2, 3)
    cache_k_rows = cache_mem_k.reshape(depth * bs, n_mem * X_HEADS, X_HEAD_DIM)
    cache_v_rows = cache_mem_v.reshape(depth * bs, n_mem * X_HEADS, X_HEAD_DIM)
    y_p = x_prompt.reshape(bp * tp, d)
    y_s = x_sample.reshape(bs * ts, d)
    mem_flat = mem_prompt.reshape(bp * n_mem, d)
    zeros_conv = jnp.zeros((bp, CONV_W - 1, GROUP_W), F32)
    zeros_shift = jnp.zeros((bp, PD_W), F32)
    zeros_wkv = jnp.zeros((bp, N_HEADS, HEAD_DIM, HEAD_DIM), F32)
    outs_p, outs_s, memk_l, memv_l = [], [], [], []
    for l in range(depth):
        lw = _layer_weights({k_: v_[l] for k_, v_ in params.items()})

        mk, mv = mem_kv(mem_flat, lw['mem_norm'], lw['wkv'], lw['xk_norm'])
        memk_l.append(mk.reshape(bp, n_mem, X_HEADS, X_HEAD_DIM))
        memv_l.append(mv.reshape(bp, n_mem, X_HEADS, X_HEAD_DIM))

        def attend_p(q, k, v, ckv, kpe):
            return mla_attn_prompt(q, k, v, bp, tp)

        kv_rows = n_mem * X_HEADS
        res = _trunk_layer(y_p, bp, tp, lw, tabs_p, tp // tm_p, attend_p, mk.reshape(bp, kv_rows, X_HEAD_DIM),
                           mv.reshape(bp, kv_rows, X_HEAD_DIM), 0, zeros_conv, zeros_shift, zeros_wkv,
                           _tile(bp, SCAN_NB_PROMPT))
        y_p = res[0]
        outs_p.append(res[1:])

        def attend_s(q, k, v, ckv, kpe, l=l, lw=lw):
            qf = q.astype(F32).reshape(bs, ts, N_HEADS, LANES)
            qn = (qf[..., 64:] * lw['gk_nope']).transpose(0, 2, 1, 3)
            qbd = jnp.einsum('bhtj,hg->bhtgj', qn, jnp.eye(N_HEADS, dtype=F32)).reshape(bs, N_HEADS * ts, 256)
            qpe = qf[..., :QK_ROPE].transpose(0, 2, 1, 3).reshape(bs, N_HEADS * ts, QK_ROPE)
            pad = lambda a: jnp.pad(a.reshape(bs, ts, -1), ((0, 0), (0, PAGE - ts), (0, 0)))
            o = mla_attn_sample(page_table, qbd.astype(BF16), qpe.astype(BF16), pad(ckv),
                                pad(kpe).transpose(0, 2, 1), lw['wuk'], lw['wuk'].T, lw['wuv'], cache_ckv,
                                cache_kpe_t, l, PAGES_PER_STEP)
            return o.reshape(bs * ts, 256)

        res = _trunk_layer(y_s, bs, ts, lw, tabs_s, 1, attend_s, cache_k_rows, cache_v_rows, l * bs,
                           state_conv[l], state_shift[l], state_wkv[l], _tile(bs, SCAN_NB_SAMPLE))
        y_s = res[0]
        outs_s.append(res[1:])

    n_pp = tp // PAGE
    stack = lambda outs, i, ax: jnp.stack([o[i] for o in outs], axis=ax)
    ckv_prompt = stack(outs_p, 0, 0).reshape(depth, bp, n_pp, PAGE, KV_LORA).transpose(1, 2, 0, 3, 4)
    kpe_prompt = stack(outs_p, 1, 0).reshape(depth, bp, n_pp, PAGE, QK_ROPE).transpose(1, 2, 0, 3, 4)
    ckv_sample = stack(outs_s, 0, 0).reshape(depth, bs, ts, KV_LORA).transpose(1, 0, 2, 3)
    kpe_sample = stack(outs_s, 1, 0).reshape(depth, bs, ts, QK_ROPE).transpose(1, 0, 2, 3)
    return (y_p.reshape(bp, tp, d), y_s.reshape(bs, ts, d), ckv_prompt, kpe_prompt, ckv_sample, kpe_sample,
            jnp.stack(memk_l, 0), jnp.stack(memv_l, 0),
            stack(outs_p, 2, 0), stack(outs_s, 2, 0), stack(outs_p, 3, 0), stack(outs_s, 3, 0),
            stack(outs_p, 4, 0), stack(outs_s, 4, 0),
            stack(outs_s, 5, 0).reshape(depth, bs, ts, GROUP_W))
```

```python
import functools

import numpy as np
import jax
import jax.numpy as jnp
from jax import lax
from jax.experimental import pallas as pl
from jax.experimental.pallas import tpu as pltpu

F32 = jnp.float32
BF16 = jnp.bfloat16

EPS = 1e-6
LN_EPS = 1e-5
RW_LN_EPS = 64e-5
NEG = -1e30
ROPE_BASE = 10000.0

LANES = 128
SUBLANES = 8
VMEM_LIMIT_BYTES = 56 * 1024 * 1024

GROUP_W = 256
HEAD_DIM = 64
N_HEADS = 4
Q_LORA = 192
KV_LORA = 128
QK_ROPE = 32
QK_NOPE = 64
CONV_W = 31
CHUNK = 128
PAGE = 128
X_HEADS = 4
X_HEAD_DIM = 128
MLA_SCALE = (QK_NOPE + QK_ROPE) ** -0.5
PA_W = 512
PD_W = 1024
PAGES_PER_STEP = 128
SCAN_NB_PROMPT = 8
SCAN_NB_SAMPLE = 8


def _cparams(*sem):
    return pltpu.CompilerParams(dimension_semantics=sem, vmem_limit_bytes=VMEM_LIMIT_BYTES)


def _dot(a, b):
    return jnp.dot(a, b, preferred_element_type=F32)


def _dot_nt(a, b):
    return lax.dot_general(a, b, (((1,), (1,)), ((), ())), preferred_element_type=F32)


def _seg_sum(x, e):
    return _dot(x.astype(BF16), e)


def _rms(x, width=None):
    w = x.shape[-1] if width is None else width
    return x * lax.rsqrt(jnp.sum(x * x, axis=-1, keepdims=True) * (1.0 / w) + EPS)


def _sigmoid(x):
    return 1.0 / (1.0 + jnp.exp(-x))


def _tile(n, pref):
    t = min(n, pref)
    while n % t:
        t //= 2
    return t


def _full(shape):
    nd = len(shape)
    return pl.BlockSpec(shape, lambda *a: (0,) * nd)


def _proj_in_kernel(x_ref, g_ref, w_ref, pa_ref, pb_ref, pc_ref, pd_ref):
    hb = (_rms(x_ref[...]) * g_ref[...]).astype(BF16)
    pa_ref[...] = _dot(hb, w_ref[:, 0:512])
    pb_ref[...] = _dot(hb, w_ref[:, 512:1024])
    pc_ref[...] = _dot(hb, w_ref[:, 1024:1536])
    pd_ref[...] = _dot(hb, w_ref[:, 1536:2560])


def proj_in(x, g, w):
    n, d = x.shape
    tm = _tile(n, 512)
    row = lambda wd: pl.BlockSpec((tm, wd), lambda i: (i, 0))
    return pl.pallas_call(
        _proj_in_kernel,
        grid=(n // tm,),
        in_specs=[row(d), _full(g.shape), _full(w.shape)],
        out_specs=[row(512), row(512), row(512), row(1024)],
        out_shape=[jax.ShapeDtypeStruct((n, wd), F32) for wd in (512, 512, 512, 1024)],
        compiler_params=_cparams("parallel"),
        name="proj_in",
    )(x, g, w)


def _rope128(x, c, sa, sb):
    w = x.shape[-1]
    return x * c + pltpu.roll(x, w - 16, 1) * sa + pltpu.roll(x, 16, 1) * sb


def _mla_prep_kernel(pa_ref, c_ref, sa_ref, sb_ref, gqn_ref, gkv_ref, wuq_ref, eq_ref, gq_ref,
                     gkr_ref, wuk_ref, ek_ref, gk_ref, wuv_ref,
                     q_ref, k_ref, v_ref, ckv_ref, kpe_ref):
    pa = pa_ref[...]
    c, sa, sb = c_ref[...], sa_ref[...], sb_ref[...]
    c4 = jnp.concatenate([c] * 4, axis=1)
    sa4 = jnp.concatenate([sa] * 4, axis=1)
    sb4 = jnp.concatenate([sb] * 4, axis=1)
    cq = _rms(pa[:, 0:256], Q_LORA) * gqn_ref[...]
    q = _dot(cq.astype(BF16), wuq_ref[...])
    qn = q * lax.rsqrt(_seg_sum(q * q, eq_ref[...]) + EPS) * gq_ref[...]
    q_ref[...] = (_rope128(qn, c4, sa4, sb4) * MLA_SCALE).astype(BF16)
    ckv = _rms(pa[:, 256:384]) * gkv_ref[...]
    ckv_ref[...] = ckv
    kp = _rms(pa[:, 384:512], QK_ROPE) * gkr_ref[...]
    kr = _rope128(kp, c, sa, sb)
    kpe_ref[...] = kr[:, 0:QK_ROPE]
    ckv_b = ckv.astype(BF16)
    kn = _dot(ckv_b, wuk_ref[...])
    kn = kn * lax.rsqrt(_seg_sum(kn * kn, ek_ref[...]) + EPS) * gk_ref[...]
    k_ref[...] = (kn + jnp.concatenate([kr] * 4, axis=1)).astype(BF16)
    v_ref[...] = _dot(ckv_b, wuv_ref[...]).astype(BF16)


def mla_prep(pa, tabs, wts, n_tab_blocks):
    n = pa.shape[0]
    c, sa, sb = tabs
    tm = c.shape[0] // n_tab_blocks
    assert n % tm == 0
    row = lambda wd: pl.BlockSpec((tm, wd), lambda i: (i, 0))
    tab = pl.BlockSpec((tm, LANES), lambda i: (i % n_tab_blocks, 0))
    return pl.pallas_call(
        _mla_prep_kernel,
        grid=(n // tm,),
        in_specs=[row(PA_W), tab, tab, tab] + [_full(w.shape) for w in wts],
        out_specs=[row(512), row(512), row(256), row(KV_LORA), row(QK_ROPE)],
        out_shape=[jax.ShapeDtypeStruct((n, 512), BF16), jax.ShapeDtypeStruct((n, 512), BF16),
                   jax.ShapeDtypeStruct((n, 256), BF16), jax.ShapeDtypeStruct((n, KV_LORA), F32),
                   jax.ShapeDtypeStruct((n, QK_ROPE), F32)],
        compiler_params=_cparams("parallel"),
        name="mla_prep",
    )(pa, c, sa, sb, *wts)


def _mla_attn_prompt_kernel(q_ref, k_ref, v_ref, o_ref, *, tq):
    i = pl.program_id(1)
    outs = []
    for h0 in range(0, N_HEADS, 2):
        heads = (h0, h0 + 1)

        def blk(j, carry, masked, heads=heads):
            off = pl.multiple_of(j * tq, tq)
            new = []
            for n, h in enumerate(heads):
                m, l, acc = carry[3 * n:3 * n + 3]
                kb = k_ref[pl.ds(off, tq), h * LANES:(h + 1) * LANES]
                vb = v_ref[pl.ds(off, tq), h * HEAD_DIM:(h + 1) * HEAD_DIM]
                s = _dot_nt(q_ref[:, h * LANES:(h + 1) * LANES], kb)
                if masked:
                    r = lax.broadcasted_iota(jnp.int32, (tq, tq), 0)
                    cc = lax.broadcasted_iota(jnp.int32, (tq, tq), 1)
                    s = jnp.where(cc <= r, s, NEG)
                m_new = jnp.maximum(m, jnp.max(s, axis=-1, keepdims=True))
                corr = jnp.exp(m - m_new)
                p = jnp.exp(s - m_new)
                l = l * corr + jnp.sum(p, axis=-1, keepdims=True)
                acc = acc * corr + _dot(p.astype(BF16), vb)
                new += [m_new, l, acc]
            return tuple(new)

        init = (jnp.full((tq, 1), NEG, F32), jnp.zeros((tq, 1), F32), jnp.zeros((tq, HEAD_DIM), F32)) * 2
        carry = lax.fori_loop(0, i, functools.partial(blk, masked=False), init)
        carry = blk(i, carry, True)
        outs += [carry[2] / carry[1], carry[5] / carry[4]]
    o_ref[...] = jnp.concatenate(outs, axis=1)


def mla_attn_prompt(q, k, v, b, t):
    tq = _tile(t, 512)
    nq = t // tq
    return pl.pallas_call(
        functools.partial(_mla_attn_prompt_kernel, tq=tq),
        grid=(b, nq),
        in_specs=[pl.BlockSpec((tq, 512), lambda bi, i: (bi * nq + i, 0)),
                  pl.BlockSpec((t, 512), lambda bi, i: (bi, 0)),
                  pl.BlockSpec((t, 256), lambda bi, i: (bi, 0))],
        out_specs=pl.BlockSpec((tq, 256), lambda bi, i: (bi * nq + i, 0)),
        out_shape=jax.ShapeDtypeStruct((b * t, 256), F32),
        compiler_params=_cparams("parallel", "arbitrary"),
        name="mla_attn_prompt",
    )(q, k, v)


SUB_PAGES = 4
PAGE_SLOTS = 2


def _mla_attn_sample_kernel(pt_ref, qbd_ref, qpe_ref, ckvn_ref, kpen_ref, wuk_ref, wukt_ref, wuv_ref, ckv_hbm,
                            kpe_hbm, o_ref, ckv_buf, kpe_buf, sem, m_scr, l_scr, acc_scr, lhs_scr, *, n_pg, tq,
                            layer):
    b_id = pl.program_id(0)
    p_id = pl.program_id(1)
    n_steps = pl.num_programs(1)
    step = b_id * n_steps + p_id
    n_total = pl.num_programs(0) * n_steps
    slot = step % PAGE_SLOTS

    def page_copies_of(st):
        return page_copies(st // n_steps, st % n_steps, st % PAGE_SLOTS)

    def page_copies(b, p, sl):
        copies = []
        for k in range(n_pg):
            phys = pt_ref[b, p * n_pg + k]
            copies.append(pltpu.make_async_copy(ckv_hbm.at[phys, layer], ckv_buf.at[sl, pl.ds(k * PAGE, PAGE), :],
                                                sem.at[sl, 0]))
            copies.append(pltpu.make_async_copy(kpe_hbm.at[phys, layer], kpe_buf.at[sl, :, pl.ds(k * PAGE, PAGE)],
                                                sem.at[sl, 1]))
        return copies

    for ahead in range(PAGE_SLOTS - 1):
        @pl.when((step == 0) & (ahead < n_total))
        def _(ahead=ahead):
            for c in page_copies_of(ahead):
                c.start()

    @pl.when(step + (PAGE_SLOTS - 1) < n_total)
    def _():
        for c in page_copies_of(step + (PAGE_SLOTS - 1)):
            c.start()

    qpe = qpe_ref[0]
    nr = N_HEADS * tq

    @pl.when(p_id == 0)
    def _():
        lhs_scr[0:GROUP_W, :] = wukt_ref[...]
        lhs_scr[GROUP_W:GROUP_W + nr, :] = _dot_nt(qbd_ref[0], wuk_ref[...]).astype(BF16)

    def scores(ckv, kpe_t):
        ckv_b = ckv.astype(BF16)
        out = _dot_nt(lhs_scr[...], ckv_b)
        rinv = []
        for h in range(N_HEADS):
            kn = out[h * HEAD_DIM:(h + 1) * HEAD_DIM]
            ss = jnp.sum(kn * kn, axis=0, keepdims=True) * (1.0 / HEAD_DIM)
            rinv.append(jnp.broadcast_to(lax.rsqrt(ss + EPS), (tq, ss.shape[1])))
        s = out[GROUP_W:GROUP_W + nr] * jnp.concatenate(rinv, axis=0) + _dot(qpe, kpe_t.astype(BF16))
        return s, ckv_b

    @pl.when(p_id == 0)
    def _():
        s, ckv_b = scores(ckvn_ref[0], kpen_ref[0])
        r = lax.broadcasted_iota(jnp.int32, s.shape, 0) % tq
        cc = lax.broadcasted_iota(jnp.int32, s.shape, 1)
        s = jnp.where(cc <= r, s, NEG)
        m = jnp.max(s, axis=-1, keepdims=True)
        p = jnp.exp(s - m)
        m_scr[...] = m
        l_scr[...] = jnp.sum(p, axis=-1, keepdims=True)
        acc_scr[...] = _dot(p.astype(BF16), ckv_b)

    pltpu.make_async_copy(ckv_buf.at[slot], ckv_buf.at[slot], sem.at[slot, 0]).wait()
    pltpu.make_async_copy(kpe_buf.at[slot], kpe_buf.at[slot], sem.at[slot, 1]).wait()

    n_sub = max(n_pg // SUB_PAGES, 1)
    keys = (n_pg // n_sub) * PAGE
    s_parts, ckv_parts = [], []
    for g in range(n_sub):
        s, ckv_b = scores(ckv_buf[slot, g * keys:(g + 1) * keys, :], kpe_buf[slot, :, g * keys:(g + 1) * keys])
        s_parts.append(s)
        ckv_parts.append(ckv_b)
    m = m_scr[...]
    m_new = m
    for s in s_parts:
        m_new = jnp.maximum(m_new, jnp.max(s, axis=-1, keepdims=True))
    corr = jnp.exp(m - m_new)
    l = l_scr[...] * corr
    acc = acc_scr[...] * corr
    for s, ckv_b in zip(s_parts, ckv_parts):
        p = jnp.exp(s - m_new)
        l = l + jnp.sum(p, axis=-1, keepdims=True)
        acc = acc + _dot(p.astype(BF16), ckv_b)
    m_scr[...] = m_new
    l_scr[...] = l
    acc_scr[...] = acc

    @pl.when(p_id == n_steps - 1)
    def _():
        lat = (acc / l).astype(BF16)
        full = _dot(lat, wuv_ref[...])
        lane_head = lax.broadcasted_iota(jnp.int32, (tq, 256), 1) // HEAD_DIM
        out = jnp.zeros((tq, 256), F32)
        for h in range(N_HEADS):
            out = jnp.where(lane_head == h, full[h * tq:(h + 1) * tq, :], out)
        o_ref[0] = out


def mla_attn_sample(page_table, qbd, qpe, ckv_new, kpe_new_t, wuk, wukt, wuv, cache_ckv, cache_kpe_t, layer,
                    pages_per_step):
    bs, n_pages = page_table.shape
    tq = qbd.shape[1] // N_HEADS
    n_pg = min(pages_per_step, n_pages)
    assert n_pages % n_pg == 0
    nr = N_HEADS * tq
    per_b = lambda shp: pl.BlockSpec((1,) + shp, lambda b, p, pt: (b, 0, 0))
    cst = lambda shp: pl.BlockSpec(shp, lambda b, p, pt: (0,) * len(shp))
    hbm = pl.BlockSpec(memory_space=pl.ANY)
    grid_spec = pltpu.PrefetchScalarGridSpec(
        num_scalar_prefetch=1,
        grid=(bs, n_pages // n_pg),
        in_specs=[per_b((nr, 256)), per_b((nr, QK_ROPE)), per_b((PAGE, KV_LORA)), per_b((QK_ROPE, PAGE)),
                  cst(wuk.shape), cst(wukt.shape), cst(wuv.shape), hbm, hbm],
        out_specs=pl.BlockSpec((1, tq, 256), lambda b, p, pt: (b, 0, 0)),
        scratch_shapes=[pltpu.VMEM((PAGE_SLOTS, n_pg * PAGE, KV_LORA), F32),
                        pltpu.VMEM((PAGE_SLOTS, QK_ROPE, n_pg * PAGE), F32),
                        pltpu.SemaphoreType.DMA((PAGE_SLOTS, 2)),
                        pltpu.VMEM((nr, 1), F32), pltpu.VMEM((nr, 1), F32), pltpu.VMEM((nr, KV_LORA), F32),
                        pltpu.VMEM((GROUP_W + nr, KV_LORA), BF16)],
    )
    return pl.pallas_call(
        functools.partial(_mla_attn_sample_kernel, n_pg=n_pg, tq=tq, layer=layer),
        grid_spec=grid_spec,
        out_shape=jax.ShapeDtypeStruct((bs, tq, 256), F32),
        compiler_params=_cparams("arbitrary", "arbitrary"),
        name="mla_attn_sample",
    )(page_table, qbd, qpe, ckv_new, kpe_new_t, wuk, wukt, wuv, cache_ckv, cache_kpe_t)


CONV_HALO = 32


def _conv_kernel(pb_ref, st_ref, cw_ref, cb_ref, e_ref, g_ref, b_ref, pw_ref, o_ref, st_out_ref, xbuf, *, tt, n_seq):
    j = pl.program_id(1)
    lo = CONV_HALO - (CONV_W - 1)

    @pl.when(j == 0)
    def _():
        xbuf[:, pl.ds(lo, CONV_W - 1), :] = st_ref[...]
        xbuf[:, pl.ds(CONV_HALO + tt, SUBLANES), :] = jnp.zeros((n_seq, SUBLANES, 256), F32)

    pb = pb_ref[...]
    glu = pb[:, 0:256] * _sigmoid(pb[:, 256:512])
    ys = []
    for s in range(n_seq):
        xbuf[s, pl.ds(CONV_HALO, tt), :] = glu[s * tt:(s + 1) * tt]
        y = jnp.zeros((tt, 256), F32) + cb_ref[...]
        for r in range(SUBLANES):
            z = None
            for k in range(CONV_W):
                if (lo + k) % SUBLANES == r:
                    term = xbuf[s, pl.ds(lo + k - r, tt + SUBLANES), :] * cw_ref[pl.ds(k, 1), :]
                    z = term if z is None else z + term
            y = y + z[r:r + tt]
        new_state = xbuf[s, pl.ds(lo + tt, CONV_W - 1), :]
        xbuf[s, pl.ds(lo, CONV_W - 1), :] = new_state
        st_out_ref[s] = new_state
        ys.append(y)
    y = jnp.concatenate(ys, axis=0) if n_seq > 1 else ys[0]
    e = e_ref[...]
    yc = y - _seg_sum(y, e)
    yn = yc * lax.rsqrt(_seg_sum(yc * yc, e) + LN_EPS) * g_ref[...] + b_ref[...]
    act = yn * _sigmoid(yn)
    o_ref[...] = _dot(act.astype(BF16), pw_ref[...])


SHORT_SEQ_ROWS = 128


def _seq_tiling(b, t):
    if t >= SHORT_SEQ_ROWS:
        return 1, _tile(t, 256)
    return _tile(b, SHORT_SEQ_ROWS // t), t


def conv_module(pb, state, wts, b, t):
    n_seq, tt = _seq_tiling(b, t)
    nt = t // tt
    return pl.pallas_call(
        functools.partial(_conv_kernel, tt=tt, n_seq=n_seq),
        grid=(b // n_seq, nt),
        in_specs=[pl.BlockSpec((n_seq * tt, 512), lambda bi, j: (bi * nt + j, 0)),
                  pl.BlockSpec((n_seq, CONV_W - 1, 256), lambda bi, j: (bi, 0, 0))]
                 + [_full(w.shape) for w in wts],
        out_specs=[pl.BlockSpec((n_seq * tt, 256), lambda bi, j: (bi * nt + j, 0)),
                   pl.BlockSpec((n_seq, CONV_W - 1, 256), lambda bi, j: (bi, 0, 0))],
        out_shape=[jax.ShapeDtypeStruct((b * t, 256), F32), jax.ShapeDtypeStruct((b, CONV_W - 1, 256), F32)],
        scratch_shapes=[pltpu.VMEM((n_seq, CONV_HALO + tt + SUBLANES, 256), F32)],
        compiler_params=_cparams("parallel", "arbitrary"),
        name="conv_module",
    )(pb, state, *wts)


def _sgu_kernel(pc_ref, g_ref, b_ref, w_ref, bias_ref, o_ref, v_ref, *, n_chunks):
    x = pc_ref[...]
    z = 0.5 * x * (1.0 + jnp.tanh(0.7978845608028654 * (x + 0.044715 * (x * x * x))))
    u = z[:, 0:256]
    v = z[:, 256:512]
    vc = v - jnp.mean(v, axis=-1, keepdims=True)
    v = vc * lax.rsqrt(jnp.mean(vc * vc, axis=-1, keepdims=True) + LN_EPS) * g_ref[...] + b_ref[...]
    v_ref[...] = v
    lane_head = lax.broadcasted_iota(jnp.int32, (CHUNK, 256), 1) // HEAD_DIM
    for c in range(n_chunks):
        vcb = v[c * CHUNK:(c + 1) * CHUNK, :]
        sv = bias_ref[...]
        for h in range(N_HEADS):
            sv = sv + _dot(w_ref[h], jnp.where(lane_head == h, vcb, 0.0).astype(BF16))
        o_ref[pl.ds(c * CHUNK, CHUNK), :] = u[c * CHUNK:(c + 1) * CHUNK, :] * sv


def sgu(pc, wts):
    n = pc.shape[0]
    tm = _tile(n, 512)
    row = lambda wd: pl.BlockSpec((tm, wd), lambda i: (i, 0))
    return pl.pallas_call(
        functools.partial(_sgu_kernel, n_chunks=tm // CHUNK),
        grid=(n // tm,),
        in_specs=[row(512)] + [_full(w.shape) for w in wts],
        out_specs=[row(256), row(256)],
        out_shape=[jax.ShapeDtypeStruct((n, 256), F32), jax.ShapeDtypeStruct((n, 256), F32)],
        compiler_params=_cparams("parallel"),
        name="sgu",
    )(pc, *wts)


RW_HALO = 8


def _rwkv_prep_kernel(pd_ref, sh_ref, mu_ref, w0_ref, w2_ref, a0_ref, a2_ref, g2_ref, kkp_ref, ka_ref, rk_ref,
                      e_ref, r_ref, w_ref, k_ref, v_ref, kk_ref, kka_ref, g_ref, bv_ref, xbuf, *, tt, n_seq):
    j = pl.program_id(1)

    @pl.when(j == 0)
    def _():
        xbuf[:, pl.ds(RW_HALO - 1, 1), :] = sh_ref[...]

    pd = pd_ref[...]
    prevs = []
    for s in range(n_seq):
        rows = pd[s * tt:(s + 1) * tt]
        xbuf[s, pl.ds(RW_HALO, tt), :] = rows
        prevs.append(xbuf[s, pl.ds(RW_HALO - 1, tt), :])
        xbuf[s, pl.ds(RW_HALO - 1, 1), :] = rows[tt - 1:tt, :]
    prev = jnp.concatenate(prevs, axis=0) if n_seq > 1 else prevs[0]
    xs = pd + (prev - pd) * mu_ref[...]
    r = xs[:, 0:256]
    k = xs[:, 256:512]
    v = xs[:, 512:768]
    xwa = xs[:, 768:896]
    xg = xs[:, 896:1024]
    z = -(w0_ref[...] + _dot(jnp.tanh(xwa).astype(BF16), w2_ref[...]))
    softplus = jnp.maximum(z, 0.0) + jnp.log(1.0 + jnp.exp(-jnp.abs(z)))
    w_ref[...] = jnp.exp(-jnp.exp(-softplus - 0.5))
    a = _sigmoid(a0_ref[...] + _dot(xwa.astype(BF16), a2_ref[...]))
    g_ref[...] = _dot(_sigmoid(xg).astype(BF16), g2_ref[...])
    e = e_ref[...]
    kk = k * kkp_ref[...]
    kk = kk * lax.rsqrt(_seg_sum(kk * kk, e) + 1e-12)
    k2 = k * (1.0 + (a - 1.0) * ka_ref[...])
    r_ref[...] = r
    k_ref[...] = k2
    v_ref[...] = v
    kk_ref[...] = kk
    kka_ref[...] = kk * a
    bv_ref[...] = _seg_sum(r * k2 * rk_ref[...], e) * v


def rwkv_prep(pd, shift, wts, b, t):
    n_seq, tt = _seq_tiling(b, t)
    nt = t // tt
    row = lambda wd: pl.BlockSpec((n_seq * tt, wd), lambda bi, j: (bi * nt + j, 0))
    return pl.pallas_call(
        functools.partial(_rwkv_prep_kernel, tt=tt, n_seq=n_seq),
        grid=(b // n_seq, nt),
        in_specs=[row(PD_W), pl.BlockSpec((n_seq, 1, PD_W), lambda bi, j: (bi, 0, 0))]
                 + [_full(w.shape) for w in wts],
        out_specs=[row(256)] * 8,
        out_shape=[jax.ShapeDtypeStruct((b * t, 256), F32)] * 8,
        scratch_shapes=[pltpu.VMEM((n_seq, RW_HALO + tt, PD_W), F32)],
        compiler_params=_cparams("parallel", "arbitrary"),
        name="rwkv_prep",
    )(pd, shift, *wts)


RW_BLOCK = 128


def _pack_bf16_pair(a, b):
    ua = lax.bitcast_convert_type(a.astype(BF16).astype(F32), jnp.uint32)
    ub = lax.bitcast_convert_type(b.astype(BF16).astype(F32), jnp.uint32)
    return lax.bitcast_convert_type(ua | (ub >> 16), jnp.int32)


def _unpack_bf16_pair(word):
    u = lax.bitcast_convert_type(word, jnp.uint32)
    return (lax.bitcast_convert_type(u & jnp.uint32(0xFFFF0000), F32),
            lax.bitcast_convert_type(u << 16, F32))


def _rwkv_scan_kernel(w_ref, kk_ref, kka_ref, k_ref, r_ref, v_ref, s0_ref, e_ref, y_ref, sf_ref, s_scr, col_scr, *,
                      nb, sblk):
    c = pl.program_id(1)

    @pl.when(c == 0)
    def _():
        s_scr[...] = s0_ref[...]

    lane = lax.broadcasted_iota(jnp.int32, (HEAD_DIM, LANES), 1)
    low = lane < HEAD_DIM
    n_half = 2 if sblk > HEAD_DIM else 1
    for b in range(nb):
        for p in range(2):
            blk = lambda ref: ref[b, :, p * LANES:(p + 1) * LANES]
            x = _pack_bf16_pair(blk(kk_ref), blk(kka_ref))
            if sblk < LANES:
                x = jnp.concatenate([x, jnp.zeros((LANES - sblk, LANES), jnp.int32)], axis=0)
            xt = x.T
            h0, h1 = xt[0:HEAD_DIM], xt[HEAD_DIM:LANES]
            col_scr[b, p, 0] = jnp.where(low, h0, pltpu.roll(h1, HEAD_DIM, 1))
            if n_half == 2:
                col_scr[b, p, 1] = jnp.where(low, pltpu.roll(h0, HEAD_DIM, 1), h1)

    base = jnp.where(low, 0, HEAD_DIM)
    diag = (lax.broadcasted_iota(jnp.int32, (HEAD_DIM, 256), 1) % HEAD_DIM
            == lax.broadcasted_iota(jnp.int32, (HEAD_DIM, 256), 0))
    for half in range(n_half):
        n_groups = min(sblk - half * HEAD_DIM, HEAD_DIM) // SUBLANES

        def group(gi, carry, half=half):
            row0 = pl.multiple_of(half * HEAD_DIM + gi * SUBLANES, SUBLANES)
            for b in range(nb):
                v8 = v_ref[b, pl.ds(row0, SUBLANES), :]
                w8 = w_ref[b, pl.ds(row0, SUBLANES), :]
                w8_hi = w8.astype(BF16).astype(F32)
                rows8 = (k_ref[b, pl.ds(row0, SUBLANES), :], r_ref[b, pl.ds(row0, SUBLANES), :], w8_hi, w8 - w8_hi)
                on_diag = [jnp.where(diag, jnp.broadcast_to(x8[i:i + 1, :], (HEAD_DIM, 256)), 0.0).astype(BF16)
                           for i in range(SUBLANES) for x8 in rows8]
                spread = _dot(jnp.concatenate(on_diag, axis=0), e_ref[...])
                ys = []
                for p in range(2):
                    st = s_scr[b, :, p * LANES:(p + 1) * LANES]
                    yp = []
                    for i in range(SUBLANES):
                        idx = base + (gi * SUBLANES + i)
                        kk, kka = _unpack_bf16_pair(jnp.take_along_axis(col_scr[b, p, half], idx, axis=1))
                        part = lambda n: spread[(4 * i + n) * HEAD_DIM:(4 * i + n + 1) * HEAD_DIM,
                                                p * LANES:(p + 1) * LANES]
                        k, r, w = part(0), part(1), part(2) + part(3)
                        vrow = v8[i:i + 1, p * LANES:(p + 1) * LANES]
                        sa = -jnp.sum(st * kk, axis=0, keepdims=True)
                        st = st * w + kka * sa + k * vrow
                        yp.append(jnp.sum(st * r, axis=0, keepdims=True))
                    s_scr[b, :, p * LANES:(p + 1) * LANES] = st
                    ys.append(jnp.concatenate(yp, axis=0))
                y_ref[b, pl.ds(row0, SUBLANES), :] = jnp.concatenate(ys, axis=1)
            return carry

        lax.fori_loop(0, n_groups, group, 0)

    @pl.when(c == pl.num_programs(1) - 1)
    def _():
        sf_ref[...] = s_scr[...]


def rwkv_scan(seqs, v, s0, nb):
    b, t, _ = v.shape
    e_heads = _head_matrix(1.0)
    sblk = min(t, RW_BLOCK)
    assert b % nb == 0 and t % sblk == 0 and sblk % SUBLANES == 0 and (sblk <= HEAD_DIM or sblk == RW_BLOCK)
    st_spec = pl.BlockSpec((nb, HEAD_DIM, 256), lambda bi, c: (bi, 0, 0))
    seq_spec = pl.BlockSpec((nb, sblk, 256), lambda bi, c: (bi, c, 0))
    return pl.pallas_call(
        functools.partial(_rwkv_scan_kernel, nb=nb, sblk=sblk),
        grid=(b // nb, t // sblk),
        in_specs=[seq_spec] * 6 + [st_spec, _full(e_heads.shape)],
        out_specs=[seq_spec, st_spec],
        out_shape=[jax.ShapeDtypeStruct((b, t, 256), F32), jax.ShapeDtypeStruct((b, HEAD_DIM, 256), F32)],
        scratch_shapes=[pltpu.VMEM((nb, HEAD_DIM, 256), F32),
                        pltpu.VMEM((nb, 2, 2, HEAD_DIM, LANES), jnp.int32)],
        compiler_params=_cparams("parallel", "arbitrary"),
        name="rwkv_scan",
    )(*seqs, v, s0, e_heads)


def _mix_out_kernel(x_ref, oa_ref, ob_ref, oc_ref, y_ref, bv_ref, g_ref, e_ref, lg_ref, lb_ref, on_ref, w_ref,
                    nx_ref, wq_ref, qg_ref, o_ref, q_ref):
    e = e_ref[...]
    y = y_ref[...]
    yc = y - _seg_sum(y, e)
    yn = yc * lax.rsqrt(_seg_sum(yc * yc, e) + RW_LN_EPS) * lg_ref[...] + lb_ref[...]
    od = (yn + bv_ref[...]) * g_ref[...]
    acc = x_ref[...]
    for gi, o in enumerate((oa_ref[...], ob_ref[...], oc_ref[...], od)):
        on = (_rms(o) * on_ref[:, gi * 256:(gi + 1) * 256]).astype(BF16)
        acc = acc + _dot(on, w_ref[pl.ds(gi * 256, 256), :])
    o_ref[...] = acc
    q = _dot((_rms(acc) * nx_ref[...]).astype(BF16), wq_ref[...])
    heads = [_rms(q[:, h * X_HEAD_DIM:(h + 1) * X_HEAD_DIM]) * qg_ref[...] * X_HEAD_DIM ** -0.5
             for h in range(X_HEADS)]
    q_ref[...] = jnp.concatenate(heads, axis=1).astype(q_ref.dtype)


def mix_out(x, oa, ob, oc, y, bv, g, wts, q_dtype):
    n, d = x.shape
    tm = _tile(n, 512)
    row = lambda wd: pl.BlockSpec((tm, wd), lambda i: (i, 0))
    return pl.pallas_call(
        _mix_out_kernel,
        grid=(n // tm,),
        in_specs=[row(d)] + [row(256)] * 6 + [_full(w.shape) for w in wts],
        out_specs=[row(d), row(512)],
        out_shape=[jax.ShapeDtypeStruct((n, d), F32), jax.ShapeDtypeStruct((n, 512), q_dtype)],
        compiler_params=_cparams("parallel"),
        name="mix_out",
    )(x, oa, ob, oc, y, bv, g, *wts)


XATTN_SHORT_ROWS = 64


def _xattn_kernel(x_ref, q_ref, k_ref, v_ref, wo_ref, o_ref, *, n_seq, rows, n_mem):
    seq_outs = []
    for s in range(n_seq):
        qs = [q_ref[s * rows:(s + 1) * rows, h * X_HEAD_DIM:(h + 1) * X_HEAD_DIM].astype(BF16)
              for h in range(X_HEADS)]
        if n_seq == 1:
            outs = []
            for h in range(X_HEADS):
                head_rows = pl.ds(h, n_mem, stride=X_HEADS)
                sc = _dot_nt(qs[h], k_ref[s, head_rows, :].astype(BF16))
                p = jnp.exp(sc - jnp.max(sc, axis=-1, keepdims=True))
                p = p / jnp.sum(p, axis=-1, keepdims=True)
                outs.append(_dot(p.astype(BF16), v_ref[s, head_rows, :].astype(BF16)))
        else:
            sc = _dot_nt(jnp.concatenate(qs, axis=0), k_ref[s].astype(BF16))
            row_head = lax.broadcasted_iota(jnp.int32, sc.shape, 0) // rows
            col_head = lax.broadcasted_iota(jnp.int32, sc.shape, 1) % X_HEADS
            sc = jnp.where(row_head == col_head, sc, NEG)
            p = jnp.exp(sc - jnp.max(sc, axis=-1, keepdims=True))
            p = p / jnp.sum(p, axis=-1, keepdims=True)
            o = _dot(p.astype(BF16), v_ref[s].astype(BF16))
            outs = [o[h * rows:(h + 1) * rows] for h in range(X_HEADS)]
        seq_outs.append(jnp.concatenate(outs, axis=1))
    xo = jnp.concatenate(seq_outs, axis=0).astype(BF16)
    o_ref[...] = x_ref[...] + _dot(xo, wo_ref[...])


def xattn(x, q, mem_k, mem_v, wo, b, t, kv_seq0):
    n, d = x.shape
    n_mem = mem_k.shape[1] // X_HEADS
    if t >= XATTN_SHORT_ROWS:
        n_seq, rows = 1, _tile(t, 512)
    else:
        n_seq, rows = _tile(b, XATTN_SHORT_ROWS // t), t
    nq = t // rows
    assert kv_seq0 % n_seq == 0
    row = lambda wd: pl.BlockSpec((n_seq * rows, wd), lambda bi, i: (bi * nq + i, 0))
    kv = pl.BlockSpec((n_seq, n_mem * X_HEADS, X_HEAD_DIM), lambda bi, i: (kv_seq0 // n_seq + bi, 0, 0))
    return pl.pallas_call(
        functools.partial(_xattn_kernel, n_seq=n_seq, rows=rows, n_mem=n_mem),
        grid=(b // n_seq, nq),
        in_specs=[row(d), row(512), kv, kv, _full(wo.shape)],
        out_specs=row(d),
        out_shape=jax.ShapeDtypeStruct((n, d), F32),
        compiler_params=_cparams("parallel", "arbitrary"),
        name="xattn",
    )(x, q, mem_k, mem_v, wo)


def _mem_kv_kernel(x_ref, g_ref, w_ref, hg_ref, k_ref, v_ref, *, tm):
    hb = (_rms(x_ref[...]) * g_ref[...]).astype(BF16)
    y = _dot(hb, w_ref[...])
    for h in range(X_HEADS):
        head_rows = pl.ds(h, tm, stride=X_HEADS)
        k_ref[head_rows, :] = _rms(y[:, h * X_HEAD_DIM:(h + 1) * X_HEAD_DIM]) * hg_ref[...]
        v_ref[head_rows, :] = y[:, (X_HEADS + h) * X_HEAD_DIM:(X_HEADS + h + 1) * X_HEAD_DIM]


def mem_kv(x, g, w, hg):
    n, d = x.shape
    tm = _tile(n, 512)
    out = pl.BlockSpec((tm * X_HEADS, X_HEAD_DIM), lambda i: (i, 0))
    return pl.pallas_call(
        functools.partial(_mem_kv_kernel, tm=tm),
        grid=(n // tm,),
        in_specs=[pl.BlockSpec((tm, d), lambda i: (i, 0)), _full(g.shape), _full(w.shape), _full(hg.shape)],
        out_specs=[out, out],
        out_shape=[jax.ShapeDtypeStruct((n * X_HEADS, X_HEAD_DIM), F32)] * 2,
        compiler_params=_cparams("parallel"),
        name="mem_kv",
    )(x, g, w, hg)


FFN_CHUNK = 256


def _ffn_kernel(x_ref, g_ref, wg_ref, wu_ref, wo_ref, o_ref, *, n_chunks):
    x = x_ref[...]
    hb = (_rms(x) * g_ref[...]).astype(BF16)
    acc = x
    for c in range(n_chunks):
        a = _dot(hb, wg_ref[c])
        u = _dot(hb, wu_ref[c])
        acc = acc + _dot((a * _sigmoid(a) * u).astype(BF16), wo_ref[c])
    o_ref[...] = acc


def ffn(x, g, wg, wu, wo):
    n, d = x.shape
    tm = _tile(n, 512)
    row = pl.BlockSpec((tm, d), lambda i: (i, 0))
    return pl.pallas_call(
        functools.partial(_ffn_kernel, n_chunks=wg.shape[0]),
        grid=(n // tm,),
        in_specs=[row, _full(g.shape), _full(wg.shape), _full(wu.shape), _full(wo.shape)],
        out_specs=row,
        out_shape=jax.ShapeDtypeStruct((n, d), F32),
        compiler_params=_cparams("parallel"),
        name="ffn",
    )(x, g, wg, wu, wo)


def _seg_matrix(seg_ids, seg_len):
    s = np.asarray(seg_ids)
    m = (s[:, None] == s[None, :]) & (s[:, None] >= 0)
    return jnp.asarray(m.astype(np.float32) / np.asarray(seg_len, np.float32)[None, :], BF16)


def _mla_segments():
    lane = np.arange(512)
    blk, off = lane // 128, lane % 128
    q_ids = np.where(off < 32, 2 * blk, np.where(off < 64, -1, 2 * blk + 1))
    q_len = np.where(off < 32, 32.0, 64.0)
    k_ids = np.where(off < 64, -1, blk)
    k_len = np.full(512, 64.0)
    return _seg_matrix(q_ids, q_len), _seg_matrix(k_ids, k_len)


def _head_matrix(scale_len):
    lane = np.arange(256)
    return _seg_matrix(lane // HEAD_DIM, np.full(256, scale_len))


def _rope_tables(pos):
    half = QK_ROPE // 2
    inv = jnp.power(ROPE_BASE, -jnp.arange(half, dtype=F32) / half)
    ang = pos.astype(F32)[:, None] * inv[None, :]
    cos, sin = jnp.cos(ang), jnp.sin(ang)
    n = pos.shape[0]
    z = lambda w: jnp.zeros((n, w), F32)
    c = jnp.concatenate([cos, cos, jnp.ones((n, LANES - QK_ROPE), F32)], axis=1)
    sa = jnp.concatenate([-sin, z(LANES - half)], axis=1)
    sb = jnp.concatenate([z(half), sin, z(LANES - QK_ROPE)], axis=1)
    return c, sa, sb


def _layer_weights(p):
    row = lambda v: v.reshape(1, -1).astype(F32)
    zc = lambda a, w: jnp.zeros((a.shape[0], w), a.dtype)
    w_in = p['w_in']
    c1, c2, c3 = 352, 352 + 512, 352 + 1024
    pa = w_in[:, :c1]
    w_in_p = jnp.concatenate(
        [pa[:, :Q_LORA], zc(pa, 64), pa[:, Q_LORA:Q_LORA + KV_LORA], pa[:, Q_LORA + KV_LORA:], zc(pa, 96),
         w_in[:, c1:c2], w_in[:, c2:c3], w_in[:, c3:]], axis=1).astype(BF16)

    wuq = p['mla_w_uq'].reshape(Q_LORA, N_HEADS, QK_NOPE + QK_ROPE)
    wuq = jnp.concatenate([wuq[:, :, QK_NOPE:], jnp.zeros((Q_LORA, N_HEADS, 32), F32), wuq[:, :, :QK_NOPE]], axis=2)
    wuq = jnp.concatenate([wuq.reshape(Q_LORA, 512), jnp.zeros((256 - Q_LORA, 512), F32)], axis=0).astype(BF16)
    wuk = p['mla_w_uk'].reshape(KV_LORA, N_HEADS, QK_NOPE)
    wuk_p = jnp.concatenate([jnp.zeros((KV_LORA, N_HEADS, 64), F32), wuk], axis=2).reshape(KV_LORA, 512).astype(BF16)
    blk = lambda a, b_, c_: jnp.tile(jnp.concatenate([a, b_, c_]), N_HEADS).reshape(1, 512)
    z32, z64 = jnp.zeros((32,), F32), jnp.zeros((64,), F32)
    gq = blk(p['mla_gq_rope'], z32, p['mla_gq_nope'])
    gk = blk(z32, z32, p['mla_gk_nope'])
    gqn = jnp.concatenate([p['mla_q_norm'], z64]).reshape(1, 256)
    gkr = jnp.concatenate([p['mla_gk_rope'], jnp.zeros((96,), F32)]).reshape(1, LANES)
    eq, ek = _mla_segments()
    mla = (gqn, row(p['mla_kv_norm']), wuq, eq, gq, gkr, wuk_p, ek, gk, p['mla_w_uv'].astype(BF16))

    e64 = _head_matrix(64.0)
    conv = (p['conv_w'], row(p['conv_b']), e64, row(p['conv_norm_g']), row(p['conv_norm_b']),
            p['conv_pw'].astype(BF16))

    z64r = jnp.zeros((64, GROUP_W), F32)
    rw = (row(p['rw_mu']),
          row(p['rw_w0']), jnp.concatenate([p['rw_w2'], z64r], axis=0).astype(BF16),
          row(p['rw_a0']), jnp.concatenate([z64r, p['rw_a2']], axis=0).astype(BF16),
          p['rw_g2'].astype(BF16), row(p['rw_kk']), row(p['rw_ka']), row(p['rw_rk']), _head_matrix(1.0))

    mix = (e64, row(p['rw_ln_g']), row(p['rw_ln_b']), row(p['out_norm']), p['w_out'].astype(BF16))

    d_ff = p['w_ffn_out'].shape[0]
    nck = d_ff // FFN_CHUNK
    d = w_in.shape[0]
    wg = p['w_ffn_in'][:, :d_ff].reshape(d, nck, FFN_CHUNK).transpose(1, 0, 2).astype(BF16)
    wu = p['w_ffn_in'][:, d_ff:].reshape(d, nck, FFN_CHUNK).transpose(1, 0, 2).astype(BF16)
    wo = p['w_ffn_out'].reshape(nck, FFN_CHUNK, d).astype(BF16)

    return dict(
        norm_mix=row(p['norm_mix']), w_in=w_in_p, mla=mla, conv=conv, rw=rw, mix=mix,
        wuk=p['mla_w_uk'].astype(BF16), wuv=p['mla_w_uv'].astype(BF16), gk_nope=p['mla_gk_nope'],
        sgu_ln=(row(p['sgu_norm_g']), row(p['sgu_norm_b'])), sgu_w=p['sgu_w'], sgu_b=p['sgu_b'],
        norm_x=row(p['norm_x']), wq=p['wq_x'].astype(BF16), xq_norm=row(p['xq_norm']),
        mem_norm=row(p['mem_norm']), wkv=jnp.concatenate([p['wk_x'], p['wv_x']], axis=1).astype(BF16),
        xk_norm=row(p['xk_norm']), wo_x=p['wo_x'].astype(BF16),
        norm_ffn=row(p['norm_ffn']), wg=wg, wu=wu, wo=wo)


def _sgu_weights(lw, t):
    l = min(t, CHUNK)
    w = lw['sgu_w'][:, :l, :l] * jnp.tril(jnp.ones((l, l), F32))
    reps = CHUNK // l
    if reps > 1:
        w = jnp.einsum('ab,hij->haibj', jnp.eye(reps, dtype=F32), w).reshape(N_HEADS, CHUNK, CHUNK)
    bias = jnp.tile(lw['sgu_b'][:, :l].T, (reps, 1))
    bias = jnp.repeat(bias, HEAD_DIM, axis=1)
    return lw['sgu_ln'] + (w.astype(BF16), bias)


def _trunk_layer(x, b, t, lw, tabs, n_tab_blocks, attend, mem_k, mem_v, kv_seq0, conv_state, shift_state, wkv_state,
                 scan_nb):
    pa, pb, pc, pd = proj_in(x, lw['norm_mix'], lw['w_in'])
    q, k, v, ckv, kpe = mla_prep(pa, tabs, lw['mla'], n_tab_blocks)
    oa = attend(q, k, v, ckv, kpe)
    ob, conv_new = conv_module(pb, conv_state, lw['conv'], b, t)
    oc, v_sgu = sgu(pc, _sgu_weights(lw, t))
    r, w, k2, vv, kk, kka, g, bv = rwkv_prep(pd, shift_state.reshape(b, 1, PD_W), lw['rw'], b, t)
    seqs = [a.reshape(b, t, 256) for a in (w, kk, kka, k2, r)]
    s0 = wkv_state.transpose(0, 3, 1, 2).reshape(b, HEAD_DIM, 256)
    y, s_fin = rwkv_scan(seqs, vv.reshape(b, t, 256), s0, scan_nb)
    wkv_new = s_fin.reshape(b, HEAD_DIM, N_HEADS, HEAD_DIM).transpose(0, 2, 3, 1)
    shift_new = pd.reshape(b, t, PD_W)[:, -1]
    q_dtype = BF16 if t % 16 == 0 else F32
    x, qx = mix_out(x, oa, ob, oc, y.reshape(b * t, 256), bv, g,
                    lw['mix'] + (lw['norm_x'], lw['wq'], lw['xq_norm']), q_dtype)
    x = xattn(x, qx, mem_k, mem_v, lw['wo_x'], b, t, kv_seq0)
    x = ffn(x, lw['norm_ffn'], lw['wg'], lw['wu'], lw['wo'])
    return x, ckv, kpe, conv_new, shift_new, wkv_new, v_sgu


def kernel(x_prompt, x_sample, mem_prompt, cache_ckv, cache_kpe, cache_mem_k, cache_mem_v, state_conv, state_shift, state_wkv, page_table, norm_mix, w_in, mla_q_norm, mla_kv_norm, mla_w_uq, mla_w_uk, mla_w_uv, mla_gq_nope, mla_gq_rope, mla_gk_nope, mla_gk_rope, conv_w, conv_b, conv_norm_g, conv_norm_b, conv_pw, sgu_norm_g, sgu_norm_b, sgu_w, sgu_b, rw_mu, rw_w0, rw_w2, rw_a0, rw_a2, rw_g2, rw_kk, rw_ka, rw_rk, rw_ln_g, rw_ln_b, out_norm, w_out, norm_x, mem_norm, wq_x, wk_x, wv_x, xq_norm, xk_norm, wo_x, norm_ffn, w_ffn_in, w_ffn_out):
    params = dict(
        norm_mix=norm_mix, w_in=w_in, mla_q_norm=mla_q_norm, mla_kv_norm=mla_kv_norm, mla_w_uq=mla_w_uq,
        mla_w_uk=mla_w_uk, mla_w_uv=mla_w_uv, mla_gq_nope=mla_gq_nope, mla_gq_rope=mla_gq_rope,
        mla_gk_nope=mla_gk_nope, mla_gk_rope=mla_gk_rope, conv_w=conv_w, conv_b=conv_b, conv_norm_g=conv_norm_g,
        conv_norm_b=conv_norm_b, conv_pw=conv_pw, sgu_norm_g=sgu_norm_g, sgu_norm_b=sgu_norm_b, sgu_w=sgu_w,
        sgu_b=sgu_b, rw_mu=rw_mu, rw_w0=rw_w0, rw_w2=rw_w2, rw_a0=rw_a0, rw_a2=rw_a2, rw_g2=rw_g2, rw_kk=rw_kk,
        rw_ka=rw_ka, rw_rk=rw_rk, rw_ln_g=rw_ln_g, rw_ln_b=rw_ln_b, out_norm=out_norm, w_out=w_out, norm_x=norm_x,
        mem_norm=mem_norm, wq_x=wq_x, wk_x=wk_x, wv_x=wv_x, xq_norm=xq_norm, xk_norm=xk_norm, wo_x=wo_x,
        norm_ffn=norm_ffn, w_ffn_in=w_ffn_in, w_ffn_out=w_ffn_out)
    depth = w_in.shape[0]
    bp, tp, d = x_prompt.shape
    bs, ts, _ = x_sample.shape
    n_mem = mem_prompt.shape[1]
    n_pages = page_table.shape[1]
    past_len = n_pages * PAGE

    tm_p = _tile(tp, 512)
    tabs_p = _rope_tables(jnp.arange(tp, dtype=jnp.int32))
    tm_s = _tile(bs * ts, 512)
    tabs_s = tuple(jnp.tile(a, (tm_s // ts, 1)) for a in _rope_tables(past_len + jnp.arange(ts, dtype=jnp.int32)))
    cache_kpe_t = jnp.swapaxes(cache_kpe, 2, 3)
    cache_k_rows = cache_mem_k.reshape(depth * bs, n_mem * X_HEADS, X_HEAD_DIM)
    cache_v_rows = cache_mem_v.reshape(depth * bs, n_mem * X_HEADS, X_HEAD_DIM)
    y_p = x_prompt.reshape(bp * tp, d)
    y_s = x_sample.reshape(bs * ts, d)
    mem_flat = mem_prompt.reshape(bp * n_mem, d)
    zeros_conv = jnp.zeros((bp, CONV_W - 1, GROUP_W), F32)
    zeros_shift = jnp.zeros((bp, PD_W), F32)
    zeros_wkv = jnp.zeros((bp, N_HEADS, HEAD_DIM, HEAD_DIM), F32)
    outs_p, outs_s, memk_l, memv_l = [], [], [], []
    for l in range(depth):
        lw = _layer_weights({k_: v_[l] for k_, v_ in params.items()})

        mk, mv = mem_kv(mem_flat, lw['mem_norm'], lw['wkv'], lw['xk_norm'])
        memk_l.append(mk.reshape(bp, n_mem, X_HEADS, X_HEAD_DIM))
        memv_l.append(mv.reshape(bp, n_mem, X_HEADS, X_HEAD_DIM))

        def attend_p(q, k, v, ckv, kpe):
            return mla_attn_prompt(q, k, v, bp, tp)

        kv_rows = n_mem * X_HEADS
        res = _trunk_layer(y_p, bp, tp, lw, tabs_p, tp // tm_p, attend_p, mk.reshape(bp, kv_rows, X_HEAD_DIM),
                           mv.reshape(bp, kv_rows, X_HEAD_DIM), 0, zeros_conv, zeros_shift, zeros_wkv,
                           _tile(bp, SCAN_NB_PROMPT))
        y_p = res[0]
        outs_p.append(res[1:])

        def attend_s(q, k, v, ckv, kpe, l=l, lw=lw):
            qf = q.astype(F32).reshape(bs, ts, N_HEADS, LANES)
            qn = (qf[..., 64:] * lw['gk_nope']).transpose(0, 2, 1, 3)
            qbd = jnp.einsum('bhtj,hg->bhtgj', qn, jnp.eye(N_HEADS, dtype=F32)).reshape(bs, N_HEADS * ts, 256)
            qpe = qf[..., :QK_ROPE].transpose(0, 2, 1, 3).reshape(bs, N_HEADS * ts, QK_ROPE)
            pad = lambda a: jnp.pad(a.reshape(bs, ts, -1), ((0, 0), (0, PAGE - ts), (0, 0)))
            o = mla_attn_sample(page_table, qbd.astype(BF16), qpe.astype(BF16), pad(ckv),
                                pad(kpe).transpose(0, 2, 1), lw['wuk'], lw['wuk'].T, lw['wuv'], cache_ckv,
                                cache_kpe_t, l, PAGES_PER_STEP)
            return o.reshape(bs * ts, 256)

        res = _trunk_layer(y_s, bs, ts, lw, tabs_s, 1, attend_s, cache_k_rows, cache_v_rows, l * bs,
                           state_conv[l], state_shift[l], state_wkv[l], _tile(bs, SCAN_NB_SAMPLE))
        y_s = res[0]
        outs_s.append(res[1:])

    n_pp = tp // PAGE
    stack = lambda outs, i, ax: jnp.stack([o[i] for o in outs], axis=ax)
    ckv_prompt = stack(outs_p, 0, 0).reshape(depth, bp, n_pp, PAGE, KV_LORA).transpose(1, 2, 0, 3, 4)
    kpe_prompt = stack(outs_p, 1, 0).reshape(depth, bp, n_pp, PAGE, QK_ROPE).transpose(1, 2, 0, 3, 4)
    ckv_sample = stack(outs_s, 0, 0).reshape(depth, bs, ts, KV_LORA).transpose(1, 0, 2, 3)
    kpe_sample = stack(outs_s, 1, 0).reshape(depth, bs, ts, QK_ROPE).transpose(1, 0, 2, 3)
    return (y_p.reshape(bp, tp, d), y_s.reshape(bs, ts, d), ckv_prompt, kpe_prompt, ckv_sample, kpe_sample,
            jnp.stack(memk_l, 0), jnp.stack(memv_l, 0),
            stack(outs_p, 2, 0), stack(outs_s, 2, 0), stack(outs_p, 3, 0), stack(outs_s, 3, 0),
            stack(outs_p, 4, 0), stack(outs_s, 4, 0),
            stack(outs_s, 5, 0).reshape(depth, bs, ts, GROUP_W))
```
